```python
import math
import jax, jax.numpy as jnp
from jax import lax
import numpy as np

D_MODEL = 4096
BATCH = 2
SEQ = 4096
DEPTH = 2

N_MIXERS = 2
N_ATTN_LAYERS = (DEPTH + 1) // 2
N_SSM_LAYERS = DEPTH // 2

HEAD_DIM = 64
N_Q_HEADS = D_MODEL // HEAD_DIM
N_KV_HEADS = 8
Q_PER_KV = N_Q_HEADS // N_KV_HEADS
ATTN_WIDTH = N_Q_HEADS * HEAD_DIM
KV_WIDTH = N_KV_HEADS * HEAD_DIM
ATTN_IN_WIDTH = 2 * ATTN_WIDTH + 2 * KV_WIDTH
WINDOW = 128
BLOCK = 128

SSM_EXPAND = 2
SSM_WIDTH = SSM_EXPAND * D_MODEL
GROUP_SIZE = 16
N_GROUPS = SSM_WIDTH // GROUP_SIZE
STATE_DIM = 64
SCAN_CHUNK = 128
DT_MIN = 0.001
DT_MAX = 0.1

RMS_EPS = 1e-6
NEG_INF = -1e30

kernel_name = "hybrid_swa_sink_alibi_s5_interleaved"


def rms_norm(x, g):
    xf = x.astype(jnp.float32)
    var = jnp.mean(xf * xf, axis=-1, keepdims=True)
    return (xf * lax.rsqrt(var + RMS_EPS) * g.astype(jnp.float32)).astype(x.dtype)


def alibi_slopes(n_heads):
    return jnp.exp2(-8.0 * jnp.arange(1, n_heads + 1, dtype=jnp.float32) / n_heads)


def attn_mixer(h, w_in, q_g, k_g, sinks, w_out):
    bsz, seq, _ = h.shape
    proj = h @ w_in
    q, k, v, gate = jnp.split(
        proj, [ATTN_WIDTH, ATTN_WIDTH + KV_WIDTH, ATTN_WIDTH + 2 * KV_WIDTH], axis=-1)
    q = rms_norm(q.reshape(bsz, seq, N_Q_HEADS, HEAD_DIM), q_g).astype(jnp.float32) * (HEAD_DIM ** -0.5)
    k = rms_norm(k.reshape(bsz, seq, N_KV_HEADS, HEAD_DIM), k_g).astype(jnp.float32)
    v = v.reshape(bsz, seq, N_KV_HEADS, HEAD_DIM).astype(jnp.float32)

    nb = seq // BLOCK
    qb = q.reshape(bsz, nb, BLOCK, N_KV_HEADS, Q_PER_KV, HEAD_DIM).transpose(1, 0, 3, 4, 2, 5)

    def band(t):
        tp = jnp.pad(t, ((0, 0), (BLOCK, 0), (0, 0), (0, 0)))
        prev = tp[:, :seq].reshape(bsz, nb, BLOCK, N_KV_HEADS, HEAD_DIM)
        cur = tp[:, BLOCK:].reshape(bsz, nb, BLOCK, N_KV_HEADS, HEAD_DIM)
        return jnp.concatenate([prev, cur], axis=2).transpose(1, 0, 3, 2, 4)

    kb, vb = band(k), band(v)
    slopes = alibi_slopes(N_Q_HEADS).reshape(N_KV_HEADS, Q_PER_KV)
    qi = jnp.arange(BLOCK)[:, None]
    kj = jnp.arange(2 * BLOCK)[None, :]
    dist = BLOCK + qi - kj
    in_window = (dist >= 0) & (dist < WINDOW)
    alibi = -slopes[:, :, None, None] * dist.astype(jnp.float32)
    sink = sinks.astype(jnp.float32).reshape(N_KV_HEADS, Q_PER_KV)[:, :, None, None]

    def one_block(args):
        n, qn, kn, vn = args
        valid = in_window & ((n - 1) * BLOCK + kj >= 0)
        s = jnp.einsum('bkgqd,bksd->bkgqs', qn, kn) + alibi
        s = jnp.where(valid, s, NEG_INF)
        m = jnp.maximum(jnp.max(s, axis=-1, keepdims=True), sink)
        p = jnp.exp(s - m)
        denom = jnp.sum(p, axis=-1, keepdims=True) + jnp.exp(sink - m)
        return jnp.einsum('bkgqs,bksd->bkgqd', p, vn) / denom

    o = lax.map(one_block, (jnp.arange(nb), qb, kb, vb))
    o = o.transpose(1, 0, 4, 2, 3, 5).reshape(bsz, seq, ATTN_WIDTH).astype(h.dtype)
    return (o * jax.nn.silu(gate)) @ w_out


def _scan_combine(e1, e2):
    a1, b1 = e1
    a2, b2 = e2
    return (a1 * a2, a2 * b1 + b2)


def ssm_mixer(h, w_in, log_dt, lam_re, lam_im, b_re, b_im, c_re, c_im, d_skip, w_glu, w_out):
    bsz, seq, _ = h.shape
    u, gate = jnp.split(h @ w_in, 2, axis=-1)
    uf = u.astype(jnp.float32)
    lam = lax.complex(lam_re.astype(jnp.float32), lam_im.astype(jnp.float32))
    dt = jnp.exp(log_dt.astype(jnp.float32))[:, None]
    a_bar = jnp.exp(lam * dt)
    b = lax.complex(b_re.astype(jnp.float32), b_im.astype(jnp.float32))
    b_bar = ((a_bar - 1.0) / lam)[..., None] * b
    b_bar_re, b_bar_im = jnp.real(b_bar), jnp.imag(b_bar)
    c = lax.complex(c_re.astype(jnp.float32), c_im.astype(jnp.float32))

    nc = seq // SCAN_CHUNK
    u_chunks = uf.reshape(bsz, nc, SCAN_CHUNK, N_GROUPS, GROUP_SIZE).transpose(1, 0, 2, 3, 4)

    def chunk_step(carry, u_c):
        bu = lax.complex(jnp.einsum('btgc,gpc->btgp', u_c, b_bar_re),
                         jnp.einsum('btgc,gpc->btgp', u_c, b_bar_im))
        a = jnp.broadcast_to(a_bar, bu.shape)
        a_cum, h_loc = lax.associative_scan(_scan_combine, (a, bu), axis=1)
        states = h_loc + a_cum * carry[:, None]
        y = jnp.real(jnp.einsum('btgp,gcp->btgc', states, c))
        return states[:, -1], y

    carry0 = jnp.zeros((bsz, N_GROUPS, STATE_DIM), dtype=jnp.complex64)
    _, ys = lax.scan(chunk_step, carry0, u_chunks)
    y = ys.transpose(1, 0, 2, 3, 4).reshape(bsz, seq, SSM_WIDTH) + d_skip.astype(jnp.float32) * uf
    y = jax.nn.gelu(y)
    y = y * jax.nn.sigmoid(y @ w_glu.astype(jnp.float32))
    y = y.astype(h.dtype) * jax.nn.silu(gate)
    return y @ w_out


def setup_inputs(seed: int = 0) -> dict:
    key = jax.random.key(seed)
    ks = jax.random.split(key, 20)
    f32 = jnp.float32
    na, ns = N_ATTN_LAYERS, N_SSM_LAYERS
    nrm = lambda k, shape, s: jax.random.normal(k, shape, f32) * s
    x = jax.random.normal(ks[0], (BATCH, SEQ, D_MODEL), f32)
    norm_g = 1.0 + nrm(ks[1], (DEPTH, D_MODEL), 0.01)
    attn_w_in = nrm(ks[2], (na, D_MODEL, ATTN_IN_WIDTH), D_MODEL ** -0.5)
    attn_q_norm_g = 1.0 + nrm(ks[3], (na, HEAD_DIM), 0.01)
    attn_k_norm_g = 1.0 + nrm(ks[4], (na, HEAD_DIM), 0.01)
    attn_sinks = nrm(ks[5], (na, N_Q_HEADS), 0.5)
    attn_w_out = nrm(ks[6], (na, ATTN_WIDTH, D_MODEL), ATTN_WIDTH ** -0.5)
    ssm_w_in = nrm(ks[7], (ns, D_MODEL, 2 * SSM_WIDTH), D_MODEL ** -0.5)
    ssm_log_dt = jax.random.uniform(ks[8], (ns, N_GROUPS), f32,
                                    math.log(DT_MIN), math.log(DT_MAX))
    ssm_lam_re = -0.5 + nrm(ks[9], (ns, N_GROUPS, STATE_DIM), 0.01)
    ssm_lam_im = (math.pi * jnp.arange(STATE_DIM, dtype=f32))[None, None, :] + nrm(
        ks[10], (ns, N_GROUPS, STATE_DIM), 0.01)
    ssm_b_re = nrm(ks[11], (ns, N_GROUPS, STATE_DIM, GROUP_SIZE), (2 * GROUP_SIZE) ** -0.5)
    ssm_b_im = nrm(ks[12], (ns, N_GROUPS, STATE_DIM, GROUP_SIZE), (2 * GROUP_SIZE) ** -0.5)
    ssm_c_re = nrm(ks[13], (ns, N_GROUPS, GROUP_SIZE, STATE_DIM), STATE_DIM ** -0.5)
    ssm_c_im = nrm(ks[14], (ns, N_GROUPS, GROUP_SIZE, STATE_DIM), STATE_DIM ** -0.5)
    ssm_d = nrm(ks[15], (ns, SSM_WIDTH), 1.0)
    ssm_w_glu = nrm(ks[16], (ns, SSM_WIDTH, SSM_WIDTH), SSM_WIDTH ** -0.5)
    ssm_w_out = nrm(ks[17], (ns, SSM_WIDTH, D_MODEL), SSM_WIDTH ** -0.5)
    return {"x": x, "norm_g": norm_g,
            "attn_w_in": attn_w_in, "attn_q_norm_g": attn_q_norm_g, "attn_k_norm_g": attn_k_norm_g,
            "attn_sinks": attn_sinks, "attn_w_out": attn_w_out,
            "ssm_w_in": ssm_w_in, "ssm_log_dt": ssm_log_dt, "ssm_lam_re": ssm_lam_re,
            "ssm_lam_im": ssm_lam_im, "ssm_b_re": ssm_b_re, "ssm_b_im": ssm_b_im,
            "ssm_c_re": ssm_c_re, "ssm_c_im": ssm_c_im, "ssm_d": ssm_d,
            "ssm_w_glu": ssm_w_glu, "ssm_w_out": ssm_w_out}


def reference(x, norm_g, attn_w_in, attn_q_norm_g, attn_k_norm_g, attn_sinks, attn_w_out,
              ssm_w_in, ssm_log_dt, ssm_lam_re, ssm_lam_im, ssm_b_re, ssm_b_im,
              ssm_c_re, ssm_c_im, ssm_d, ssm_w_glu, ssm_w_out):
    h = x
    for i in range(DEPTH):
        hn = rms_norm(h, norm_g[i])
        j = i // N_MIXERS
        if i % N_MIXERS == 0:
            out = attn_mixer(hn, attn_w_in[j], attn_q_norm_g[j], attn_k_norm_g[j],
                             attn_sinks[j], attn_w_out[j])
        else:
            out = ssm_mixer(hn, ssm_w_in[j], ssm_log_dt[j], ssm_lam_re[j], ssm_lam_im[j],
                            ssm_b_re[j], ssm_b_im[j], ssm_c_re[j], ssm_c_im[j], ssm_d[j],
                            ssm_w_glu[j], ssm_w_out[j])
        h = h + out.astype(h.dtype)
    return h
```

```python
import functools
import math

import jax
import jax.numpy as jnp
from jax import lax
from jax.experimental import pallas as pl
from jax.experimental.pallas import tpu as pltpu

F32 = jnp.float32
BF16 = jnp.bfloat16

D_MODEL = 4096
HEAD_DIM = 64
N_Q_HEADS = 64
N_KV_HEADS = 8
Q_PER_KV = 8
KV_WIDTH = N_KV_HEADS * HEAD_DIM
GROUP_W = Q_PER_KV * HEAD_DIM
BLOCK = 128
SSM_WIDTH = 8192
GROUP_SIZE = 16
N_GROUPS = 512
STATE_DIM = 64
CHUNK = 16
CHUNK_W = CHUNK * GROUP_SIZE
RMS_EPS = 1e-6
NEG_INF = -1e30

VMEM_LIMIT = 56 * 1024 * 1024
MXU_W = 256


def _params(n_grid_dims):
    return pltpu.CompilerParams(
        dimension_semantics=("arbitrary",) * n_grid_dims, vmem_limit_bytes=VMEM_LIMIT)


def _rmsnorm_kernel(x_ref, g_ref, o_ref):
    x = x_ref[...]
    var = jnp.mean(x * x, axis=-1, keepdims=True)
    o_ref[...] = (x * lax.rsqrt(var + RMS_EPS) * g_ref[...]).astype(o_ref.dtype)


def _rmsnorm(x, g, tm=256):
    m, d = x.shape
    return pl.pallas_call(
        _rmsnorm_kernel,
        out_shape=jax.ShapeDtypeStruct((m, d), BF16),
        grid=(m // tm,),
        in_specs=[pl.BlockSpec((tm, d), lambda i: (i, 0)),
                  pl.BlockSpec((1, d), lambda i: (0, 0))],
        out_specs=pl.BlockSpec((tm, d), lambda i: (i, 0)),
        compiler_params=_params(1),
        name="rmsnorm",
    )(x, g.reshape(1, d))


def _head_rmsnorm(acc, ones_bd, gain):
    outs = []
    for c in range(acc.shape[1] // MXU_W):
        a = acc[:, c * MXU_W:(c + 1) * MXU_W]
        sq = a * a
        hi = sq.astype(BF16)
        lo = (sq - hi.astype(F32)).astype(BF16)
        ss = (jnp.dot(hi, ones_bd, preferred_element_type=F32)
              + jnp.dot(lo, ones_bd, preferred_element_type=F32))
        inv = lax.rsqrt(ss * (1.0 / HEAD_DIM) + RMS_EPS)
        outs.append(a * inv * gain[:, c * MXU_W:(c + 1) * MXU_W])
    return outs[0] if len(outs) == 1 else jnp.concatenate(outs, axis=1)


def _q_proj_kernel(x_ref, w_ref, e_ref, g_ref, o_ref):
    acc = jnp.dot(x_ref[...], w_ref[...], preferred_element_type=F32)
    o_ref[...] = _head_rmsnorm(acc, e_ref[...], g_ref[...]).astype(o_ref.dtype)


def _kv_proj_kernel(x_ref, w_ref, e_ref, g_ref, o_ref):
    acc = jnp.dot(x_ref[...], w_ref[...], preferred_element_type=F32)
    kn = _head_rmsnorm(acc[:, :KV_WIDTH], e_ref[...], g_ref[...])
    for h in range(N_KV_HEADS):
        o_ref[h, :, 0:HEAD_DIM] = kn[:, h * HEAD_DIM:(h + 1) * HEAD_DIM].astype(o_ref.dtype)
        o_ref[h, :, HEAD_DIM:2 * HEAD_DIM] = acc[
            :, KV_WIDTH + h * HEAD_DIM:KV_WIDTH + (h + 1) * HEAD_DIM].astype(o_ref.dtype)


def _cast_proj_kernel(x_ref, w_ref, o_ref):
    o_ref[...] = jnp.dot(x_ref[...], w_ref[...], preferred_element_type=F32).astype(o_ref.dtype)


def _ones_block_diag():
    r = jnp.arange(MXU_W) // HEAD_DIM
    return (r[:, None] == r[None, :]).astype(BF16)


def _q_proj(hn, wq, gain, tm=1024):
    m, k = hn.shape
    return pl.pallas_call(
        _q_proj_kernel,
        out_shape=jax.ShapeDtypeStruct((N_KV_HEADS, m, GROUP_W), BF16),
        grid=(m // tm, N_KV_HEADS),
        in_specs=[pl.BlockSpec((tm, k), lambda i, j: (i, 0)),
                  pl.BlockSpec((k, GROUP_W), lambda i, j: (0, j)),
                  pl.BlockSpec((MXU_W, MXU_W), lambda i, j: (0, 0)),
                  pl.BlockSpec((1, GROUP_W), lambda i, j: (0, 0))],
        out_specs=pl.BlockSpec((None, tm, GROUP_W), lambda i, j: (j, i, 0)),
        compiler_params=_params(2),
        name="attn_q_proj",
    )(hn, wq, _ones_block_diag(), gain)


def _kv_proj(hn, wkv, gain, tm=1024):
    m, k = hn.shape
    return pl.pallas_call(
        _kv_proj_kernel,
        out_shape=jax.ShapeDtypeStruct((N_KV_HEADS, m, 2 * HEAD_DIM), BF16),
        grid=(m // tm,),
        in_specs=[pl.BlockSpec((tm, k), lambda i: (i, 0)),
                  pl.BlockSpec((k, 2 * KV_WIDTH), lambda i: (0, 0)),
                  pl.BlockSpec((MXU_W, MXU_W), lambda i: (0, 0)),
                  pl.BlockSpec((1, KV_WIDTH), lambda i: (0, 0))],
        out_specs=pl.BlockSpec((N_KV_HEADS, tm, 2 * HEAD_DIM), lambda i: (0, i, 0)),
        compiler_params=_params(1),
        name="attn_kv_proj",
    )(hn, wkv, _ones_block_diag(), gain)


def _gate_proj(hn, wg, tm=1024):
    m, k = hn.shape
    return pl.pallas_call(
        _cast_proj_kernel,
        out_shape=jax.ShapeDtypeStruct((N_KV_HEADS, m, GROUP_W), BF16),
        grid=(m // tm, N_KV_HEADS),
        in_specs=[pl.BlockSpec((tm, k), lambda i, j: (i, 0)),
                  pl.BlockSpec((k, GROUP_W), lambda i, j: (0, j))],
        out_specs=pl.BlockSpec((None, tm, GROUP_W), lambda i, j: (j, i, 0)),
        compiler_params=_params(2),
        name="attn_gate_proj",
    )(hn, wg)


def _attn_kernel(slope_ref, sink_ref, q_ref, kvc_ref, kvp_ref, gate_ref, o_ref):
    n = pl.program_id(1)
    qi = lax.broadcasted_iota(jnp.int32, (BLOCK, BLOCK), 0)
    kj = lax.broadcasted_iota(jnp.int32, (BLOCK, BLOCK), 1)
    lower = kj <= qi
    dist = (qi - kj + jnp.where(lower, 0, BLOCK)).astype(F32)
    valid = jnp.logical_or(lower, n > 0)
    nt = (((1,), (1,)), ((), ()))

    def kv_head(kvh, carry):
        kvc = kvc_ref[kvh]
        kvp = kvp_ref[kvh]
        kc, vc = kvc[:, :HEAD_DIM], kvc[:, HEAD_DIM:]
        kp, vp = kvp[:, :HEAD_DIM], kvp[:, HEAD_DIM:]
        for g in range(Q_PER_KV):
            h = kvh * Q_PER_KV + g
            cols = slice(g * HEAD_DIM, (g + 1) * HEAD_DIM)
            qh = q_ref[kvh, :, cols]
            sc = lax.dot_general(qh, kc, nt, preferred_element_type=F32)
            sp = lax.dot_general(qh, kp, nt, preferred_element_type=F32)
            s = jnp.where(lower, sc, sp) - slope_ref[h] * dist
            s = jnp.where(valid, s, NEG_INF)
            sink = sink_ref[h]
            m = jnp.maximum(jnp.max(s, axis=-1, keepdims=True), sink)
            p = jnp.exp(s - m)
            denom = jnp.sum(p, axis=-1, keepdims=True) + jnp.exp(sink - m)
            pb = p.astype(BF16)
            zero = jnp.zeros_like(pb)
            o = (jnp.dot(jnp.where(lower, pb, zero), vc, preferred_element_type=F32)
                 + jnp.dot(jnp.where(lower, zero, pb), vp, preferred_element_type=F32)) / denom
            gt = gate_ref[kvh, :, cols].astype(F32)
            o_ref[kvh, :, cols] = (o * (gt * jax.nn.sigmoid(gt))).astype(o_ref.dtype)
        return carry

    lax.fori_loop(0, N_KV_HEADS, kv_head, 0)


def _attention(q, kv, gate, slopes, sinks, batch, seq):
    nb = seq // BLOCK
    m = batch * seq
    blk = lambda b, n: (0, b * nb + n, 0)
    prev = lambda b, n: (0, b * nb + jnp.maximum(n - 1, 0), 0)
    smem = pl.BlockSpec(memory_space=pltpu.SMEM)
    return pl.pallas_call(
        _attn_kernel,
        out_shape=jax.ShapeDtypeStruct((N_KV_HEADS, m, GROUP_W), BF16),
        grid=(batch, nb),
        in_specs=[smem, smem,
                  pl.BlockSpec((N_KV_HEADS, BLOCK, GROUP_W), blk),
                  pl.BlockSpec((N_KV_HEADS, BLOCK, 2 * HEAD_DIM), blk),
                  pl.BlockSpec((N_KV_HEADS, BLOCK, 2 * HEAD_DIM), prev),
                  pl.BlockSpec((N_KV_HEADS, BLOCK, GROUP_W), blk)],
        out_specs=pl.BlockSpec((N_KV_HEADS, BLOCK, GROUP_W), blk),
        compiler_params=_params(2),
        name="band_attention",
    )(slopes, sinks, q, kv, kv, gate)


def _attn_out_kernel(x_ref, w_ref, r_ref, o_ref):
    acc = r_ref[...]
    for h in range(N_KV_HEADS):
        acc = acc + jnp.dot(x_ref[h], w_ref[h], preferred_element_type=F32)
    o_ref[...] = acc


def _attn_out_proj(og, w, resid, tm=512, tn=1024):
    _, m, _ = og.shape
    n = w.shape[2]
    return pl.pallas_call(
        _attn_out_kernel,
        out_shape=jax.ShapeDtypeStruct((m, n), F32),
        grid=(m // tm, n // tn),
        in_specs=[pl.BlockSpec((N_KV_HEADS, tm, GROUP_W), lambda i, j: (0, i, 0)),
                  pl.BlockSpec((N_KV_HEADS, GROUP_W, tn), lambda i, j: (0, 0, j)),
                  pl.BlockSpec((tm, tn), lambda i, j: (i, j))],
        out_specs=pl.BlockSpec((tm, tn), lambda i, j: (i, j)),
        compiler_params=_params(2),
        name="attn_out_proj",
    )(og, w, resid)


def _cast_proj(x, w, tm=1024, tn=1024):
    m, k = x.shape
    n = w.shape[1]
    return pl.pallas_call(
        _cast_proj_kernel,
        out_shape=jax.ShapeDtypeStruct((m, n), BF16),
        grid=(m // tm, n // tn),
        in_specs=[pl.BlockSpec((tm, k), lambda i, j: (i, 0)),
                  pl.BlockSpec((k, tn), lambda i, j: (0, j))],
        out_specs=pl.BlockSpec((tm, tn), lambda i, j: (i, j)),
        compiler_params=_params(2),
        name="ssm_in_proj",
    )(x, w)


def _glu_kernel(x_ref, w_ref, y_ref, gate_ref, o_ref):
    z = jnp.dot(x_ref[...], w_ref[...], preferred_element_type=F32)
    y = y_ref[...].astype(F32)
    gt = gate_ref[...].astype(F32)
    o_ref[...] = (y * jax.nn.sigmoid(z) * (gt * jax.nn.sigmoid(gt))).astype(o_ref.dtype)


def _glu(y, w, ug, tm=512, tn=512):
    m, k = y.shape
    n = w.shape[1]
    gate_off = n // tn
    return pl.pallas_call(
        _glu_kernel,
        out_shape=jax.ShapeDtypeStruct((m, n), BF16),
        grid=(m // tm, n // tn),
        in_specs=[pl.BlockSpec((tm, k), lambda i, j: (i, 0)),
                  pl.BlockSpec((k, tn), lambda i, j: (0, j)),
                  pl.BlockSpec((tm, tn), lambda i, j: (i, j)),
                  pl.BlockSpec((tm, tn), lambda i, j: (i, j + gate_off))],
        out_specs=pl.BlockSpec((tm, tn), lambda i, j: (i, j)),
        compiler_params=_params(2),
        name="ssm_glu",
    )(y, w, y, ug)


def _resid_proj_kernel(x_ref, w_ref, r_ref, o_ref):
    o_ref[...] = r_ref[...] + jnp.dot(x_ref[...], w_ref[...], preferred_element_type=F32)


def _resid_proj(x, w, resid, tm=512, tn=512):
    m, k = x.shape
    n = w.shape[1]
    return pl.pallas_call(
        _resid_proj_kernel,
        out_shape=jax.ShapeDtypeStruct((m, n), F32),
        grid=(m // tm, n // tn),
        in_specs=[pl.BlockSpec((tm, k), lambda i, j: (i, 0)),
                  pl.BlockSpec((k, tn), lambda i, j: (0, j)),
                  pl.BlockSpec((tm, tn), lambda i, j: (i, j))],
        out_specs=pl.BlockSpec((tm, tn), lambda i, j: (i, j)),
        compiler_params=_params(2),
        name="ssm_out_proj",
    )(x, w, resid)


N_SCAN_STEPS = 8


def _ssm_prep_kernel(lr_ref, li_ref, ldt_ref, btr_ref, bti_ref, cr_ref, ci_ref,
                     m_ref, be_ref, cet_ref, apr_ref, api_ref, *, gb):
    tau = lax.broadcasted_iota(jnp.int32, (24, STATE_DIM), 0).astype(F32)
    kexp = lax.broadcasted_iota(jnp.int32, (N_SCAN_STEPS, STATE_DIM), 0)
    nsteps = (CHUNK * jnp.left_shift(1, kexp)).astype(F32)
    lane = lax.broadcasted_iota(jnp.int32, (GROUP_SIZE, CHUNK_W), 1)
    nt = (((1,), (1,)), ((), ()))

    def one_group(g, carry):
        lr = lr_ref[g]
        li = li_ref[g]
        dt = jnp.exp(ldt_ref[g])
        mag = jnp.exp(tau * (lr * dt))
        ang = tau * (li * dt)
        pr = mag * jnp.cos(ang)
        pi = mag * jnp.sin(ang)
        ar, ai = pr[1:2], pi[1:2]
        den = lr * lr + li * li
        xr = ar - 1.0
        wr = (xr * lr + ai * li) / den
        wi = (ai * lr - xr * li) / den
        btr, bti = btr_ref[g], bti_ref[g]
        br = btr * wr - bti * wi
        bi = btr * wi + bti * wr
        cr, ci = cr_ref[g], ci_ref[g]

        ca_r, ca_i = [], []
        for t in range(CHUNK + 1):
            ca_r.append(cr * pr[t:t + 1] - ci * pi[t:t + 1])
            ca_i.append(cr * pi[t:t + 1] + ci * pr[t:t + 1])

        rt = jnp.concatenate([jnp.concatenate(ca_r[:CHUNK], axis=0),
                              jnp.concatenate(ca_i[:CHUNK], axis=0)], axis=1)
        bcat = jnp.concatenate([br, -bi], axis=1)
        kp = lax.dot_general(bcat.astype(BF16), rt.astype(BF16), nt,
                             preferred_element_type=F32)
        for s in range(CHUNK):
            blk = kp if s == 0 else pltpu.roll(kp, GROUP_SIZE * s, 1)
            blk = jnp.where(lane >= GROUP_SIZE * s, blk, 0.0)
            m_ref[g, s * GROUP_SIZE:(s + 1) * GROUP_SIZE, :] = blk.astype(m_ref.dtype)
            q = CHUNK - 1 - s
            be_re = br * pr[q:q + 1] - bi * pi[q:q + 1]
            be_im = br * pi[q:q + 1] + bi * pr[q:q + 1]
            be_ref[g, s * GROUP_SIZE:(s + 1) * GROUP_SIZE, :] = jnp.concatenate(
                [be_re, be_im], axis=1).astype(be_ref.dtype)
            cet_ref[g, s * GROUP_SIZE:(s + 1) * GROUP_SIZE, :] = jnp.concatenate(
                [ca_r[s + 1], -ca_i[s + 1]], axis=1).astype(cet_ref.dtype)

        smag = jnp.exp(nsteps * (lr * dt))
        sang = nsteps * (li * dt)
        sr = smag * jnp.cos(sang)
        si = smag * jnp.sin(sang)
        apr_ref[g] = jnp.concatenate([sr, sr], axis=1)
        api_ref[g] = jnp.concatenate([-si, si], axis=1)
        return carry

    lax.fori_loop(0, gb, one_group, 0)


def _ssm_prep(lam_re, lam_im, log_dt, b_re, b_im, c_re, c_im, gb=8):
    g = N_GROUPS
    row = lambda a: a.reshape(g, 1, STATE_DIM)
    ldt = jnp.broadcast_to(log_dt.reshape(g, 1, 1), (g, 1, STATE_DIM))
    btr = jnp.swapaxes(b_re, 1, 2)
    bti = jnp.swapaxes(b_im, 1, 2)
    vec = pl.BlockSpec((gb, 1, STATE_DIM), lambda i: (i, 0, 0))
    mat = pl.BlockSpec((gb, GROUP_SIZE, STATE_DIM), lambda i: (i, 0, 0))
    return pl.pallas_call(
        functools.partial(_ssm_prep_kernel, gb=gb),
        out_shape=[jax.ShapeDtypeStruct((g, CHUNK_W, CHUNK_W), BF16),
                   jax.ShapeDtypeStruct((g, CHUNK_W, 2 * STATE_DIM), BF16),
                   jax.ShapeDtypeStruct((g, CHUNK_W, 2 * STATE_DIM), BF16),
                   jax.ShapeDtypeStruct((g, N_SCAN_STEPS, 2 * STATE_DIM), F32),
                   jax.ShapeDtypeStruct((g, N_SCAN_STEPS, 2 * STATE_DIM), F32)],
        grid=(g // gb,),
        in_specs=[vec, vec, vec, mat, mat, mat, mat],
        out_specs=[pl.BlockSpec((gb, CHUNK_W, CHUNK_W), lambda i: (i, 0, 0)),
                   pl.BlockSpec((gb, CHUNK_W, 2 * STATE_DIM), lambda i: (i, 0, 0)),
                   pl.BlockSpec((gb, CHUNK_W, 2 * STATE_DIM), lambda i: (i, 0, 0)),
                   pl.BlockSpec((gb, N_SCAN_STEPS, 2 * STATE_DIM), lambda i: (i, 0, 0)),
                   pl.BlockSpec((gb, N_SCAN_STEPS, 2 * STATE_DIM), lambda i: (i, 0, 0))],
        compiler_params=_params(1),
        name="ssm_prep",
    )(row(lam_re), row(lam_im), ldt, btr, bti, c_re, c_im)


SCAN_PAD = 128


def _ssm_kernel(uf_ref, m_ref, be_ref, cet_ref, apr_ref, api_ref, d_ref, y_ref, xs_ref,
                *, gb, batch, n_chunks):
    xs_ref[:, 0:SCAN_PAD, :] = jnp.zeros((batch, SCAN_PAD, 2 * STATE_DIM), F32)
    nt = (((1,), (1,)), ((), ()))

    def one_group(g, carry):
        u = uf_ref[g]
        y = jnp.dot(u, m_ref[g], preferred_element_type=F32)
        v = jnp.dot(u, be_ref[g], preferred_element_type=F32)
        apr = apr_ref[g]
        api = api_ref[g]
        entering = []
        for b in range(batch):
            x = v[b * n_chunks:(b + 1) * n_chunks]
            for k in range(N_SCAN_STEPS):
                d = 1 << k
                xs_ref[b, SCAN_PAD:SCAN_PAD + n_chunks, :] = x
                sh = xs_ref[b, SCAN_PAD - d:SCAN_PAD - d + n_chunks, :]
                x = x + sh * apr[k:k + 1] + pltpu.roll(sh, STATE_DIM, 1) * api[k:k + 1]
            xs_ref[b, SCAN_PAD:SCAN_PAD + n_chunks, :] = x
            entering.append(xs_ref[b, SCAN_PAD - 1:SCAN_PAD - 1 + n_chunks, :])
        e = jnp.concatenate(entering, axis=0).astype(BF16)
        y = y + lax.dot_general(e, cet_ref[g], nt, preferred_element_type=F32)
        y = y + d_ref[g] * u.astype(F32)
        y_ref[g] = jax.nn.gelu(y).astype(y_ref.dtype)
        return carry

    lax.fori_loop(0, gb, one_group, 0)


def _ssm(uf, m, be, cet, apr, api, dvec, batch, gb=4):
    g, rows, _ = uf.shape
    n_chunks = rows // batch
    blk = lambda last2: pl.BlockSpec((gb,) + last2, lambda i: (i, 0, 0))
    return pl.pallas_call(
        functools.partial(_ssm_kernel, gb=gb, batch=batch, n_chunks=n_chunks),
        out_shape=jax.ShapeDtypeStruct((g, rows, CHUNK_W), BF16),
        grid=(g // gb,),
        in_specs=[blk((rows, CHUNK_W)), blk((CHUNK_W, CHUNK_W)), blk((CHUNK_W, 2 * STATE_DIM)),
                  blk((CHUNK_W, 2 * STATE_DIM)), blk((N_SCAN_STEPS, 2 * STATE_DIM)),
                  blk((N_SCAN_STEPS, 2 * STATE_DIM)), blk((1, CHUNK_W))],
        out_specs=blk((rows, CHUNK_W)),
        scratch_shapes=[pltpu.VMEM((batch, SCAN_PAD + n_chunks, 2 * STATE_DIM), F32)],
        compiler_params=_params(1),
        name="ssm_chunked",
    )(uf, m, be, cet, apr, api, dvec)


def kernel(x, norm_g, attn_w_in, attn_q_norm_g, attn_k_norm_g, attn_sinks, attn_w_out, ssm_w_in,
           ssm_log_dt, ssm_lam_re, ssm_lam_im, ssm_b_re, ssm_b_im, ssm_c_re, ssm_c_im, ssm_d,
           ssm_w_glu, ssm_w_out):
    batch, seq, d = x.shape
    m = batch * seq
    assert d == D_MODEL and seq % BLOCK == 0 and seq // CHUNK == 1 << N_SCAN_STEPS
    x2 = x.reshape(m, d)

    w_in = attn_w_in[0]
    wq = w_in[:, :D_MODEL].astype(BF16)
    wkv = w_in[:, D_MODEL:D_MODEL + 2 * KV_WIDTH].astype(BF16)
    wg = w_in[:, D_MODEL + 2 * KV_WIDTH:].astype(BF16)
    q_gain = jnp.tile(attn_q_norm_g[0].astype(F32) * HEAD_DIM ** -0.5, Q_PER_KV).reshape(1, GROUP_W)
    k_gain = jnp.tile(attn_k_norm_g[0].astype(F32), N_KV_HEADS).reshape(1, KV_WIDTH)
    slopes = jnp.exp2(-8.0 * jnp.arange(1, N_Q_HEADS + 1, dtype=F32) / N_Q_HEADS)

    hn0 = _rmsnorm(x2, norm_g[0])
    q = _q_proj(hn0, wq, q_gain)
    kv = _kv_proj(hn0, wkv, k_gain)
    gate = _gate_proj(hn0, wg)
    og = _attention(q, kv, gate, slopes, attn_sinks[0].astype(F32), batch, seq)
    w_out = attn_w_out[0].astype(BF16).reshape(N_KV_HEADS, GROUP_W, D_MODEL)
    h1 = _attn_out_proj(og, w_out, x2)

    hn1 = _rmsnorm(h1, norm_g[1])
    ug = _cast_proj(hn1, ssm_w_in[0].astype(BF16))
    nj = m // CHUNK
    uf = ug[:, :SSM_WIDTH].reshape(nj, CHUNK, N_GROUPS, GROUP_SIZE)
    uf = uf.transpose(2, 0, 1, 3).reshape(N_GROUPS, nj, CHUNK_W)
    mt, be, cet, apr, api = _ssm_prep(ssm_lam_re[0], ssm_lam_im[0], ssm_log_dt[0],
                                      ssm_b_re[0], ssm_b_im[0], ssm_c_re[0], ssm_c_im[0])
    dvec = jnp.tile(ssm_d[0].astype(F32).reshape(N_GROUPS, 1, GROUP_SIZE), (1, 1, CHUNK))
    yf = _ssm(uf, mt, be, cet, apr, api, dvec, batch)
    y = yf.reshape(N_GROUPS, nj, CHUNK, GROUP_SIZE).transpose(1, 2, 0, 3).reshape(m, SSM_WIDTH)
    p = _glu(y, ssm_w_glu[0].astype(BF16), ug)
    out = _resid_proj(p, ssm_w_out[0].astype(BF16), h1)
    return out.reshape(batch, seq, d)
```

```python
import functools

import jax
import jax.numpy as jnp
from jax import lax
from jax.experimental import pallas as pl
from jax.experimental.pallas import tpu as pltpu

F32 = jnp.float32
BF16 = jnp.bfloat16

D_MODEL = 4096
HEAD_DIM = 64
N_Q_HEADS = 64
N_KV_HEADS = 8
Q_PER_KV = 8
N_PAIRS = Q_PER_KV // 2
KV_WIDTH = N_KV_HEADS * HEAD_DIM
GROUP_W = Q_PER_KV * HEAD_DIM
BLOCK = 128
STACK = Q_PER_KV * BLOCK
SSM_WIDTH = 8192
GROUP_SIZE = 16
N_GROUPS = 512
STATE_DIM = 64
CHUNK = 16
CHUNK_W = CHUNK * GROUP_SIZE
RMS_EPS = 1e-6
NEG_INF = -1e30

LANES = 128
VMEM_LIMIT = 56 * 1024 * 1024
MXU_W = 256
GROUPS_PER_BLOCK = LANES // GROUP_SIZE

NT = (((1,), (1,)), ((), ()))


def _lane_roll(x, shift):
    return jnp.concatenate([x[:, LANES - shift:], x[:, :LANES - shift]], axis=1)


def _params(n_grid_dims):
    return pltpu.CompilerParams(
        dimension_semantics=("arbitrary",) * n_grid_dims, vmem_limit_bytes=VMEM_LIMIT)


def _rms(x, g):
    var = jnp.mean(x * x, axis=-1, keepdims=True)
    return x * lax.rsqrt(var + RMS_EPS) * g


def _rmsnorm_kernel(x_ref, g_ref, o_ref):
    o_ref[...] = _rms(x_ref[...], g_ref[...]).astype(o_ref.dtype)


def _rmsnorm(x, g, tm=256):
    m, d = x.shape
    return pl.pallas_call(
        _rmsnorm_kernel,
        out_shape=jax.ShapeDtypeStruct((m, d), BF16),
        grid=(m // tm,),
        in_specs=[pl.BlockSpec((tm, d), lambda i: (i, 0)),
                  pl.BlockSpec((1, d), lambda i: (0, 0))],
        out_specs=pl.BlockSpec((tm, d), lambda i: (i, 0)),
        compiler_params=_params(1),
        name="rmsnorm",
    )(x, g.reshape(1, d))


def _rmsnorm_chunk_major_kernel(x_ref, g_ref, o_ref, cols_ref):
    hn = _rms(x_ref[...], g_ref[...])
    n_cols = hn.shape[1] // LANES
    nj = hn.shape[0] // CHUNK
    for c in range(n_cols):
        cols_ref[c] = hn[:, c * LANES:(c + 1) * LANES]
    for s in range(CHUNK):
        for c in range(n_cols):
            o_ref[s, :, c * LANES:(c + 1) * LANES] = cols_ref[
                c, pl.ds(s, nj, stride=CHUNK), :].astype(o_ref.dtype)


def _rmsnorm_chunk_major(x, g, tm=256):
    m, d = x.shape
    nj = tm // CHUNK
    out = pl.pallas_call(
        _rmsnorm_chunk_major_kernel,
        out_shape=jax.ShapeDtypeStruct((CHUNK, m // CHUNK, d), BF16),
        grid=(m // tm,),
        in_specs=[pl.BlockSpec((tm, d), lambda i: (i, 0)),
                  pl.BlockSpec((1, d), lambda i: (0, 0))],
        out_specs=pl.BlockSpec((CHUNK, nj, d), lambda i: (0, i, 0)),
        scratch_shapes=[pltpu.VMEM((d // LANES, tm, LANES), F32)],
        compiler_params=_params(1),
        name="rmsnorm_chunk_major",
    )(x, g.reshape(1, d))
    return out.reshape(m, d)


def _head_rmsnorm(acc, ones_bd, gain):
    outs = []
    for c in range(acc.shape[1] // MXU_W):
        a = acc[:, c * MXU_W:(c + 1) * MXU_W]
        sq = a * a
        hi = sq.astype(BF16)
        lo = (sq - hi.astype(F32)).astype(BF16)
        ss = (jnp.dot(hi, ones_bd, preferred_element_type=F32)
              + jnp.dot(lo, ones_bd, preferred_element_type=F32))
        inv = lax.rsqrt(ss * (1.0 / HEAD_DIM) + RMS_EPS)
        outs.append(a * inv * gain[:, c * MXU_W:(c + 1) * MXU_W])
    return outs[0] if len(outs) == 1 else jnp.concatenate(outs, axis=1)


def _q_proj_kernel(x_ref, w_ref, e_ref, g_ref, o_ref):
    acc = jnp.dot(x_ref[...], w_ref[...], preferred_element_type=F32)
    o_ref[...] = _head_rmsnorm(acc, e_ref[...], g_ref[...]).astype(o_ref.dtype)


def _kv_proj_kernel(x_ref, w_ref, e_ref, g_ref, o_ref):
    acc = jnp.dot(x_ref[...], w_ref[...], preferred_element_type=F32)
    kn = _head_rmsnorm(acc[:, :KV_WIDTH], e_ref[...], g_ref[...])
    for h in range(N_KV_HEADS):
        o_ref[h, :, 0:HEAD_DIM] = kn[:, h * HEAD_DIM:(h + 1) * HEAD_DIM].astype(o_ref.dtype)
        o_ref[h, :, HEAD_DIM:2 * HEAD_DIM] = acc[
            :, KV_WIDTH + h * HEAD_DIM:KV_WIDTH + (h + 1) * HEAD_DIM].astype(o_ref.dtype)


def _cast_proj_kernel(x_ref, w_ref, o_ref):
    o_ref[...] = jnp.dot(x_ref[...], w_ref[...], preferred_element_type=F32).astype(o_ref.dtype)


def _ones_block_diag():
    r = jnp.arange(MXU_W) // HEAD_DIM
    return (r[:, None] == r[None, :]).astype(BF16)


def _q_proj(hn, wq, gain, tm=1024):
    m, k = hn.shape
    return pl.pallas_call(
        _q_proj_kernel,
        out_shape=jax.ShapeDtypeStruct((N_KV_HEADS, m, GROUP_W), BF16),
        grid=(m // tm, N_KV_HEADS),
        in_specs=[pl.BlockSpec((tm, k), lambda i, j: (i, 0)),
                  pl.BlockSpec((k, GROUP_W), lambda i, j: (0, j)),
                  pl.BlockSpec((MXU_W, MXU_W), lambda i, j: (0, 0)),
                  pl.BlockSpec((1, GROUP_W), lambda i, j: (0, 0))],
        out_specs=pl.BlockSpec((None, tm, GROUP_W), lambda i, j: (j, i, 0)),
        compiler_params=_params(2),
        name="attn_q_proj",
    )(hn, wq, _ones_block_diag(), gain)


def _kv_proj(hn, wkv, gain, tm=1024):
    m, k = hn.shape
    return pl.pallas_call(
        _kv_proj_kernel,
        out_shape=jax.ShapeDtypeStruct((N_KV_HEADS, m, 2 * HEAD_DIM), BF16),
        grid=(m // tm,),
        in_specs=[pl.BlockSpec((tm, k), lambda i: (i, 0)),
                  pl.BlockSpec((k, 2 * KV_WIDTH), lambda i: (0, 0)),
                  pl.BlockSpec((MXU_W, MXU_W), lambda i: (0, 0)),
                  pl.BlockSpec((1, KV_WIDTH), lambda i: (0, 0))],
        out_specs=pl.BlockSpec((N_KV_HEADS, tm, 2 * HEAD_DIM), lambda i: (0, i, 0)),
        compiler_params=_params(1),
        name="attn_kv_proj",
    )(hn, wkv, _ones_block_diag(), gain)


def _gate_proj(hn, wg, tm=1024):
    m, k = hn.shape
    return pl.pallas_call(
        _cast_proj_kernel,
        out_shape=jax.ShapeDtypeStruct((N_KV_HEADS, m, GROUP_W), BF16),
        grid=(m // tm, N_KV_HEADS),
        in_specs=[pl.BlockSpec((tm, k), lambda i, j: (i, 0)),
                  pl.BlockSpec((k, GROUP_W), lambda i, j: (0, j))],
        out_specs=pl.BlockSpec((None, tm, GROUP_W), lambda i, j: (j, i, 0)),
        compiler_params=_params(2),
        name="attn_gate_proj",
    )(hn, wg)


def _attn_kernel(sink_ref, alibi_ref, q_ref, kvc_ref, kvp_ref, gate_ref, o_ref):
    n = pl.program_id(1)
    row = lax.broadcasted_iota(jnp.int32, (STACK, BLOCK), 0)
    col = lax.broadcasted_iota(jnp.int32, (STACK, BLOCK), 1)
    lower = col <= (row & (BLOCK - 1))
    valid = jnp.logical_or(lower, n > 0)
    left = lax.broadcasted_iota(jnp.int32, (BLOCK, 2 * HEAD_DIM), 1) < HEAD_DIM
    ones = jnp.ones((2 * BLOCK, 2 * HEAD_DIM), BF16)

    def split_kv(kv):
        swapped = _lane_roll(kv, HEAD_DIM)
        zero = jnp.zeros_like(kv)
        return jnp.where(left, kv, zero), jnp.where(left, zero, swapped), jnp.where(left, swapped, kv)

    def kv_head(kvh, carry):
        kc_even, kc_odd, vvc = split_kv(kvc_ref[kvh])
        kp_even, kp_odd, vvp = split_kv(kvp_ref[kvh])
        qp = jnp.concatenate(
            [q_ref[kvh, :, a * LANES:(a + 1) * LANES] for a in range(N_PAIRS)], axis=0)
        dot_nt = lambda a, b: lax.dot_general(a, b, NT, preferred_element_type=F32)
        sc = jnp.concatenate([dot_nt(qp, kc_even), dot_nt(qp, kc_odd)], axis=0)
        sp = jnp.concatenate([dot_nt(qp, kp_even), dot_nt(qp, kp_odd)], axis=0)
        s = jnp.where(lower, sc, sp) + alibi_ref[kvh]
        s = jnp.where(valid, s, NEG_INF)
        sink = jnp.concatenate(
            [jnp.full((BLOCK, 1), sink_ref[kvh * Q_PER_KV + 2 * a + par], F32)
             for par in range(2) for a in range(N_PAIRS)], axis=0)
        m = jnp.maximum(jnp.max(s, axis=-1, keepdims=True), sink)
        p = jnp.exp(s - m)
        pcat = jnp.concatenate([jnp.where(lower, p, 0.0), jnp.where(lower, 0.0, p)],
                               axis=1).astype(BF16)
        vcat = jnp.concatenate([vvc, vvp], axis=0)
        num = jnp.dot(pcat, vcat, preferred_element_type=F32)
        den = jnp.dot(pcat, ones, preferred_element_type=F32) + jnp.exp(sink - m)
        o = num / den
        for a in range(N_PAIRS):
            o_even = o[a * BLOCK:(a + 1) * BLOCK]
            o_odd = o[(N_PAIRS + a) * BLOCK:(N_PAIRS + a + 1) * BLOCK]
            gt = gate_ref[kvh, :, a * LANES:(a + 1) * LANES].astype(F32)
            o_ref[kvh, :, a * LANES:(a + 1) * LANES] = (
                jnp.where(left, o_even, o_odd) * (gt * jax.nn.sigmoid(gt))).astype(o_ref.dtype)
        return carry

    lax.fori_loop(0, N_KV_HEADS, kv_head, 0)


def _alibi_table():
    qi = jnp.arange(BLOCK)[:, None]
    kj = jnp.arange(BLOCK)[None, :]
    dist = jnp.where(kj <= qi, qi - kj, BLOCK + qi - kj).astype(F32)
    slopes = jnp.exp2(-8.0 * jnp.arange(1, N_Q_HEADS + 1, dtype=F32) / N_Q_HEADS)
    slopes = slopes.reshape(N_KV_HEADS, N_PAIRS, 2).transpose(0, 2, 1)
    return (-slopes[:, :, :, None, None] * dist).reshape(N_KV_HEADS, STACK, BLOCK)


def _attention(q, kv, gate, sinks, batch, seq):
    nb = seq // BLOCK
    m = batch * seq
    blk = lambda b, n: (0, b * nb + n, 0)
    prev = lambda b, n: (0, b * nb + jnp.maximum(n - 1, 0), 0)
    return pl.pallas_call(
        _attn_kernel,
        out_shape=jax.ShapeDtypeStruct((N_KV_HEADS, m, GROUP_W), BF16),
        grid=(batch, nb),
        in_specs=[pl.BlockSpec(memory_space=pltpu.SMEM),
                  pl.BlockSpec((N_KV_HEADS, STACK, BLOCK), lambda b, n: (0, 0, 0)),
                  pl.BlockSpec((N_KV_HEADS, BLOCK, GROUP_W), blk),
                  pl.BlockSpec((N_KV_HEADS, BLOCK, 2 * HEAD_DIM), blk),
                  pl.BlockSpec((N_KV_HEADS, BLOCK, 2 * HEAD_DIM), prev),
                  pl.BlockSpec((N_KV_HEADS, BLOCK, GROUP_W), blk)],
        out_specs=pl.BlockSpec((N_KV_HEADS, BLOCK, GROUP_W), blk),
        compiler_params=_params(2),
        name="band_attention",
    )(sinks, _alibi_table(), q, kv, kv, gate)


def _attn_out_kernel(x_ref, w_ref, r_ref, o_ref):
    acc = r_ref[...]
    for h in range(N_KV_HEADS):
        acc = acc + jnp.dot(x_ref[h], w_ref[h], preferred_element_type=F32)
    o_ref[...] = acc


def _attn_out_proj(og, w, resid, tm=512, tn=1024):
    _, m, _ = og.shape
    n = w.shape[2]
    return pl.pallas_call(
        _attn_out_kernel,
        out_shape=jax.ShapeDtypeStruct((m, n), F32),
        grid=(m // tm, n // tn),
        in_specs=[pl.BlockSpec((N_KV_HEADS, tm, GROUP_W), lambda i, j: (0, i, 0)),
                  pl.BlockSpec((N_KV_HEADS, GROUP_W, tn), lambda i, j: (0, 0, j)),
                  pl.BlockSpec((tm, tn), lambda i, j: (i, j))],
        out_specs=pl.BlockSpec((tm, tn), lambda i, j: (i, j)),
        compiler_params=_params(2),
        name="attn_out_proj",
    )(og, w, resid)


def _cast_proj(x, w, tm=1024, tn=1024):
    m, k = x.shape
    n = w.shape[1]
    return pl.pallas_call(
        _cast_proj_kernel,
        out_shape=jax.ShapeDtypeStruct((m, n), BF16),
        grid=(m // tm, n // tn),
        in_specs=[pl.BlockSpec((tm, k), lambda i, j: (i, 0)),
                  pl.BlockSpec((k, tn), lambda i, j: (0, j))],
        out_specs=pl.BlockSpec((tm, tn), lambda i, j: (i, j)),
        compiler_params=_params(2),
        name="ssm_in_proj",
    )(x, w)


def _glu_kernel(x_ref, w_ref, y_ref, gate_ref, o_ref):
    z = jnp.dot(x_ref[...], w_ref[...], preferred_element_type=F32)
    y = y_ref[...].astype(F32)
    gt = gate_ref[...].astype(F32)
    o_ref[...] = (y * jax.nn.sigmoid(z) * (gt * jax.nn.sigmoid(gt))).astype(o_ref.dtype)


def _glu(y, w, ug, tm=512, tn=512):
    m, k = y.shape
    n = w.shape[1]
    gate_off = n // tn
    return pl.pallas_call(
        _glu_kernel,
        out_shape=jax.ShapeDtypeStruct((m, n), BF16),
        grid=(m // tm, n // tn),
        in_specs=[pl.BlockSpec((tm, k), lambda i, j: (i, 0)),
                  pl.BlockSpec((k, tn), lambda i, j: (0, j)),
                  pl.BlockSpec((tm, tn), lambda i, j: (i, j)),
                  pl.BlockSpec((tm, tn), lambda i, j: (i, j + gate_off))],
        out_specs=pl.BlockSpec((tm, tn), lambda i, j: (i, j)),
        compiler_params=_params(2),
        name="ssm_glu",
    )(y, w, y, ug)


def _ssm_out_kernel(x_ref, w_ref, r_ref, o_ref, cols_ref):
    nj = x_ref.shape[1]
    x = x_ref[...].reshape(CHUNK * nj, x_ref.shape[2])
    acc = jnp.dot(x, w_ref[...], preferred_element_type=F32)
    for c in range(acc.shape[1] // LANES):
        for s in range(CHUNK):
            cols_ref[c, pl.ds(s, nj, stride=CHUNK), :] = acc[
                s * nj:(s + 1) * nj, c * LANES:(c + 1) * LANES]
        o_ref[:, c * LANES:(c + 1) * LANES] = cols_ref[c] + r_ref[:, c * LANES:(c + 1) * LANES]


def _ssm_out_proj(x, w, resid, tm=512, tn=512):
    m, k = x.shape
    n = w.shape[1]
    nj = tm // CHUNK
    return pl.pallas_call(
        _ssm_out_kernel,
        out_shape=jax.ShapeDtypeStruct((m, n), F32),
        grid=(m // tm, n // tn),
        in_specs=[pl.BlockSpec((CHUNK, nj, k), lambda i, j: (0, i, 0)),
                  pl.BlockSpec((k, tn), lambda i, j: (0, j)),
                  pl.BlockSpec((tm, tn), lambda i, j: (i, j))],
        out_specs=pl.BlockSpec((tm, tn), lambda i, j: (i, j)),
        scratch_shapes=[pltpu.VMEM((tn // LANES, tm, LANES), F32)],
        compiler_params=_params(2),
        name="ssm_out_proj",
    )(x.reshape(CHUNK, m // CHUNK, k), w, resid)


N_SCAN_STEPS = 8


def _ssm_prep_kernel(lr_ref, li_ref, ldt_ref, btr_ref, bti_ref, cr_ref, ci_ref,
                     m_ref, be_ref, cet_ref, apr_ref, api_ref, *, gb):
    tau = lax.broadcasted_iota(jnp.int32, (24, STATE_DIM), 0).astype(F32)
    kexp = lax.broadcasted_iota(jnp.int32, (N_SCAN_STEPS, STATE_DIM), 0)
    nsteps = (CHUNK * jnp.left_shift(1, kexp)).astype(F32)
    lane = lax.broadcasted_iota(jnp.int32, (GROUP_SIZE, CHUNK_W), 1)

    def one_group(g, carry):
        lr = lr_ref[g]
        li = li_ref[g]
        dt = jnp.exp(ldt_ref[g])
        mag = jnp.exp(tau * (lr * dt))
        ang = tau * (li * dt)
        pr = mag * jnp.cos(ang)
        pi = mag * jnp.sin(ang)
        ar, ai = pr[1:2], pi[1:2]
        den = lr * lr + li * li
        xr = ar - 1.0
        wr = (xr * lr + ai * li) / den
        wi = (ai * lr - xr * li) / den
        btr, bti = btr_ref[g], bti_ref[g]
        br = btr * wr - bti * wi
        bi = btr * wi + bti * wr
        cr, ci = cr_ref[g], ci_ref[g]

        ca_r, ca_i = [], []
        for t in range(CHUNK + 1):
            ca_r.append(cr * pr[t:t + 1] - ci * pi[t:t + 1])
            ca_i.append(cr * pi[t:t + 1] + ci * pr[t:t + 1])

        rt = jnp.concatenate([jnp.concatenate(ca_r[:CHUNK], axis=0),
                              jnp.concatenate(ca_i[:CHUNK], axis=0)], axis=1)
        bcat = jnp.concatenate([br, -bi], axis=1)
        kp = lax.dot_general(bcat.astype(BF16), rt.astype(BF16), NT,
                             preferred_element_type=F32)
        for s in range(CHUNK):
            blk = kp if s == 0 else pltpu.roll(kp, GROUP_SIZE * s, 1)
            blk = jnp.where(lane >= GROUP_SIZE * s, blk, 0.0)
            m_ref[g, s * GROUP_SIZE:(s + 1) * GROUP_SIZE, :] = blk.astype(m_ref.dtype)
            q = CHUNK - 1 - s
            be_re = br * pr[q:q + 1] - bi * pi[q:q + 1]
            be_im = br * pi[q:q + 1] + bi * pr[q:q + 1]
            be_ref[g, s * GROUP_SIZE:(s + 1) * GROUP_SIZE, :] = jnp.concatenate(
                [be_re, be_im], axis=1).astype(be_ref.dtype)
            cet_ref[g, s * GROUP_SIZE:(s + 1) * GROUP_SIZE, :] = jnp.concatenate(
                [ca_r[s + 1], -ca_i[s + 1]], axis=1).astype(cet_ref.dtype)

        smag = jnp.exp(nsteps * (lr * dt))
        sang = nsteps * (li * dt)
        sr = smag * jnp.cos(sang)
        si = smag * jnp.sin(sang)
        apr_ref[g] = jnp.concatenate([sr, sr], axis=1)
        api_ref[g] = jnp.concatenate([-si, si], axis=1)
        return carry

    lax.fori_loop(0, gb, one_group, 0)


def _ssm_prep(lam_re, lam_im, log_dt, b_re, b_im, c_re, c_im, gb=8):
    g = N_GROUPS
    row = lambda a: a.reshape(g, 1, STATE_DIM)
    ldt = jnp.broadcast_to(log_dt.reshape(g, 1, 1), (g, 1, STATE_DIM))
    btr = jnp.swapaxes(b_re, 1, 2)
    bti = jnp.swapaxes(b_im, 1, 2)
    vec = pl.BlockSpec((gb, 1, STATE_DIM), lambda i: (i, 0, 0))
    mat = pl.BlockSpec((gb, GROUP_SIZE, STATE_DIM), lambda i: (i, 0, 0))
    return pl.pallas_call(
        functools.partial(_ssm_prep_kernel, gb=gb),
        out_shape=[jax.ShapeDtypeStruct((g, CHUNK_W, CHUNK_W), BF16),
                   jax.ShapeDtypeStruct((g, CHUNK_W, 2 * STATE_DIM), BF16),
                   jax.ShapeDtypeStruct((g, CHUNK_W, 2 * STATE_DIM), BF16),
                   jax.ShapeDtypeStruct((g, N_SCAN_STEPS, 2 * STATE_DIM), F32),
                   jax.ShapeDtypeStruct((g, N_SCAN_STEPS, 2 * STATE_DIM), F32)],
        grid=(g // gb,),
        in_specs=[vec, vec, vec, mat, mat, mat, mat],
        out_specs=[pl.BlockSpec((gb, CHUNK_W, CHUNK_W), lambda i: (i, 0, 0)),
                   pl.BlockSpec((gb, CHUNK_W, 2 * STATE_DIM), lambda i: (i, 0, 0)),
                   pl.BlockSpec((gb, CHUNK_W, 2 * STATE_DIM), lambda i: (i, 0, 0)),
                   pl.BlockSpec((gb, N_SCAN_STEPS, 2 * STATE_DIM), lambda i: (i, 0, 0)),
                   pl.BlockSpec((gb, N_SCAN_STEPS, 2 * STATE_DIM), lambda i: (i, 0, 0))],
        compiler_params=_params(1),
        name="ssm_prep",
    )(row(lam_re), row(lam_im), ldt, btr, bti, c_re, c_im)


SCAN_PAD = 128
XPOSE_ROWS = 64


def _piece_transpose(vs):
    lane = lax.broadcasted_iota(jnp.int32, vs[0].shape, 1)
    vs = list(vs)
    for d in (4, 2, 1):
        w = GROUP_SIZE * d
        keep = (lane & w) == 0
        for k in range(GROUPS_PER_BLOCK):
            if k & d:
                continue
            a, b = vs[k], vs[k + d]
            vs[k] = jnp.where(keep, a, _lane_roll(b, w))
            vs[k + d] = jnp.where(keep, _lane_roll(a, LANES - w), b)
    return vs


def _ssm_kernel(x_ref, m_ref, be_ref, cet_ref, apr_ref, api_ref, d_ref, o_ref,
                uf_ref, yf_ref, xs_ref, *, batch, n_chunks):
    rows = batch * n_chunks

    def gather(i, carry):
        r0 = pl.multiple_of(i * XPOSE_ROWS, XPOSE_ROWS)
        for half in range(2):
            vs = [x_ref[GROUPS_PER_BLOCK * half + k, pl.ds(r0, XPOSE_ROWS), :]
                  for k in range(GROUPS_PER_BLOCK)]
            ts = _piece_transpose(vs)
            for g in range(GROUPS_PER_BLOCK):
                uf_ref[g, pl.ds(r0, XPOSE_ROWS), half * LANES:(half + 1) * LANES] = ts[g]
        return carry

    lax.fori_loop(0, rows // XPOSE_ROWS, gather, 0)

    xs_ref[:, 0:SCAN_PAD, :] = jnp.zeros((batch, SCAN_PAD, 2 * STATE_DIM), F32)

    def one_group(g, carry):
        u = uf_ref[g]
        y = jnp.dot(u, m_ref[g], preferred_element_type=F32)
        v = jnp.dot(u, be_ref[g], preferred_element_type=F32)
        apr = apr_ref[g]
        api = api_ref[g]
        entering = []
        for b in range(batch):
            x = v[b * n_chunks:(b + 1) * n_chunks]
            for k in range(N_SCAN_STEPS):
                d = 1 << k
                xs_ref[b, SCAN_PAD:SCAN_PAD + n_chunks, :] = x
                sh = xs_ref[b, SCAN_PAD - d:SCAN_PAD - d + n_chunks, :]
                x = x + sh * apr[k:k + 1] + pltpu.roll(sh, STATE_DIM, 1) * api[k:k + 1]
            xs_ref[b, SCAN_PAD:SCAN_PAD + n_chunks, :] = x
            entering.append(xs_ref[b, SCAN_PAD - 1:SCAN_PAD - 1 + n_chunks, :])
        e = jnp.concatenate(entering, axis=0).astype(BF16)
        y = y + lax.dot_general(e, cet_ref[g], NT, preferred_element_type=F32)
        y = y + d_ref[g] * u.astype(F32)
        yf_ref[g] = jax.nn.gelu(y).astype(yf_ref.dtype)
        return carry

    lax.fori_loop(0, GROUPS_PER_BLOCK, one_group, 0)

    def scatter(i, carry):
        r0 = pl.multiple_of(i * XPOSE_ROWS, XPOSE_ROWS)
        for half in range(2):
            vs = [yf_ref[g, pl.ds(r0, XPOSE_ROWS), half * LANES:(half + 1) * LANES]
                  for g in range(GROUPS_PER_BLOCK)]
            ts = _piece_transpose(vs)
            for k in range(GROUPS_PER_BLOCK):
                o_ref[GROUPS_PER_BLOCK * half + k, pl.ds(r0, XPOSE_ROWS), :] = ts[k]
        return carry

    lax.fori_loop(0, rows // XPOSE_ROWS, scatter, 0)


def _ssm(ug, m, be, cet, apr, api, dvec, batch):
    _, rows, _ = ug.shape
    n_chunks = rows // batch
    gb = GROUPS_PER_BLOCK
    col_blk = pl.BlockSpec((CHUNK, rows, LANES), lambda i: (0, 0, i))
    grp = lambda last2: pl.BlockSpec((gb,) + last2, lambda i: (i, 0, 0))
    return pl.pallas_call(
        functools.partial(_ssm_kernel, batch=batch, n_chunks=n_chunks),
        out_shape=jax.ShapeDtypeStruct((CHUNK, rows, SSM_WIDTH), BF16),
        grid=(N_GROUPS // gb,),
        in_specs=[col_blk, grp((CHUNK_W, CHUNK_W)), grp((CHUNK_W, 2 * STATE_DIM)),
                  grp((CHUNK_W, 2 * STATE_DIM)), grp((N_SCAN_STEPS, 2 * STATE_DIM)),
                  grp((N_SCAN_STEPS, 2 * STATE_DIM)), grp((1, CHUNK_W))],
        out_specs=col_blk,
        scratch_shapes=[pltpu.VMEM((gb, rows, CHUNK_W), BF16),
                        pltpu.VMEM((gb, rows, CHUNK_W), BF16),
                        pltpu.VMEM((batch, SCAN_PAD + n_chunks, 2 * STATE_DIM), F32)],
        compiler_params=_params(1),
        name="ssm_chunked",
    )(ug, m, be, cet, apr, api, dvec)


def kernel(x, norm_g, attn_w_in, attn_q_norm_g, attn_k_norm_g, attn_sinks, attn_w_out, ssm_w_in,
           ssm_log_dt, ssm_lam_re, ssm_lam_im, ssm_b_re, ssm_b_im, ssm_c_re, ssm_c_im, ssm_d,
           ssm_w_glu, ssm_w_out):
    batch, seq, d = x.shape
    m = batch * seq
    assert d == D_MODEL and seq % BLOCK == 0 and seq // CHUNK == 1 << N_SCAN_STEPS
    x2 = x.reshape(m, d)

    w_in = attn_w_in[0]
    wq = w_in[:, :D_MODEL].astype(BF16)
    wkv = w_in[:, D_MODEL:D_MODEL + 2 * KV_WIDTH].astype(BF16)
    wg = w_in[:, D_MODEL + 2 * KV_WIDTH:].astype(BF16)
    q_gain = jnp.tile(attn_q_norm_g[0].astype(F32) * HEAD_DIM ** -0.5, Q_PER_KV).reshape(1, GROUP_W)
    k_gain = jnp.tile(attn_k_norm_g[0].astype(F32), N_KV_HEADS).reshape(1, KV_WIDTH)

    hn0 = _rmsnorm(x2, norm_g[0])
    q = _q_proj(hn0, wq, q_gain)
    kv = _kv_proj(hn0, wkv, k_gain)
    gate = _gate_proj(hn0, wg)
    og = _attention(q, kv, gate, attn_sinks[0].astype(F32), batch, seq)
    w_out = attn_w_out[0].astype(BF16).reshape(N_KV_HEADS, GROUP_W, D_MODEL)
    h1 = _attn_out_proj(og, w_out, x2)

    nj = m // CHUNK
    hn1 = _rmsnorm_chunk_major(h1, norm_g[1])
    ug = _cast_proj(hn1, ssm_w_in[0].astype(BF16))
    mt, be, cet, apr, api = _ssm_prep(ssm_lam_re[0], ssm_lam_im[0], ssm_log_dt[0],
                                      ssm_b_re[0], ssm_b_im[0], ssm_c_re[0], ssm_c_im[0])
    dvec = jnp.tile(ssm_d[0].astype(F32).reshape(N_GROUPS, 1, GROUP_SIZE), (1, 1, CHUNK))
    y = _ssm(ug.reshape(CHUNK, nj, 2 * SSM_WIDTH), mt, be, cet, apr, api, dvec, batch)
    y = y.reshape(m, SSM_WIDTH)
    p = _glu(y, ssm_w_glu[0].astype(BF16), ug)
    out = _ssm_out_proj(p, ssm_w_out[0].astype(BF16), h1)
    return out.reshape(batch, seq, d)
```

```python
import functools

import jax
import jax.numpy as jnp
from jax import lax
from jax.experimental import pallas as pl
from jax.experimental.pallas import tpu as pltpu

F32 = jnp.float32
BF16 = jnp.bfloat16

D_MODEL = 4096
HEAD_DIM = 64
N_Q_HEADS = 64
N_KV_HEADS = 8
Q_PER_KV = 8
N_PAIRS = Q_PER_KV // 2
KV_WIDTH = N_KV_HEADS * HEAD_DIM
GROUP_W = Q_PER_KV * HEAD_DIM
BLOCK = 128
STACK = Q_PER_KV * BLOCK
SSM_WIDTH = 8192
GROUP_SIZE = 16
N_GROUPS = 512
STATE_DIM = 64
CHUNK = 16
CHUNK_W = CHUNK * GROUP_SIZE
RMS_EPS = 1e-6
NEG_INF = -1e30

LANES = 128
VMEM_LIMIT = 56 * 1024 * 1024
MXU_W = 256
GROUPS_PER_BLOCK = LANES // GROUP_SIZE

NT = (((1,), (1,)), ((), ()))


def _lane_roll(x, shift):
    return jnp.concatenate([x[:, LANES - shift:], x[:, :LANES - shift]], axis=1)


def _cast_weight_once(w_ref, wb_ref):
    @pl.when(pl.program_id(1) == 0)
    def _():
        wb_ref[...] = w_ref[...].astype(BF16)


def _params(n_grid_dims):
    return pltpu.CompilerParams(
        dimension_semantics=("arbitrary",) * n_grid_dims, vmem_limit_bytes=VMEM_LIMIT)


def _rms(x, g):
    var = jnp.mean(x * x, axis=-1, keepdims=True)
    return x * lax.rsqrt(var + RMS_EPS) * g


def _rmsnorm_kernel(x_ref, g_ref, o_ref):
    o_ref[...] = _rms(x_ref[...], g_ref[...]).astype(o_ref.dtype)


def _rmsnorm(x, g, tm=256):
    m, d = x.shape
    return pl.pallas_call(
        _rmsnorm_kernel,
        out_shape=jax.ShapeDtypeStruct((m, d), BF16),
        grid=(m // tm,),
        in_specs=[pl.BlockSpec((tm, d), lambda i: (i, 0)),
                  pl.BlockSpec((1, d), lambda i: (0, 0))],
        out_specs=pl.BlockSpec((tm, d), lambda i: (i, 0)),
        compiler_params=_params(1),
        name="rmsnorm",
    )(x, g.reshape(1, d))


def _rmsnorm_chunk_major_kernel(x_ref, g_ref, o_ref, cols_ref):
    hn = _rms(x_ref[...], g_ref[...])
    n_cols = hn.shape[1] // LANES
    nj = hn.shape[0] // CHUNK
    for c in range(n_cols):
        cols_ref[c] = hn[:, c * LANES:(c + 1) * LANES]
    for s in range(CHUNK):
        for c in range(n_cols):
            o_ref[s, :, c * LANES:(c + 1) * LANES] = cols_ref[
                c, pl.ds(s, nj, stride=CHUNK), :].astype(o_ref.dtype)


def _rmsnorm_chunk_major(x, g, tm=256):
    m, d = x.shape
    nj = tm // CHUNK
    out = pl.pallas_call(
        _rmsnorm_chunk_major_kernel,
        out_shape=jax.ShapeDtypeStruct((CHUNK, m // CHUNK, d), BF16),
        grid=(m // tm,),
        in_specs=[pl.BlockSpec((tm, d), lambda i: (i, 0)),
                  pl.BlockSpec((1, d), lambda i: (0, 0))],
        out_specs=pl.BlockSpec((CHUNK, nj, d), lambda i: (0, i, 0)),
        scratch_shapes=[pltpu.VMEM((d // LANES, tm, LANES), F32)],
        compiler_params=_params(1),
        name="rmsnorm_chunk_major",
    )(x, g.reshape(1, d))
    return out.reshape(m, d)


def _head_rmsnorm(acc, ones_bd, gain):
    outs = []
    for c in range(acc.shape[1] // MXU_W):
        a = acc[:, c * MXU_W:(c + 1) * MXU_W]
        sq = a * a
        hi = sq.astype(BF16)
        lo = (sq - hi.astype(F32)).astype(BF16)
        ss = (jnp.dot(hi, ones_bd, preferred_element_type=F32)
              + jnp.dot(lo, ones_bd, preferred_element_type=F32))
        inv = lax.rsqrt(ss * (1.0 / HEAD_DIM) + RMS_EPS)
        outs.append(a * inv * gain[:, c * MXU_W:(c + 1) * MXU_W])
    return outs[0] if len(outs) == 1 else jnp.concatenate(outs, axis=1)


def _q_proj_kernel(x_ref, w_ref, e_ref, g_ref, o_ref, wb_ref):
    _cast_weight_once(w_ref, wb_ref)
    acc = jnp.dot(x_ref[...], wb_ref[...], preferred_element_type=F32)
    o_ref[...] = _head_rmsnorm(acc, e_ref[...], g_ref[...]).astype(o_ref.dtype)


def _kv_proj_kernel(x_ref, w_ref, e_ref, g_ref, o_ref, wb_ref):
    _cast_weight_once(w_ref, wb_ref)
    acc = jnp.dot(x_ref[...], wb_ref[...], preferred_element_type=F32)
    kn = _head_rmsnorm(acc[:, :KV_WIDTH], e_ref[...], g_ref[...])
    for h in range(N_KV_HEADS):
        o_ref[h, :, 0:HEAD_DIM] = kn[:, h * HEAD_DIM:(h + 1) * HEAD_DIM].astype(o_ref.dtype)
        o_ref[h, :, HEAD_DIM:2 * HEAD_DIM] = acc[
            :, KV_WIDTH + h * HEAD_DIM:KV_WIDTH + (h + 1) * HEAD_DIM].astype(o_ref.dtype)


def _cast_proj_kernel(x_ref, w_ref, o_ref, wb_ref):
    _cast_weight_once(w_ref, wb_ref)
    o_ref[...] = jnp.dot(x_ref[...], wb_ref[...], preferred_element_type=F32).astype(o_ref.dtype)


def _ones_block_diag():
    r = jnp.arange(MXU_W) // HEAD_DIM
    return (r[:, None] == r[None, :]).astype(BF16)


Q_COL0 = 0
KV_COL0 = D_MODEL
GATE_COL0 = D_MODEL + 2 * KV_WIDTH


def _q_proj(hn, w_in, gain, tm=1024):
    m, k = hn.shape
    return pl.pallas_call(
        _q_proj_kernel,
        out_shape=jax.ShapeDtypeStruct((N_KV_HEADS, m, GROUP_W), BF16),
        grid=(N_KV_HEADS, m // tm),
        in_specs=[pl.BlockSpec((tm, k), lambda j, i: (i, 0)),
                  pl.BlockSpec((k, GROUP_W), lambda j, i: (0, Q_COL0 // GROUP_W + j)),
                  pl.BlockSpec((MXU_W, MXU_W), lambda j, i: (0, 0)),
                  pl.BlockSpec((1, GROUP_W), lambda j, i: (0, 0))],
        out_specs=pl.BlockSpec((None, tm, GROUP_W), lambda j, i: (j, i, 0)),
        scratch_shapes=[pltpu.VMEM((k, GROUP_W), BF16)],
        compiler_params=_params(2),
        name="attn_q_proj",
    )(hn, w_in, _ones_block_diag(), gain)


def _kv_proj(hn, w_in, gain, tm=512):
    m, k = hn.shape
    n = 2 * KV_WIDTH
    return pl.pallas_call(
        _kv_proj_kernel,
        out_shape=jax.ShapeDtypeStruct((N_KV_HEADS, m, 2 * HEAD_DIM), BF16),
        grid=(1, m // tm),
        in_specs=[pl.BlockSpec((tm, k), lambda j, i: (i, 0)),
                  pl.BlockSpec((k, n), lambda j, i: (0, KV_COL0 // n)),
                  pl.BlockSpec((MXU_W, MXU_W), lambda j, i: (0, 0)),
                  pl.BlockSpec((1, KV_WIDTH), lambda j, i: (0, 0))],
        out_specs=pl.BlockSpec((N_KV_HEADS, tm, 2 * HEAD_DIM), lambda j, i: (0, i, 0)),
        scratch_shapes=[pltpu.VMEM((k, n), BF16)],
        compiler_params=_params(2),
        name="attn_kv_proj",
    )(hn, w_in, _ones_block_diag(), gain)


def _gate_proj(hn, w_in, tm=1024):
    m, k = hn.shape
    return pl.pallas_call(
        _cast_proj_kernel,
        out_shape=jax.ShapeDtypeStruct((N_KV_HEADS, m, GROUP_W), BF16),
        grid=(N_KV_HEADS, m // tm),
        in_specs=[pl.BlockSpec((tm, k), lambda j, i: (i, 0)),
                  pl.BlockSpec((k, GROUP_W), lambda j, i: (0, GATE_COL0 // GROUP_W + j))],
        out_specs=pl.BlockSpec((None, tm, GROUP_W), lambda j, i: (j, i, 0)),
        scratch_shapes=[pltpu.VMEM((k, GROUP_W), BF16)],
        compiler_params=_params(2),
        name="attn_gate_proj",
    )(hn, w_in)


def _attn_kernel(sink_ref, alibi_ref, q_ref, kvc_ref, kvp_ref, gate_ref, o_ref):
    n = pl.program_id(1)
    row = lax.broadcasted_iota(jnp.int32, (BLOCK, BLOCK), 0)
    col = lax.broadcasted_iota(jnp.int32, (BLOCK, BLOCK), 1)
    lower = col <= row
    valid = jnp.logical_or(lower, n > 0)
    left = col < HEAD_DIM
    ones = jnp.ones((2 * BLOCK, LANES), BF16)

    def split_kv(kv):
        swapped = _lane_roll(kv, HEAD_DIM)
        zero = jnp.zeros_like(kv)
        return jnp.where(left, kv, zero), jnp.where(left, zero, swapped), jnp.where(left, swapped, kv)

    def kv_head(kvh, carry):
        kc_even, kc_odd, vvc = split_kv(kvc_ref[kvh])
        kp_even, kp_odd, vvp = split_kv(kvp_ref[kvh])
        qp = jnp.concatenate(
            [q_ref[kvh, :, a * LANES:(a + 1) * LANES] for a in range(N_PAIRS)], axis=0)
        keys = jnp.concatenate([kc_even, kc_odd, kp_even, kp_odd], axis=0)
        s_all = lax.dot_general(qp, keys, NT, preferred_element_type=F32)
        v_ones = jnp.concatenate([jnp.concatenate([vvc, vvp], axis=0), ones], axis=1)
        ps, sink_terms = [], []
        for par in range(2):
            for a in range(N_PAIRS):
                rows = slice(a * BLOCK, (a + 1) * BLOCK)
                sc = s_all[rows, par * LANES:(par + 1) * LANES]
                sp = s_all[rows, (2 + par) * LANES:(3 + par) * LANES]
                blk = par * N_PAIRS + a
                s = jnp.where(lower, sc, sp) + alibi_ref[kvh, blk * BLOCK:(blk + 1) * BLOCK, :]
                s = jnp.where(valid, s, NEG_INF)
                sink = sink_ref[kvh * Q_PER_KV + 2 * a + par]
                m = jnp.maximum(jnp.max(s, axis=-1, keepdims=True), sink)
                p = jnp.exp(s - m)
                ps.append(jnp.concatenate([jnp.where(lower, p, 0.0), jnp.where(lower, 0.0, p)],
                                          axis=1).astype(BF16))
                sink_terms.append(jnp.exp(sink - m))
        nd = jnp.dot(jnp.concatenate(ps, axis=0), v_ones, preferred_element_type=F32)
        for a in range(N_PAIRS):
            o = []
            for par in range(2):
                blk = par * N_PAIRS + a
                r = nd[blk * BLOCK:(blk + 1) * BLOCK]
                o.append(r[:, :LANES] / (r[:, LANES:] + sink_terms[blk]))
            gt = gate_ref[kvh, :, a * LANES:(a + 1) * LANES].astype(F32)
            o_ref[kvh, :, a * LANES:(a + 1) * LANES] = (
                jnp.where(left, o[0], o[1]) * (gt * jax.nn.sigmoid(gt))).astype(o_ref.dtype)
        return carry

    lax.fori_loop(0, N_KV_HEADS, kv_head, 0, unroll=True)


def _alibi_table():
    qi = jnp.arange(BLOCK)[:, None]
    kj = jnp.arange(BLOCK)[None, :]
    dist = jnp.where(kj <= qi, qi - kj, BLOCK + qi - kj).astype(F32)
    slopes = jnp.exp2(-8.0 * jnp.arange(1, N_Q_HEADS + 1, dtype=F32) / N_Q_HEADS)
    slopes = slopes.reshape(N_KV_HEADS, N_PAIRS, 2).transpose(0, 2, 1)
    return (-slopes[:, :, :, None, None] * dist).reshape(N_KV_HEADS, STACK, BLOCK)


def _attention(q, kv, gate, sinks, batch, seq):
    nb = seq // BLOCK
    m = batch * seq
    blk = lambda b, n: (0, b * nb + n, 0)
    prev = lambda b, n: (0, b * nb + jnp.maximum(n - 1, 0), 0)
    return pl.pallas_call(
        _attn_kernel,
        out_shape=jax.ShapeDtypeStruct((N_KV_HEADS, m, GROUP_W), BF16),
        grid=(batch, nb),
        in_specs=[pl.BlockSpec(memory_space=pltpu.SMEM),
                  pl.BlockSpec((N_KV_HEADS, STACK, BLOCK), lambda b, n: (0, 0, 0)),
                  pl.BlockSpec((N_KV_HEADS, BLOCK, GROUP_W), blk),
                  pl.BlockSpec((N_KV_HEADS, BLOCK, 2 * HEAD_DIM), blk),
                  pl.BlockSpec((N_KV_HEADS, BLOCK, 2 * HEAD_DIM), prev),
                  pl.BlockSpec((N_KV_HEADS, BLOCK, GROUP_W), blk)],
        out_specs=pl.BlockSpec((N_KV_HEADS, BLOCK, GROUP_W), blk),
        compiler_params=_params(2),
        name="band_attention",
    )(sinks, _alibi_table(), q, kv, kv, gate)


def _attn_out_kernel(x_ref, w_ref, r_ref, o_ref, wb_ref):
    _cast_weight_once(w_ref, wb_ref)
    acc = r_ref[...]
    for h in range(N_KV_HEADS):
        acc = acc + jnp.dot(x_ref[h], wb_ref[h], preferred_element_type=F32)
    o_ref[...] = acc


def _attn_out_proj(og, w, resid, tm=1024, tn=512):
    _, m, _ = og.shape
    n = w.shape[2]
    return pl.pallas_call(
        _attn_out_kernel,
        out_shape=jax.ShapeDtypeStruct((m, n), F32),
        grid=(n // tn, m // tm),
        in_specs=[pl.BlockSpec((N_KV_HEADS, tm, GROUP_W), lambda j, i: (0, i, 0)),
                  pl.BlockSpec((N_KV_HEADS, GROUP_W, tn), lambda j, i: (0, 0, j)),
                  pl.BlockSpec((tm, tn), lambda j, i: (i, j))],
        out_specs=pl.BlockSpec((tm, tn), lambda j, i: (i, j)),
        scratch_shapes=[pltpu.VMEM((N_KV_HEADS, GROUP_W, tn), BF16)],
        compiler_params=_params(2),
        name="attn_out_proj",
    )(og, w, resid)


def _ssm_in_proj(x, w, tm=512, tn=1024):
    m, k = x.shape
    n = w.shape[1]
    return pl.pallas_call(
        _cast_proj_kernel,
        out_shape=jax.ShapeDtypeStruct((m, n), BF16),
        grid=(n // tn, m // tm),
        in_specs=[pl.BlockSpec((tm, k), lambda j, i: (i, 0)),
                  pl.BlockSpec((k, tn), lambda j, i: (0, j))],
        out_specs=pl.BlockSpec((tm, tn), lambda j, i: (i, j)),
        scratch_shapes=[pltpu.VMEM((k, tn), BF16)],
        compiler_params=_params(2),
        name="ssm_in_proj",
    )(x, w)


def _glu_kernel(x_ref, w_ref, y_ref, gate_ref, o_ref):
    z = jnp.dot(x_ref[...], w_ref[...], preferred_element_type=F32)
    y = y_ref[...].astype(F32)
    gt = gate_ref[...].astype(F32)
    o_ref[...] = (y * jax.nn.sigmoid(z) * (gt * jax.nn.sigmoid(gt))).astype(o_ref.dtype)


def _glu(y, w, ug, tm=1024, tn=256):
    m, k = y.shape
    n = w.shape[1]
    gate_off = n // tn
    return pl.pallas_call(
        _glu_kernel,
        out_shape=jax.ShapeDtypeStruct((m, n), BF16),
        grid=(m // tm, n // tn),
        in_specs=[pl.BlockSpec((tm, k), lambda i, j: (i, 0)),
                  pl.BlockSpec((k, tn), lambda i, j: (0, j)),
                  pl.BlockSpec((tm, tn), lambda i, j: (i, j)),
                  pl.BlockSpec((tm, tn), lambda i, j: (i, j + gate_off))],
        out_specs=pl.BlockSpec((tm, tn), lambda i, j: (i, j)),
        compiler_params=_params(2),
        name="ssm_glu",
    )(y, w, y, ug)


def _ssm_out_kernel(x_ref, w_ref, r_ref, o_ref, cols_ref):
    nj = x_ref.shape[1]
    x = x_ref[...].reshape(CHUNK * nj, x_ref.shape[2])
    acc = jnp.dot(x, w_ref[...], preferred_element_type=F32)
    for c in range(acc.shape[1] // LANES):
        for s in range(CHUNK):
            cols_ref[c, pl.ds(s, nj, stride=CHUNK), :] = acc[
                s * nj:(s + 1) * nj, c * LANES:(c + 1) * LANES]
        o_ref[:, c * LANES:(c + 1) * LANES] = cols_ref[c] + r_ref[:, c * LANES:(c + 1) * LANES]


def _ssm_out_proj(x, w, resid, tm=1024, tn=256):
    m, k = x.shape
    n = w.shape[1]
    nj = tm // CHUNK
    return pl.pallas_call(
        _ssm_out_kernel,
        out_shape=jax.ShapeDtypeStruct((m, n), F32),
        grid=(m // tm, n // tn),
        in_specs=[pl.BlockSpec((CHUNK, nj, k), lambda i, j: (0, i, 0)),
                  pl.BlockSpec((k, tn), lambda i, j: (0, j)),
                  pl.BlockSpec((tm, tn), lambda i, j: (i, j))],
        out_specs=pl.BlockSpec((tm, tn), lambda i, j: (i, j)),
        scratch_shapes=[pltpu.VMEM((tn // LANES, tm, LANES), F32)],
        compiler_params=_params(2),
        name="ssm_out_proj",
    )(x.reshape(CHUNK, m // CHUNK, k), w, resid)


N_SCAN_STEPS = 8


def _ssm_prep_kernel(lr_ref, li_ref, ldt_ref, btr_ref, bti_ref, cr_ref, ci_ref,
                     m_ref, be_ref, cet_ref, apr_ref, api_ref, *, gb):
    tau = lax.broadcasted_iota(jnp.int32, (24, STATE_DIM), 0).astype(F32)
    kexp = lax.broadcasted_iota(jnp.int32, (N_SCAN_STEPS, STATE_DIM), 0)
    nsteps = (CHUNK * jnp.left_shift(1, kexp)).astype(F32)
    lane = lax.broadcasted_iota(jnp.int32, (GROUP_SIZE, CHUNK_W), 1)

    def one_group(g, carry):
        lr = lr_ref[g]
        li = li_ref[g]
        dt = jnp.exp(ldt_ref[g])
        mag = jnp.exp(tau * (lr * dt))
        ang = tau * (li * dt)
        pr = mag * jnp.cos(ang)
        pi = mag * jnp.sin(ang)
        ar, ai = pr[1:2], pi[1:2]
        den = lr * lr + li * li
        xr = ar - 1.0
        wr = (xr * lr + ai * li) / den
        wi = (ai * lr - xr * li) / den
        btr, bti = btr_ref[g], bti_ref[g]
        br = btr * wr - bti * wi
        bi = btr * wi + bti * wr
        cr, ci = cr_ref[g], ci_ref[g]

        ca_r, ca_i = [], []
        for t in range(CHUNK + 1):
            ca_r.append(cr * pr[t:t + 1] - ci * pi[t:t + 1])
            ca_i.append(cr * pi[t:t + 1] + ci * pr[t:t + 1])

        rt = jnp.concatenate([jnp.concatenate(ca_r[:CHUNK], axis=0),
                              jnp.concatenate(ca_i[:CHUNK], axis=0)], axis=1)
        bcat = jnp.concatenate([br, -bi], axis=1)
        kp = lax.dot_general(bcat.astype(BF16), rt.astype(BF16), NT,
                             preferred_element_type=F32)
        for s in range(CHUNK):
            blk = kp if s == 0 else pltpu.roll(kp, GROUP_SIZE * s, 1)
            blk = jnp.where(lane >= GROUP_SIZE * s, blk, 0.0)
            m_ref[g, s * GROUP_SIZE:(s + 1) * GROUP_SIZE, :] = blk.astype(m_ref.dtype)
            q = CHUNK - 1 - s
            be_re = br * pr[q:q + 1] - bi * pi[q:q + 1]
            be_im = br * pi[q:q + 1] + bi * pr[q:q + 1]
            be_ref[g, s * GROUP_SIZE:(s + 1) * GROUP_SIZE, :] = jnp.concatenate(
                [be_re, be_im], axis=1).astype(be_ref.dtype)
            cet_ref[g, s * GROUP_SIZE:(s + 1) * GROUP_SIZE, :] = jnp.concatenate(
                [ca_r[s + 1], -ca_i[s + 1]], axis=1).astype(cet_ref.dtype)

        smag = jnp.exp(nsteps * (lr * dt))
        sang = nsteps * (li * dt)
        sr = smag * jnp.cos(sang)
        si = smag * jnp.sin(sang)
        apr_ref[g] = jnp.concatenate([sr, sr], axis=1)
        api_ref[g] = jnp.concatenate([-si, si], axis=1)
        return carry

    lax.fori_loop(0, gb, one_group, 0)


def _ssm_prep(lam_re, lam_im, log_dt, b_re, b_im, c_re, c_im, gb=8):
    g = N_GROUPS
    row = lambda a: a.reshape(g, 1, STATE_DIM)
    ldt = jnp.broadcast_to(log_dt.reshape(g, 1, 1), (g, 1, STATE_DIM))
    btr = jnp.swapaxes(b_re, 1, 2)
    bti = jnp.swapaxes(b_im, 1, 2)
    vec = pl.BlockSpec((gb, 1, STATE_DIM), lambda i: (i, 0, 0))
    mat = pl.BlockSpec((gb, GROUP_SIZE, STATE_DIM), lambda i: (i, 0, 0))
    return pl.pallas_call(
        functools.partial(_ssm_prep_kernel, gb=gb),
        out_shape=[jax.ShapeDtypeStruct((g, CHUNK_W, CHUNK_W), BF16),
                   jax.ShapeDtypeStruct((g, CHUNK_W, 2 * STATE_DIM), BF16),
                   jax.ShapeDtypeStruct((g, CHUNK_W, 2 * STATE_DIM), BF16),
                   jax.ShapeDtypeStruct((g, N_SCAN_STEPS, 2 * STATE_DIM), F32),
                   jax.ShapeDtypeStruct((g, N_SCAN_STEPS, 2 * STATE_DIM), F32)],
        grid=(g // gb,),
        in_specs=[vec, vec, vec, mat, mat, mat, mat],
        out_specs=[pl.BlockSpec((gb, CHUNK_W, CHUNK_W), lambda i: (i, 0, 0)),
                   pl.BlockSpec((gb, CHUNK_W, 2 * STATE_DIM), lambda i: (i, 0, 0)),
                   pl.BlockSpec((gb, CHUNK_W, 2 * STATE_DIM), lambda i: (i, 0, 0)),
                   pl.BlockSpec((gb, N_SCAN_STEPS, 2 * STATE_DIM), lambda i: (i, 0, 0)),
                   pl.BlockSpec((gb, N_SCAN_STEPS, 2 * STATE_DIM), lambda i: (i, 0, 0))],
        compiler_params=_params(1),
        name="ssm_prep",
    )(row(lam_re), row(lam_im), ldt, btr, bti, c_re, c_im)


SCAN_PAD = 128
XPOSE_ROWS = 256


def _piece_transpose(vs):
    lane = lax.broadcasted_iota(jnp.int32, vs[0].shape, 1)
    vs = list(vs)
    for d in (4, 2, 1):
        w = GROUP_SIZE * d
        keep = (lane & w) == 0
        for k in range(GROUPS_PER_BLOCK):
            if k & d:
                continue
            a, b = vs[k], vs[k + d]
            vs[k] = jnp.where(keep, a, _lane_roll(b, w))
            vs[k + d] = jnp.where(keep, _lane_roll(a, LANES - w), b)
    return vs


def _ssm_kernel(x_ref, m_ref, be_ref, cet_ref, apr_ref, api_ref, d_ref, o_ref,
                uf_ref, yf_ref, xs_ref, *, batch, n_chunks):
    rows = batch * n_chunks

    def gather(i, carry):
        r0 = pl.multiple_of(i * XPOSE_ROWS, XPOSE_ROWS)
        for half in range(2):
            vs = [x_ref[GROUPS_PER_BLOCK * half + k, pl.ds(r0, XPOSE_ROWS), :]
                  for k in range(GROUPS_PER_BLOCK)]
            ts = _piece_transpose(vs)
            for g in range(GROUPS_PER_BLOCK):
                uf_ref[g, pl.ds(r0, XPOSE_ROWS), half * LANES:(half + 1) * LANES] = ts[g]
        return carry

    lax.fori_loop(0, rows // XPOSE_ROWS, gather, 0)

    xs_ref[:, 0:SCAN_PAD, :] = jnp.zeros((batch, SCAN_PAD, 2 * STATE_DIM), F32)

    def one_group(g, carry):
        u = uf_ref[g]
        y = jnp.dot(u, m_ref[g], preferred_element_type=F32)
        v = jnp.dot(u, be_ref[g], preferred_element_type=F32)
        apr = apr_ref[g]
        api = api_ref[g]
        entering = []
        for b in range(batch):
            x = v[b * n_chunks:(b + 1) * n_chunks]
            for k in range(N_SCAN_STEPS):
                d = 1 << k
                xs_ref[b, SCAN_PAD:SCAN_PAD + n_chunks, :] = x
                sh = xs_ref[b, SCAN_PAD - d:SCAN_PAD - d + n_chunks, :]
                x = x + sh * apr[k:k + 1] + pltpu.roll(sh, STATE_DIM, 1) * api[k:k + 1]
            xs_ref[b, SCAN_PAD:SCAN_PAD + n_chunks, :] = x
            entering.append(xs_ref[b, SCAN_PAD - 1:SCAN_PAD - 1 + n_chunks, :])
        e = jnp.concatenate(entering, axis=0).astype(BF16)
        y = y + lax.dot_general(e, cet_ref[g], NT, preferred_element_type=F32)
        y = y + d_ref[g] * u.astype(F32)
        yf_ref[g] = jax.nn.gelu(y).astype(yf_ref.dtype)
        return carry

    lax.fori_loop(0, GROUPS_PER_BLOCK, one_group, 0)

    def scatter(i, carry):
        r0 = pl.multiple_of(i * XPOSE_ROWS, XPOSE_ROWS)
        for half in range(2):
            vs = [yf_ref[g, pl.ds(r0, XPOSE_ROWS), half * LANES:(half + 1) * LANES]
                  for g in range(GROUPS_PER_BLOCK)]
            ts = _piece_transpose(vs)
            for k in range(GROUPS_PER_BLOCK):
                o_ref[GROUPS_PER_BLOCK * half + k, pl.ds(r0, XPOSE_ROWS), :] = ts[k]
        return carry

    lax.fori_loop(0, rows // XPOSE_ROWS, scatter, 0)


def _ssm(ug, m, be, cet, apr, api, dvec, batch):
    _, rows, _ = ug.shape
    n_chunks = rows // batch
    gb = GROUPS_PER_BLOCK
    col_blk = pl.BlockSpec((CHUNK, rows, LANES), lambda i: (0, 0, i))
    grp = lambda last2: pl.BlockSpec((gb,) + last2, lambda i: (i, 0, 0))
    return pl.pallas_call(
        functools.partial(_ssm_kernel, batch=batch, n_chunks=n_chunks),
        out_shape=jax.ShapeDtypeStruct((CHUNK, rows, SSM_WIDTH), BF16),
        grid=(N_GROUPS // gb,),
        in_specs=[col_blk, grp((CHUNK_W, CHUNK_W)), grp((CHUNK_W, 2 * STATE_DIM)),
                  grp((CHUNK_W, 2 * STATE_DIM)), grp((N_SCAN_STEPS, 2 * STATE_DIM)),
                  grp((N_SCAN_STEPS, 2 * STATE_DIM)), grp((1, CHUNK_W))],
        out_specs=col_blk,
        scratch_shapes=[pltpu.VMEM((gb, rows, CHUNK_W), BF16),
                        pltpu.VMEM((gb, rows, CHUNK_W), BF16),
                        pltpu.VMEM((batch, SCAN_PAD + n_chunks, 2 * STATE_DIM), F32)],
        compiler_params=_params(1),
        name="ssm_chunked",
    )(ug, m, be, cet, apr, api, dvec)


def kernel(x, norm_g, attn_w_in, attn_q_norm_g, attn_k_norm_g, attn_sinks, attn_w_out, ssm_w_in,
           ssm_log_dt, ssm_lam_re, ssm_lam_im, ssm_b_re, ssm_b_im, ssm_c_re, ssm_c_im, ssm_d,
           ssm_w_glu, ssm_w_out):
    batch, seq, d = x.shape
    m = batch * seq
    assert d == D_MODEL and seq % BLOCK == 0 and seq // CHUNK == 1 << N_SCAN_STEPS
    x2 = x.reshape(m, d)

    w_in = attn_w_in[0]
    q_gain = jnp.tile(attn_q_norm_g[0].astype(F32) * HEAD_DIM ** -0.5, Q_PER_KV).reshape(1, GROUP_W)
    k_gain = jnp.tile(attn_k_norm_g[0].astype(F32), N_KV_HEADS).reshape(1, KV_WIDTH)

    hn0 = _rmsnorm(x2, norm_g[0])
    q = _q_proj(hn0, w_in, q_gain)
    kv = _kv_proj(hn0, w_in, k_gain)
    gate = _gate_proj(hn0, w_in)
    og = _attention(q, kv, gate, attn_sinks[0].astype(F32), batch, seq)
    h1 = _attn_out_proj(og, attn_w_out[0].reshape(N_KV_HEADS, GROUP_W, D_MODEL), x2)

    nj = m // CHUNK
    hn1 = _rmsnorm_chunk_major(h1, norm_g[1])
    ug = _ssm_in_proj(hn1, ssm_w_in[0])
    mt, be, cet, apr, api = _ssm_prep(ssm_lam_re[0], ssm_lam_im[0], ssm_log_dt[0],
                                      ssm_b_re[0], ssm_b_im[0], ssm_c_re[0], ssm_c_im[0])
    dvec = jnp.tile(ssm_d[0].astype(F32).reshape(N_GROUPS, 1, GROUP_SIZE), (1, 1, CHUNK))
    y = _ssm(ug.reshape(CHUNK, nj, 2 * SSM_WIDTH), mt, be, cet, apr, api, dvec, batch)
    y = y.reshape(m, SSM_WIDTH)
    p = _glu(y, ssm_w_glu[0].astype(BF16), ug)
    out = _ssm_out_proj(p, ssm_w_out[0].astype(BF16), h1)
    return out.reshape(batch, seq, d)
```

```python
import functools

import jax
import jax.numpy as jnp
from jax import lax
from jax.experimental import pallas as pl
from jax.experimental.pallas import tpu as pltpu

F32 = jnp.float32
BF16 = jnp.bfloat16

D_MODEL = 4096
HEAD_DIM = 64
N_Q_HEADS = 64
N_KV_HEADS = 8
Q_PER_KV = 8
N_PAIRS = Q_PER_KV // 2
KV_WIDTH = N_KV_HEADS * HEAD_DIM
GROUP_W = Q_PER_KV * HEAD_DIM
BLOCK = 128
STACK = Q_PER_KV * BLOCK
SSM_WIDTH = 8192
GROUP_SIZE = 16
N_GROUPS = 512
STATE_DIM = 64
CHUNK = 16
CHUNK_W = CHUNK * GROUP_SIZE
RMS_EPS = 1e-6
NEG_INF = -1e30

LANES = 128
VMEM_LIMIT = 56 * 1024 * 1024
MXU_W = 256
GROUPS_PER_BLOCK = LANES // GROUP_SIZE

NT = (((1,), (1,)), ((), ()))


def _lane_roll(x, shift):
    return jnp.concatenate([x[:, LANES - shift:], x[:, :LANES - shift]], axis=1)


def _cast_weight_once(w_ref, wb_ref):
    @pl.when(pl.program_id(1) == 0)
    def _():
        wb_ref[...] = w_ref[...].astype(BF16)


def _params(n_grid_dims):
    return pltpu.CompilerParams(
        dimension_semantics=("arbitrary",) * n_grid_dims, vmem_limit_bytes=VMEM_LIMIT)


def _rms(x, g):
    var = jnp.mean(x * x, axis=-1, keepdims=True)
    return x * lax.rsqrt(var + RMS_EPS) * g


def _rmsnorm_kernel(x_ref, g_ref, o_ref):
    o_ref[...] = _rms(x_ref[...], g_ref[...]).astype(o_ref.dtype)


def _rmsnorm(x, g, tm=256):
    m, d = x.shape
    return pl.pallas_call(
        _rmsnorm_kernel,
        out_shape=jax.ShapeDtypeStruct((m, d), BF16),
        grid=(m // tm,),
        in_specs=[pl.BlockSpec((tm, d), lambda i: (i, 0)),
                  pl.BlockSpec((1, d), lambda i: (0, 0))],
        out_specs=pl.BlockSpec((tm, d), lambda i: (i, 0)),
        compiler_params=_params(1),
        name="rmsnorm",
    )(x, g.reshape(1, d))


def _rmsnorm_chunk_major_kernel(x_ref, g_ref, o_ref, cols_ref):
    hn = _rms(x_ref[...], g_ref[...])
    n_cols = hn.shape[1] // LANES
    nj = hn.shape[0] // CHUNK
    for c in range(n_cols):
        cols_ref[c] = hn[:, c * LANES:(c + 1) * LANES]
    for s in range(CHUNK):
        for c in range(n_cols):
            o_ref[s, :, c * LANES:(c + 1) * LANES] = cols_ref[
                c, pl.ds(s, nj, stride=CHUNK), :].astype(o_ref.dtype)


def _rmsnorm_chunk_major(x, g, tm=256):
    m, d = x.shape
    nj = tm // CHUNK
    out = pl.pallas_call(
        _rmsnorm_chunk_major_kernel,
        out_shape=jax.ShapeDtypeStruct((CHUNK, m // CHUNK, d), BF16),
        grid=(m // tm,),
        in_specs=[pl.BlockSpec((tm, d), lambda i: (i, 0)),
                  pl.BlockSpec((1, d), lambda i: (0, 0))],
        out_specs=pl.BlockSpec((CHUNK, nj, d), lambda i: (0, i, 0)),
        scratch_shapes=[pltpu.VMEM((d // LANES, tm, LANES), F32)],
        compiler_params=_params(1),
        name="rmsnorm_chunk_major",
    )(x, g.reshape(1, d))
    return out.reshape(m, d)


def _head_rmsnorm(acc, ones_bd, gain):
    outs = []
    for c in range(acc.shape[1] // MXU_W):
        a = acc[:, c * MXU_W:(c + 1) * MXU_W]
        sq = a * a
        hi = sq.astype(BF16)
        lo = (sq - hi.astype(F32)).astype(BF16)
        ss = (jnp.dot(hi, ones_bd, preferred_element_type=F32)
              + jnp.dot(lo, ones_bd, preferred_element_type=F32))
        inv = lax.rsqrt(ss * (1.0 / HEAD_DIM) + RMS_EPS)
        outs.append(a * inv * gain[:, c * MXU_W:(c + 1) * MXU_W])
    return outs[0] if len(outs) == 1 else jnp.concatenate(outs, axis=1)


def _q_proj_kernel(x_ref, w_ref, e_ref, g_ref, o_ref, wb_ref):
    _cast_weight_once(w_ref, wb_ref)
    acc = jnp.dot(x_ref[...], wb_ref[...], preferred_element_type=F32)
    o_ref[...] = _head_rmsnorm(acc, e_ref[...], g_ref[...]).astype(o_ref.dtype)


def _kv_proj_kernel(x_ref, w_ref, e_ref, g_ref, o_ref, wb_ref):
    _cast_weight_once(w_ref, wb_ref)
    acc = jnp.dot(x_ref[...], wb_ref[...], preferred_element_type=F32)
    kn = _head_rmsnorm(acc[:, :KV_WIDTH], e_ref[...], g_ref[...])
    for h in range(N_KV_HEADS):
        o_ref[h, :, 0:HEAD_DIM] = kn[:, h * HEAD_DIM:(h + 1) * HEAD_DIM].astype(o_ref.dtype)
        o_ref[h, :, HEAD_DIM:2 * HEAD_DIM] = acc[
            :, KV_WIDTH + h * HEAD_DIM:KV_WIDTH + (h + 1) * HEAD_DIM].astype(o_ref.dtype)


def _cast_proj_kernel(x_ref, w_ref, o_ref, wb_ref):
    _cast_weight_once(w_ref, wb_ref)
    o_ref[...] = jnp.dot(x_ref[...], wb_ref[...], preferred_element_type=F32).astype(o_ref.dtype)


def _ones_block_diag():
    r = jnp.arange(MXU_W) // HEAD_DIM
    return (r[:, None] == r[None, :]).astype(BF16)


Q_COL0 = 0
KV_COL0 = D_MODEL
GATE_COL0 = D_MODEL + 2 * KV_WIDTH


def _q_proj(hn, w_in, gain, tm=1024):
    m, k = hn.shape
    return pl.pallas_call(
        _q_proj_kernel,
        out_shape=jax.ShapeDtypeStruct((N_KV_HEADS, m, GROUP_W), BF16),
        grid=(N_KV_HEADS, m // tm),
        in_specs=[pl.BlockSpec((tm, k), lambda j, i: (i, 0)),
                  pl.BlockSpec((k, GROUP_W), lambda j, i: (0, Q_COL0 // GROUP_W + j)),
                  pl.BlockSpec((MXU_W, MXU_W), lambda j, i: (0, 0)),
                  pl.BlockSpec((1, GROUP_W), lambda j, i: (0, 0))],
        out_specs=pl.BlockSpec((None, tm, GROUP_W), lambda j, i: (j, i, 0)),
        scratch_shapes=[pltpu.VMEM((k, GROUP_W), BF16)],
        compiler_params=_params(2),
        name="attn_q_proj",
    )(hn, w_in, _ones_block_diag(), gain)


def _kv_proj(hn, w_in, gain, tm=512):
    m, k = hn.shape
    n = 2 * KV_WIDTH
    return pl.pallas_call(
        _kv_proj_kernel,
        out_shape=jax.ShapeDtypeStruct((N_KV_HEADS, m, 2 * HEAD_DIM), BF16),
        grid=(1, m // tm),
        in_specs=[pl.BlockSpec((tm, k), lambda j, i: (i, 0)),
                  pl.BlockSpec((k, n), lambda j, i: (0, KV_COL0 // n)),
                  pl.BlockSpec((MXU_W, MXU_W), lambda j, i: (0, 0)),
                  pl.BlockSpec((1, KV_WIDTH), lambda j, i: (0, 0))],
        out_specs=pl.BlockSpec((N_KV_HEADS, tm, 2 * HEAD_DIM), lambda j, i: (0, i, 0)),
        scratch_shapes=[pltpu.VMEM((k, n), BF16)],
        compiler_params=_params(2),
        name="attn_kv_proj",
    )(hn, w_in, _ones_block_diag(), gain)


def _gate_proj(hn, w_in, tm=1024):
    m, k = hn.shape
    return pl.pallas_call(
        _cast_proj_kernel,
        out_shape=jax.ShapeDtypeStruct((N_KV_HEADS, m, GROUP_W), BF16),
        grid=(N_KV_HEADS, m // tm),
        in_specs=[pl.BlockSpec((tm, k), lambda j, i: (i, 0)),
                  pl.BlockSpec((k, GROUP_W), lambda j, i: (0, GATE_COL0 // GROUP_W + j))],
        out_specs=pl.BlockSpec((None, tm, GROUP_W), lambda j, i: (j, i, 0)),
        scratch_shapes=[pltpu.VMEM((k, GROUP_W), BF16)],
        compiler_params=_params(2),
        name="attn_gate_proj",
    )(hn, w_in)


def _attn_kernel(sink_ref, alibi_ref, q_ref, kvc_ref, kvp_ref, gate_ref, o_ref):
    n = pl.program_id(1)
    row = lax.broadcasted_iota(jnp.int32, (BLOCK, BLOCK), 0)
    col = lax.broadcasted_iota(jnp.int32, (BLOCK, BLOCK), 1)
    lower = col <= row
    valid = jnp.logical_or(lower, n > 0)
    left = col < HEAD_DIM
    ones = jnp.ones((2 * BLOCK, LANES), BF16)

    def split_kv(kv):
        swapped = _lane_roll(kv, HEAD_DIM)
        zero = jnp.zeros_like(kv)
        return jnp.where(left, kv, zero), jnp.where(left, zero, swapped), jnp.where(left, swapped, kv)

    def kv_head(kvh, carry):
        kc_even, kc_odd, vvc = split_kv(kvc_ref[kvh])
        kp_even, kp_odd, vvp = split_kv(kvp_ref[kvh])
        qp = jnp.concatenate(
            [q_ref[kvh, :, a * LANES:(a + 1) * LANES] for a in range(N_PAIRS)], axis=0)
        keys = jnp.concatenate([kc_even, kc_odd, kp_even, kp_odd], axis=0)
        s_all = lax.dot_general(qp, keys, NT, preferred_element_type=F32)
        v_ones = jnp.concatenate([jnp.concatenate([vvc, vvp], axis=0), ones], axis=1)
        ps, sink_terms = [], []
        for par in range(2):
            for a in range(N_PAIRS):
                rows = slice(a * BLOCK, (a + 1) * BLOCK)
                sc = s_all[rows, par * LANES:(par + 1) * LANES]
                sp = s_all[rows, (2 + par) * LANES:(3 + par) * LANES]
                blk = par * N_PAIRS + a
                s = jnp.where(lower, sc, sp) + alibi_ref[kvh, blk * BLOCK:(blk + 1) * BLOCK, :]
                s = jnp.where(valid, s, NEG_INF)
                sink = sink_ref[kvh * Q_PER_KV + 2 * a + par]
                m = jnp.maximum(jnp.max(s, axis=-1, keepdims=True), sink)
                p = jnp.exp(s - m)
                ps.append(jnp.concatenate([jnp.where(lower, p, 0.0), jnp.where(lower, 0.0, p)],
                                          axis=1).astype(BF16))
                sink_terms.append(jnp.exp(sink - m))
        nd = jnp.dot(jnp.concatenate(ps, axis=0), v_ones, preferred_element_type=F32)
        for a in range(N_PAIRS):
            o = []
            for par in range(2):
                blk = par * N_PAIRS + a
                r = nd[blk * BLOCK:(blk + 1) * BLOCK]
                o.append(r[:, :LANES] / (r[:, LANES:] + sink_terms[blk]))
            gt = gate_ref[kvh, :, a * LANES:(a + 1) * LANES].astype(F32)
            o_ref[kvh, :, a * LANES:(a + 1) * LANES] = (
                jnp.where(left, o[0], o[1]) * (gt * jax.nn.sigmoid(gt))).astype(o_ref.dtype)
        return carry

    lax.fori_loop(0, N_KV_HEADS, kv_head, 0, unroll=True)


def _alibi_table():
    qi = jnp.arange(BLOCK)[:, None]
    kj = jnp.arange(BLOCK)[None, :]
    dist = jnp.where(kj <= qi, qi - kj, BLOCK + qi - kj).astype(F32)
    slopes = jnp.exp2(-8.0 * jnp.arange(1, N_Q_HEADS + 1, dtype=F32) / N_Q_HEADS)
    slopes = slopes.reshape(N_KV_HEADS, N_PAIRS, 2).transpose(0, 2, 1)
    return (-slopes[:, :, :, None, None] * dist).reshape(N_KV_HEADS, STACK, BLOCK)


def _attention(q, kv, gate, sinks, batch, seq):
    nb = seq // BLOCK
    m = batch * seq
    blk = lambda b, n: (0, b * nb + n, 0)
    prev = lambda b, n: (0, b * nb + jnp.maximum(n - 1, 0), 0)
    return pl.pallas_call(
        _attn_kernel,
        out_shape=jax.ShapeDtypeStruct((N_KV_HEADS, m, GROUP_W), BF16),
        grid=(batch, nb),
        in_specs=[pl.BlockSpec(memory_space=pltpu.SMEM),
                  pl.BlockSpec((N_KV_HEADS, STACK, BLOCK), lambda b, n: (0, 0, 0)),
                  pl.BlockSpec((N_KV_HEADS, BLOCK, GROUP_W), blk),
                  pl.BlockSpec((N_KV_HEADS, BLOCK, 2 * HEAD_DIM), blk),
                  pl.BlockSpec((N_KV_HEADS, BLOCK, 2 * HEAD_DIM), prev),
                  pl.BlockSpec((N_KV_HEADS, BLOCK, GROUP_W), blk)],
        out_specs=pl.BlockSpec((N_KV_HEADS, BLOCK, GROUP_W), blk),
        compiler_params=_params(2),
        name="band_attention",
    )(sinks, _alibi_table(), q, kv, kv, gate)


def _attn_out_kernel(x_ref, w_ref, r_ref, o_ref, wb_ref):
    _cast_weight_once(w_ref, wb_ref)
    acc = r_ref[...]
    for h in range(N_KV_HEADS):
        acc = acc + jnp.dot(x_ref[h], wb_ref[h], preferred_element_type=F32)
    o_ref[...] = acc


def _attn_out_proj(og, w, resid, tm=1024, tn=512):
    _, m, _ = og.shape
    n = w.shape[2]
    return pl.pallas_call(
        _attn_out_kernel,
        out_shape=jax.ShapeDtypeStruct((m, n), F32),
        grid=(n // tn, m // tm),
        in_specs=[pl.BlockSpec((N_KV_HEADS, tm, GROUP_W), lambda j, i: (0, i, 0)),
                  pl.BlockSpec((N_KV_HEADS, GROUP_W, tn), lambda j, i: (0, 0, j)),
                  pl.BlockSpec((tm, tn), lambda j, i: (i, j))],
        out_specs=pl.BlockSpec((tm, tn), lambda j, i: (i, j)),
        scratch_shapes=[pltpu.VMEM((N_KV_HEADS, GROUP_W, tn), BF16)],
        compiler_params=_params(2),
        name="attn_out_proj",
    )(og, w, resid)


def _ssm_in_proj(x, w, tm=512, tn=1024):
    m, k = x.shape
    n = w.shape[1]
    return pl.pallas_call(
        _cast_proj_kernel,
        out_shape=jax.ShapeDtypeStruct((m, n), BF16),
        grid=(n // tn, m // tm),
        in_specs=[pl.BlockSpec((tm, k), lambda j, i: (i, 0)),
                  pl.BlockSpec((k, tn), lambda j, i: (0, j))],
        out_specs=pl.BlockSpec((tm, tn), lambda j, i: (i, j)),
        scratch_shapes=[pltpu.VMEM((k, tn), BF16)],
        compiler_params=_params(2),
        name="ssm_in_proj",
    )(x, w)


def _glu_kernel(x_ref, w_ref, y_ref, gate_ref, o_ref):
    gt = gate_ref[...].astype(F32)
    gated = y_ref[...].astype(F32) * (gt * jax.nn.sigmoid(gt))
    z = jnp.dot(x_ref[...], w_ref[...], preferred_element_type=F32)
    o_ref[...] = (gated * jax.nn.sigmoid(z)).astype(o_ref.dtype)


def _glu(y, w, ug, tm=1024, tn=256):
    m, k = y.shape
    n = w.shape[1]
    gate_off = n // tn
    return pl.pallas_call(
        _glu_kernel,
        out_shape=jax.ShapeDtypeStruct((m, n), BF16),
        grid=(m // tm, n // tn),
        in_specs=[pl.BlockSpec((tm, k), lambda i, j: (i, 0)),
                  pl.BlockSpec((k, tn), lambda i, j: (0, j)),
                  pl.BlockSpec((tm, tn), lambda i, j: (i, j)),
                  pl.BlockSpec((tm, tn), lambda i, j: (i, j + gate_off))],
        out_specs=pl.BlockSpec((tm, tn), lambda i, j: (i, j)),
        compiler_params=_params(2),
        name="ssm_glu",
    )(y, w, y, ug)


def _ssm_out_kernel(x_ref, w_ref, r_ref, o_ref, cols_ref):
    nj = x_ref.shape[1]
    x = x_ref[...].reshape(CHUNK * nj, x_ref.shape[2])
    acc = jnp.dot(x, w_ref[...], preferred_element_type=F32)
    for c in range(acc.shape[1] // LANES):
        for s in range(CHUNK):
            cols_ref[c, pl.ds(s, nj, stride=CHUNK), :] = acc[
                s * nj:(s + 1) * nj, c * LANES:(c + 1) * LANES]
        o_ref[:, c * LANES:(c + 1) * LANES] = cols_ref[c] + r_ref[:, c * LANES:(c + 1) * LANES]


def _ssm_out_proj(x, w, resid, tm=1024, tn=256):
    m, k = x.shape
    n = w.shape[1]
    nj = tm // CHUNK
    return pl.pallas_call(
        _ssm_out_kernel,
        out_shape=jax.ShapeDtypeStruct((m, n), F32),
        grid=(m // tm, n // tn),
        in_specs=[pl.BlockSpec((CHUNK, nj, k), lambda i, j: (0, i, 0)),
                  pl.BlockSpec((k, tn), lambda i, j: (0, j)),
                  pl.BlockSpec((tm, tn), lambda i, j: (i, j))],
        out_specs=pl.BlockSpec((tm, tn), lambda i, j: (i, j)),
        scratch_shapes=[pltpu.VMEM((tn // LANES, tm, LANES), F32)],
        compiler_params=_params(2),
        name="ssm_out_proj",
    )(x.reshape(CHUNK, m // CHUNK, k), w, resid)


N_SCAN_STEPS = 8
SCAN_BLOCK = 8
N_LOCAL_STEPS = 3


def _ssm_prep_kernel(lr_ref, li_ref, ldt_ref, btr_ref, bti_ref, cr_ref, ci_ref,
                     m_ref, be_ref, cet_ref, apr_ref, api_ref, bpr_ref, bpi_ref, *, gb):
    tau = lax.broadcasted_iota(jnp.int32, (24, 2 * STATE_DIM), 0).astype(F32)
    quarter = jnp.where(lax.broadcasted_iota(jnp.int32, (24, 2 * STATE_DIM), 1) < STATE_DIM, 0.0, 0.5 * jnp.pi)
    lane = lax.broadcasted_iota(jnp.int32, (GROUP_SIZE, CHUNK_W), 1)

    def one_group(g, carry):
        lr = lr_ref[g]
        li = li_ref[g]
        dt = jnp.exp(ldt_ref[g])
        lr2 = jnp.concatenate([lr, lr], axis=1)
        li2 = jnp.concatenate([li, li], axis=1)
        dt2 = jnp.concatenate([dt, dt], axis=1)
        powers = jnp.exp(tau * (lr2 * dt2)) * jnp.cos(tau * (li2 * dt2) - quarter)
        pr = powers[:, :STATE_DIM]
        pi = powers[:, STATE_DIM:]
        ar, ai = pr[1:2], pi[1:2]
        den = lr * lr + li * li
        xr = ar - 1.0
        wr = (xr * lr + ai * li) / den
        wi = (ai * lr - xr * li) / den
        btr, bti = btr_ref[g], bti_ref[g]
        br = btr * wr - bti * wi
        bi = btr * wi + bti * wr
        cr, ci = cr_ref[g], ci_ref[g]

        ca_r, ca_i = [], []
        for t in range(CHUNK + 1):
            ca_r.append(cr * pr[t:t + 1] - ci * pi[t:t + 1])
            ca_i.append(cr * pi[t:t + 1] + ci * pr[t:t + 1])

        rt = jnp.concatenate([jnp.concatenate(ca_r[:CHUNK], axis=0),
                              jnp.concatenate(ca_i[:CHUNK], axis=0)], axis=1)
        bcat = jnp.concatenate([br, -bi], axis=1)
        kp = lax.dot_general(bcat.astype(BF16), rt.astype(BF16), NT,
                             preferred_element_type=F32)
        for s in range(CHUNK):
            blk = kp if s == 0 else pltpu.roll(kp, GROUP_SIZE * s, 1)
            blk = jnp.where(lane >= GROUP_SIZE * s, blk, 0.0)
            m_ref[g, s * GROUP_SIZE:(s + 1) * GROUP_SIZE, :] = blk.astype(m_ref.dtype)
            q = CHUNK - 1 - s
            be_re = br * pr[q:q + 1] - bi * pi[q:q + 1]
            be_im = br * pi[q:q + 1] + bi * pr[q:q + 1]
            be_ref[g, s * GROUP_SIZE:(s + 1) * GROUP_SIZE, :] = jnp.concatenate(
                [be_re, be_im], axis=1).astype(be_ref.dtype)
            cet_ref[g, s * GROUP_SIZE:(s + 1) * GROUP_SIZE, :] = jnp.concatenate(
                [ca_r[s + 1], -ca_i[s + 1]], axis=1).astype(cet_ref.dtype)

        sq_r, sq_i = [pr[CHUNK:CHUNK + 1]], [pi[CHUNK:CHUNK + 1]]
        for _ in range(N_SCAN_STEPS - 1):
            xr, xi = sq_r[-1], sq_i[-1]
            sq_r.append(xr * xr - xi * xi)
            sq_i.append(2.0 * (xr * xi))
        mul_r, mul_i = [jnp.ones_like(sq_r[0]), sq_r[0]], [jnp.zeros_like(sq_r[0]), sq_i[0]]
        for _ in range(SCAN_BLOCK - 2):
            xr, xi = mul_r[-1], mul_i[-1]
            mul_r.append(xr * sq_r[0] - xi * sq_i[0])
            mul_i.append(xr * sq_i[0] + xi * sq_r[0])
        sr, si = jnp.concatenate(sq_r, axis=0), jnp.concatenate(sq_i, axis=0)
        rr, ri = jnp.concatenate(mul_r, axis=0), jnp.concatenate(mul_i, axis=0)
        apr_ref[g] = jnp.concatenate([sr, sr], axis=1)
        api_ref[g] = jnp.concatenate([-si, si], axis=1)
        bpr_ref[g] = jnp.concatenate([rr, rr], axis=1)
        bpi_ref[g] = jnp.concatenate([-ri, ri], axis=1)
        return carry

    lax.fori_loop(0, gb, one_group, 0, unroll=4)


def _ssm_prep(lam_re, lam_im, log_dt, b_re, b_im, c_re, c_im, gb=8):
    g = N_GROUPS
    row = lambda a: a.reshape(g, 1, STATE_DIM)
    ldt = jnp.broadcast_to(log_dt.reshape(g, 1, 1), (g, 1, STATE_DIM))
    btr = jnp.swapaxes(b_re, 1, 2)
    bti = jnp.swapaxes(b_im, 1, 2)
    vec = pl.BlockSpec((gb, 1, STATE_DIM), lambda i: (i, 0, 0))
    mat = pl.BlockSpec((gb, GROUP_SIZE, STATE_DIM), lambda i: (i, 0, 0))
    return pl.pallas_call(
        functools.partial(_ssm_prep_kernel, gb=gb),
        out_shape=[jax.ShapeDtypeStruct((g, CHUNK_W, CHUNK_W), BF16),
                   jax.ShapeDtypeStruct((g, CHUNK_W, 2 * STATE_DIM), BF16),
                   jax.ShapeDtypeStruct((g, CHUNK_W, 2 * STATE_DIM), BF16),
                   jax.ShapeDtypeStruct((g, N_SCAN_STEPS, 2 * STATE_DIM), F32),
                   jax.ShapeDtypeStruct((g, N_SCAN_STEPS, 2 * STATE_DIM), F32),
                   jax.ShapeDtypeStruct((g, SCAN_BLOCK, 2 * STATE_DIM), F32),
                   jax.ShapeDtypeStruct((g, SCAN_BLOCK, 2 * STATE_DIM), F32)],
        grid=(g // gb,),
        in_specs=[vec, vec, vec, mat, mat, mat, mat],
        out_specs=[pl.BlockSpec((gb, CHUNK_W, CHUNK_W), lambda i: (i, 0, 0)),
                   pl.BlockSpec((gb, CHUNK_W, 2 * STATE_DIM), lambda i: (i, 0, 0)),
                   pl.BlockSpec((gb, CHUNK_W, 2 * STATE_DIM), lambda i: (i, 0, 0)),
                   pl.BlockSpec((gb, N_SCAN_STEPS, 2 * STATE_DIM), lambda i: (i, 0, 0)),
                   pl.BlockSpec((gb, N_SCAN_STEPS, 2 * STATE_DIM), lambda i: (i, 0, 0)),
                   pl.BlockSpec((gb, SCAN_BLOCK, 2 * STATE_DIM), lambda i: (i, 0, 0)),
                   pl.BlockSpec((gb, SCAN_BLOCK, 2 * STATE_DIM), lambda i: (i, 0, 0))],
        compiler_params=_params(1),
        name="ssm_prep",
    )(row(lam_re), row(lam_im), ldt, btr, bti, c_re, c_im)


XPOSE_ROWS = 256


def _piece_transpose(vs):
    lane = lax.broadcasted_iota(jnp.int32, vs[0].shape, 1)
    vs = list(vs)
    for d in (4, 2, 1):
        w = GROUP_SIZE * d
        keep = (lane & w) == 0
        for k in range(GROUPS_PER_BLOCK):
            if k & d:
                continue
            a, b = vs[k], vs[k + d]
            vs[k] = jnp.where(keep, a, _lane_roll(b, w))
            vs[k + d] = jnp.where(keep, _lane_roll(a, LANES - w), b)
    return vs


def _cmul(x, mr, mi):
    return x * mr + pltpu.roll(x, STATE_DIM, 1) * mi


def _ssm_kernel(x_ref, m_ref, be_ref, cet_ref, apr_ref, api_ref, bpr_ref, bpi_ref, d_ref, o_ref,
                uf_ref, yf_ref, xs_ref, cs_ref, *, batch, n_chunks):
    rows = batch * n_chunks
    n_blocks = n_chunks // SCAN_BLOCK

    def gather(i, carry):
        r0 = pl.multiple_of(i * XPOSE_ROWS, XPOSE_ROWS)
        for half in range(2):
            vs = [x_ref[GROUPS_PER_BLOCK * half + k, pl.ds(r0, XPOSE_ROWS), :]
                  for k in range(GROUPS_PER_BLOCK)]
            ts = _piece_transpose(vs)
            for g in range(GROUPS_PER_BLOCK):
                uf_ref[g, pl.ds(r0, XPOSE_ROWS), half * LANES:(half + 1) * LANES] = ts[g]
        return carry

    lax.fori_loop(0, rows // XPOSE_ROWS, gather, 0)

    n_all = rows // SCAN_BLOCK
    row = lax.broadcasted_iota(jnp.int32, (rows, 2 * STATE_DIM), 0)
    in_block = row & (SCAN_BLOCK - 1)
    brow = lax.broadcasted_iota(jnp.int32, (n_all, 2 * STATE_DIM), 0)
    in_seq = brow & (n_blocks - 1)

    def shift_rows(x, d, pos):
        return jnp.where(pos >= d, pltpu.roll(x, d, 0), 0.0)

    def one_group(g, carry):
        u = uf_ref[g]
        y = jnp.dot(u, m_ref[g], preferred_element_type=F32)
        x = jnp.dot(u, be_ref[g], preferred_element_type=F32)
        apr, api = apr_ref[g], api_ref[g]
        for k in range(N_LOCAL_STEPS):
            x = x + _cmul(shift_rows(x, 1 << k, in_block), apr[k:k + 1], api[k:k + 1])
        xs_ref[...] = x
        c = xs_ref[pl.ds(SCAN_BLOCK - 1, n_all, stride=SCAN_BLOCK), :]
        for k in range(N_LOCAL_STEPS, N_SCAN_STEPS):
            c = c + _cmul(shift_rows(c, 1 << (k - N_LOCAL_STEPS), in_seq), apr[k:k + 1], api[k:k + 1])
        cs_ref[...] = shift_rows(c, 1, in_seq)
        bpr, bpi = bpr_ref[g], bpi_ref[g]
        carried = [_cmul(jnp.broadcast_to(cs_ref[mblk:mblk + 1, :], (SCAN_BLOCK, 2 * STATE_DIM)), bpr, bpi)
                   for mblk in range(n_all)]
        e = (shift_rows(x, 1, in_block) + jnp.concatenate(carried, axis=0)).astype(BF16)
        y = y + lax.dot_general(e, cet_ref[g], NT, preferred_element_type=F32)
        y = y + d_ref[g] * u.astype(F32)
        yf_ref[g] = jax.nn.gelu(y).astype(yf_ref.dtype)
        return carry

    lax.fori_loop(0, GROUPS_PER_BLOCK, one_group, 0, unroll=2)

    def scatter(i, carry):
        r0 = pl.multiple_of(i * XPOSE_ROWS, XPOSE_ROWS)
        for half in range(2):
            vs = [yf_ref[g, pl.ds(r0, XPOSE_ROWS), half * LANES:(half + 1) * LANES]
                  for g in range(GROUPS_PER_BLOCK)]
            ts = _piece_transpose(vs)
            for k in range(GROUPS_PER_BLOCK):
                o_ref[GROUPS_PER_BLOCK * half + k, pl.ds(r0, XPOSE_ROWS), :] = ts[k]
        return carry

    lax.fori_loop(0, rows // XPOSE_ROWS, scatter, 0)


def _ssm(ug, m, be, cet, apr, api, bpr, bpi, dvec, batch):
    _, rows, _ = ug.shape
    n_chunks = rows // batch
    gb = GROUPS_PER_BLOCK
    col_blk = pl.BlockSpec((CHUNK, rows, LANES), lambda i: (0, 0, i))
    grp = lambda last2: pl.BlockSpec((gb,) + last2, lambda i: (i, 0, 0))
    return pl.pallas_call(
        functools.partial(_ssm_kernel, batch=batch, n_chunks=n_chunks),
        out_shape=jax.ShapeDtypeStruct((CHUNK, rows, SSM_WIDTH), BF16),
        grid=(N_GROUPS // gb,),
        in_specs=[col_blk, grp((CHUNK_W, CHUNK_W)), grp((CHUNK_W, 2 * STATE_DIM)),
                  grp((CHUNK_W, 2 * STATE_DIM)), grp((N_SCAN_STEPS, 2 * STATE_DIM)),
                  grp((N_SCAN_STEPS, 2 * STATE_DIM)), grp((SCAN_BLOCK, 2 * STATE_DIM)),
                  grp((SCAN_BLOCK, 2 * STATE_DIM)), grp((1, CHUNK_W))],
        out_specs=col_blk,
        scratch_shapes=[pltpu.VMEM((gb, rows, CHUNK_W), BF16),
                        pltpu.VMEM((gb, rows, CHUNK_W), BF16),
                        pltpu.VMEM((rows, 2 * STATE_DIM), F32),
                        pltpu.VMEM((rows // SCAN_BLOCK, 2 * STATE_DIM), F32)],
        compiler_params=_params(1),
        name="ssm_chunked",
    )(ug, m, be, cet, apr, api, bpr, bpi, dvec)


def kernel(x, norm_g, attn_w_in, attn_q_norm_g, attn_k_norm_g, attn_sinks, attn_w_out, ssm_w_in,
           ssm_log_dt, ssm_lam_re, ssm_lam_im, ssm_b_re, ssm_b_im, ssm_c_re, ssm_c_im, ssm_d,
           ssm_w_glu, ssm_w_out):
    batch, seq, d = x.shape
    m = batch * seq
    assert d == D_MODEL and seq % BLOCK == 0 and seq // CHUNK == 1 << N_SCAN_STEPS
    x2 = x.reshape(m, d)

    w_in = attn_w_in[0]
    q_gain = jnp.tile(attn_q_norm_g[0].astype(F32) * HEAD_DIM ** -0.5, Q_PER_KV).reshape(1, GROUP_W)
    k_gain = jnp.tile(attn_k_norm_g[0].astype(F32), N_KV_HEADS).reshape(1, KV_WIDTH)

    hn0 = _rmsnorm(x2, norm_g[0])
    q = _q_proj(hn0, w_in, q_gain)
    kv = _kv_proj(hn0, w_in, k_gain)
    gate = _gate_proj(hn0, w_in)
    og = _attention(q, kv, gate, attn_sinks[0].astype(F32), batch, seq)
    h1 = _attn_out_proj(og, attn_w_out[0].reshape(N_KV_HEADS, GROUP_W, D_MODEL), x2)

    nj = m // CHUNK
    hn1 = _rmsnorm_chunk_major(h1, norm_g[1])
    ug = _ssm_in_proj(hn1, ssm_w_in[0])
    mt, be, cet, apr, api, bpr, bpi = _ssm_prep(ssm_lam_re[0], ssm_lam_im[0], ssm_log_dt[0],
                                      ssm_b_re[0], ssm_b_im[0], ssm_c_re[0], ssm_c_im[0])
    dvec = jnp.tile(ssm_d[0].astype(F32).reshape(N_GROUPS, 1, GROUP_SIZE), (1, 1, CHUNK))
    y = _ssm(ug.reshape(CHUNK, nj, 2 * SSM_WIDTH), mt, be, cet, apr, api, bpr, bpi, dvec, batch)
    y = y.reshape(m, SSM_WIDTH)
    p = _glu(y, ssm_w_glu[0].astype(BF16), ug)
    out = _ssm_out_proj(p, ssm_w_out[0].astype(BF16), h1)
    return out.reshape(batch, seq, d)
```

```python
import functools

import jax
import jax.numpy as jnp
from jax import lax
from jax.experimental import pallas as pl
from jax.experimental.pallas import tpu as pltpu

F32 = jnp.float32
BF16 = jnp.bfloat16

D_MODEL = 4096
HEAD_DIM = 64
N_Q_HEADS = 64
N_KV_HEADS = 8
Q_PER_KV = 8
N_PAIRS = Q_PER_KV // 2
KV_WIDTH = N_KV_HEADS * HEAD_DIM
GROUP_W = Q_PER_KV * HEAD_DIM
BLOCK = 128
STACK = Q_PER_KV * BLOCK
SSM_WIDTH = 8192
GROUP_SIZE = 16
N_GROUPS = 512
STATE_DIM = 64
CHUNK = 16
CHUNK_W = CHUNK * GROUP_SIZE
RMS_EPS = 1e-6
NEG_INF = -1e30

LANES = 128
VMEM_LIMIT = 60 * 1024 * 1024
MXU_W = 256
GROUPS_PER_BLOCK = LANES // GROUP_SIZE

NT = (((1,), (1,)), ((), ()))


def _lane_roll(x, shift):
    return jnp.concatenate([x[:, LANES - shift:], x[:, :LANES - shift]], axis=1)


def _cast_weight_once(w_ref, wb_ref):
    @pl.when(pl.program_id(1) == 0)
    def _():
        wb_ref[...] = w_ref[...].astype(BF16)


def _sigmoid(x):
    return 0.5 * jnp.tanh(0.5 * x) + 0.5


def _params(n_grid_dims):
    return pltpu.CompilerParams(
        dimension_semantics=("arbitrary",) * n_grid_dims, vmem_limit_bytes=VMEM_LIMIT)


def _rms(x, g):
    var = jnp.mean(x * x, axis=-1, keepdims=True)
    return x * lax.rsqrt(var + RMS_EPS) * g


def _rmsnorm_kernel(x_ref, g_ref, o_ref):
    o_ref[...] = _rms(x_ref[...], g_ref[...]).astype(o_ref.dtype)


def _rmsnorm(x, g, tm=256):
    m, d = x.shape
    return pl.pallas_call(
        _rmsnorm_kernel,
        out_shape=jax.ShapeDtypeStruct((m, d), BF16),
        grid=(m // tm,),
        in_specs=[pl.BlockSpec((tm, d), lambda i: (i, 0)),
                  pl.BlockSpec((1, d), lambda i: (0, 0))],
        out_specs=pl.BlockSpec((tm, d), lambda i: (i, 0)),
        compiler_params=_params(1),
        name="rmsnorm",
    )(x, g.reshape(1, d))


def _rmsnorm_chunk_major_kernel(x_ref, g_ref, o_ref, cols_ref):
    hn = _rms(x_ref[...], g_ref[...])
    n_cols = hn.shape[1] // LANES
    nj = hn.shape[0] // CHUNK
    for c in range(n_cols):
        cols_ref[c] = hn[:, c * LANES:(c + 1) * LANES]
    for s in range(CHUNK):
        for c in range(n_cols):
            o_ref[s, :, c * LANES:(c + 1) * LANES] = cols_ref[
                c, pl.ds(s, nj, stride=CHUNK), :].astype(o_ref.dtype)


def _rmsnorm_chunk_major(x, g, tm=256):
    m, d = x.shape
    nj = tm // CHUNK
    out = pl.pallas_call(
        _rmsnorm_chunk_major_kernel,
        out_shape=jax.ShapeDtypeStruct((CHUNK, m // CHUNK, d), BF16),
        grid=(m // tm,),
        in_specs=[pl.BlockSpec((tm, d), lambda i: (i, 0)),
                  pl.BlockSpec((1, d), lambda i: (0, 0))],
        out_specs=pl.BlockSpec((CHUNK, nj, d), lambda i: (0, i, 0)),
        scratch_shapes=[pltpu.VMEM((d // LANES, tm, LANES), F32)],
        compiler_params=_params(1),
        name="rmsnorm_chunk_major",
    )(x, g.reshape(1, d))
    return out.reshape(m, d)


def _head_rmsnorm(acc, ones_bd, gain):
    outs = []
    for c in range(acc.shape[1] // MXU_W):
        a = acc[:, c * MXU_W:(c + 1) * MXU_W]
        sq = a * a
        hi = sq.astype(BF16)
        lo = (sq - hi.astype(F32)).astype(BF16)
        ss = (jnp.dot(hi, ones_bd, preferred_element_type=F32)
              + jnp.dot(lo, ones_bd, preferred_element_type=F32))
        inv = lax.rsqrt(ss * (1.0 / HEAD_DIM) + RMS_EPS)
        outs.append(a * inv * gain[:, c * MXU_W:(c + 1) * MXU_W])
    return outs[0] if len(outs) == 1 else jnp.concatenate(outs, axis=1)


def _q_proj_kernel(x_ref, w_ref, e_ref, g_ref, o_ref, wb_ref):
    _cast_weight_once(w_ref, wb_ref)
    acc = jnp.dot(x_ref[...], wb_ref[...], preferred_element_type=F32)
    o_ref[...] = _head_rmsnorm(acc, e_ref[...], g_ref[...]).astype(o_ref.dtype)


def _kv_proj_kernel(x_ref, w_ref, e_ref, g_ref, o_ref, wb_ref):
    _cast_weight_once(w_ref, wb_ref)
    acc = jnp.dot(x_ref[...], wb_ref[...], preferred_element_type=F32)
    kn = _head_rmsnorm(acc[:, :KV_WIDTH], e_ref[...], g_ref[...])
    for h in range(N_KV_HEADS):
        o_ref[h, :, 0:HEAD_DIM] = kn[:, h * HEAD_DIM:(h + 1) * HEAD_DIM].astype(o_ref.dtype)
        o_ref[h, :, HEAD_DIM:2 * HEAD_DIM] = acc[
            :, KV_WIDTH + h * HEAD_DIM:KV_WIDTH + (h + 1) * HEAD_DIM].astype(o_ref.dtype)


def _cast_proj_kernel(x_ref, w_ref, o_ref, wb_ref):
    _cast_weight_once(w_ref, wb_ref)
    o_ref[...] = jnp.dot(x_ref[...], wb_ref[...], preferred_element_type=F32).astype(o_ref.dtype)


def _ones_block_diag():
    r = jnp.arange(MXU_W) // HEAD_DIM
    return (r[:, None] == r[None, :]).astype(BF16)


Q_COL0 = 0
KV_COL0 = D_MODEL
GATE_COL0 = D_MODEL + 2 * KV_WIDTH


def _q_proj(hn, w_in, gain, tm=1024):
    m, k = hn.shape
    return pl.pallas_call(
        _q_proj_kernel,
        out_shape=jax.ShapeDtypeStruct((N_KV_HEADS, m, GROUP_W), BF16),
        grid=(N_KV_HEADS, m // tm),
        in_specs=[pl.BlockSpec((tm, k), lambda j, i: (i, 0)),
                  pl.BlockSpec((k, GROUP_W), lambda j, i: (0, Q_COL0 // GROUP_W + j)),
                  pl.BlockSpec((MXU_W, MXU_W), lambda j, i: (0, 0)),
                  pl.BlockSpec((1, GROUP_W), lambda j, i: (0, 0))],
        out_specs=pl.BlockSpec((None, tm, GROUP_W), lambda j, i: (j, i, 0)),
        scratch_shapes=[pltpu.VMEM((k, GROUP_W), BF16)],
        compiler_params=_params(2),
        name="attn_q_proj",
    )(hn, w_in, _ones_block_diag(), gain)


def _kv_proj(hn, w_in, gain, tm=512):
    m, k = hn.shape
    n = 2 * KV_WIDTH
    return pl.pallas_call(
        _kv_proj_kernel,
        out_shape=jax.ShapeDtypeStruct((N_KV_HEADS, m, 2 * HEAD_DIM), BF16),
        grid=(1, m // tm),
        in_specs=[pl.BlockSpec((tm, k), lambda j, i: (i, 0)),
                  pl.BlockSpec((k, n), lambda j, i: (0, KV_COL0 // n)),
                  pl.BlockSpec((MXU_W, MXU_W), lambda j, i: (0, 0)),
                  pl.BlockSpec((1, KV_WIDTH), lambda j, i: (0, 0))],
        out_specs=pl.BlockSpec((N_KV_HEADS, tm, 2 * HEAD_DIM), lambda j, i: (0, i, 0)),
        scratch_shapes=[pltpu.VMEM((k, n), BF16)],
        compiler_params=_params(2),
        name="attn_kv_proj",
    )(hn, w_in, _ones_block_diag(), gain)


def _gate_proj(hn, w_in, tm=1024):
    m, k = hn.shape
    return pl.pallas_call(
        _cast_proj_kernel,
        out_shape=jax.ShapeDtypeStruct((N_KV_HEADS, m, GROUP_W), BF16),
        grid=(N_KV_HEADS, m // tm),
        in_specs=[pl.BlockSpec((tm, k), lambda j, i: (i, 0)),
                  pl.BlockSpec((k, GROUP_W), lambda j, i: (0, GATE_COL0 // GROUP_W + j))],
        out_specs=pl.BlockSpec((None, tm, GROUP_W), lambda j, i: (j, i, 0)),
        scratch_shapes=[pltpu.VMEM((k, GROUP_W), BF16)],
        compiler_params=_params(2),
        name="attn_gate_proj",
    )(hn, w_in)


def _attn_kernel(sink_ref, alibi_ref, q_ref, kvc_ref, kvp_ref, gate_ref, o_ref):
    n = pl.program_id(1)
    row = lax.broadcasted_iota(jnp.int32, (BLOCK, BLOCK), 0)
    col = lax.broadcasted_iota(jnp.int32, (BLOCK, BLOCK), 1)
    lower = col <= row
    valid = jnp.logical_or(lower, n > 0)
    left = col < HEAD_DIM
    ones = jnp.ones((2 * BLOCK, LANES), BF16)

    def split_kv(kv):
        swapped = _lane_roll(kv, HEAD_DIM)
        zero = jnp.zeros_like(kv)
        return jnp.where(left, kv, zero), jnp.where(left, zero, swapped), jnp.where(left, swapped, kv)

    def kv_head(kvh, carry):
        kc_even, kc_odd, vvc = split_kv(kvc_ref[kvh])
        kp_even, kp_odd, vvp = split_kv(kvp_ref[kvh])
        qp = jnp.concatenate(
            [q_ref[kvh, :, a * LANES:(a + 1) * LANES] for a in range(N_PAIRS)], axis=0)
        keys = jnp.concatenate([kc_even, kc_odd, kp_even, kp_odd], axis=0)
        s_all = lax.dot_general(qp, keys, NT, preferred_element_type=F32)
        v_ones = jnp.concatenate([jnp.concatenate([vvc, vvp], axis=0), ones], axis=1)
        ps, sink_terms = [], []
        for par in range(2):
            for a in range(N_PAIRS):
                rows = slice(a * BLOCK, (a + 1) * BLOCK)
                sc = s_all[rows, par * LANES:(par + 1) * LANES]
                sp = s_all[rows, (2 + par) * LANES:(3 + par) * LANES]
                blk = par * N_PAIRS + a
                s = jnp.where(lower, sc, sp) + alibi_ref[kvh, blk * BLOCK:(blk + 1) * BLOCK, :]
                s = jnp.where(valid, s, NEG_INF)
                sink = sink_ref[kvh * Q_PER_KV + 2 * a + par]
                m = jnp.maximum(jnp.max(s, axis=-1, keepdims=True), sink)
                p = jnp.exp(s - m)
                ps.append(jnp.concatenate([jnp.where(lower, p, 0.0), jnp.where(lower, 0.0, p)],
                                          axis=1).astype(BF16))
                sink_terms.append(jnp.exp(sink - m))
        nd = jnp.dot(jnp.concatenate(ps, axis=0), v_ones, preferred_element_type=F32)
        for a in range(N_PAIRS):
            o = []
            for par in range(2):
                blk = par * N_PAIRS + a
                r = nd[blk * BLOCK:(blk + 1) * BLOCK]
                o.append(r[:, :LANES] / (r[:, LANES:] + sink_terms[blk]))
            gt = gate_ref[kvh, :, a * LANES:(a + 1) * LANES].astype(F32)
            o_ref[kvh, :, a * LANES:(a + 1) * LANES] = (
                jnp.where(left, o[0], o[1]) * (gt * _sigmoid(gt))).astype(o_ref.dtype)
        return carry

    lax.fori_loop(0, N_KV_HEADS, kv_head, 0, unroll=True)


def _alibi_table():
    qi = jnp.arange(BLOCK)[:, None]
    kj = jnp.arange(BLOCK)[None, :]
    dist = jnp.where(kj <= qi, qi - kj, BLOCK + qi - kj).astype(F32)
    slopes = jnp.exp2(-8.0 * jnp.arange(1, N_Q_HEADS + 1, dtype=F32) / N_Q_HEADS)
    slopes = slopes.reshape(N_KV_HEADS, N_PAIRS, 2).transpose(0, 2, 1)
    return (-slopes[:, :, :, None, None] * dist).reshape(N_KV_HEADS, STACK, BLOCK)


def _attention(q, kv, gate, sinks, batch, seq):
    nb = seq // BLOCK
    m = batch * seq
    blk = lambda b, n: (0, b * nb + n, 0)
    prev = lambda b, n: (0, b * nb + jnp.maximum(n - 1, 0), 0)
    return pl.pallas_call(
        _attn_kernel,
        out_shape=jax.ShapeDtypeStruct((N_KV_HEADS, m, GROUP_W), BF16),
        grid=(batch, nb),
        in_specs=[pl.BlockSpec(memory_space=pltpu.SMEM),
                  pl.BlockSpec((N_KV_HEADS, STACK, BLOCK), lambda b, n: (0, 0, 0)),
                  pl.BlockSpec((N_KV_HEADS, BLOCK, GROUP_W), blk),
                  pl.BlockSpec((N_KV_HEADS, BLOCK, 2 * HEAD_DIM), blk),
                  pl.BlockSpec((N_KV_HEADS, BLOCK, 2 * HEAD_DIM), prev),
                  pl.BlockSpec((N_KV_HEADS, BLOCK, GROUP_W), blk)],
        out_specs=pl.BlockSpec((N_KV_HEADS, BLOCK, GROUP_W), blk),
        compiler_params=_params(2),
        name="band_attention",
    )(sinks, _alibi_table(), q, kv, kv, gate)


def _attn_out_kernel(x_ref, w_ref, r_ref, o_ref, wb_ref):
    _cast_weight_once(w_ref, wb_ref)
    acc = r_ref[...]
    for h in range(N_KV_HEADS):
        acc = acc + jnp.dot(x_ref[h], wb_ref[h], preferred_element_type=F32)
    o_ref[...] = acc


def _attn_out_proj(og, w, resid, tm=1024, tn=512):
    _, m, _ = og.shape
    n = w.shape[2]
    return pl.pallas_call(
        _attn_out_kernel,
        out_shape=jax.ShapeDtypeStruct((m, n), F32),
        grid=(n // tn, m // tm),
        in_specs=[pl.BlockSpec((N_KV_HEADS, tm, GROUP_W), lambda j, i: (0, i, 0)),
                  pl.BlockSpec((N_KV_HEADS, GROUP_W, tn), lambda j, i: (0, 0, j)),
                  pl.BlockSpec((tm, tn), lambda j, i: (i, j))],
        out_specs=pl.BlockSpec((tm, tn), lambda j, i: (i, j)),
        scratch_shapes=[pltpu.VMEM((N_KV_HEADS, GROUP_W, tn), BF16)],
        compiler_params=_params(2),
        name="attn_out_proj",
    )(og, w, resid)


def _ssm_in_kernel(x_ref, w_ref, o_ref, wb_ref):
    jj, i = pl.program_id(0), pl.program_id(1)
    chunk = w_ref.shape[0]
    fill = jj % 2
    rows = pl.ds(pl.multiple_of(i * chunk, chunk), chunk)

    @pl.when(jj == 0)
    def _():
        wb_ref[fill, rows, :] = w_ref[...].astype(BF16)

    @pl.when(jj > 0)
    def _():
        wb_ref[fill, rows, :] = w_ref[...].astype(BF16)
        o_ref[...] = jnp.dot(x_ref[...], wb_ref[1 - fill], preferred_element_type=F32).astype(o_ref.dtype)


def _ssm_in_proj(x, w, tm=1024, tn=1024):
    m, k = x.shape
    n = w.shape[1]
    ni, nj = m // tm, n // tn
    chunk = k // ni
    row_tile = lambda jj, i: jnp.where(jj == 0, 0, i)
    return pl.pallas_call(
        _ssm_in_kernel,
        out_shape=jax.ShapeDtypeStruct((m, n), BF16),
        grid=(nj + 1, ni),
        in_specs=[pl.BlockSpec((tm, k), lambda jj, i: (row_tile(jj, i), 0)),
                  pl.BlockSpec((chunk, tn), lambda jj, i: (i, jnp.minimum(jj, nj - 1)))],
        out_specs=pl.BlockSpec((tm, tn), lambda jj, i: (row_tile(jj, i), jnp.maximum(jj - 1, 0))),
        scratch_shapes=[pltpu.VMEM((2, k, tn), BF16)],
        compiler_params=_params(2),
        name="ssm_in_proj",
    )(x, w)


def _glu_kernel(x_ref, w_ref, y_ref, gate_ref, o_ref):
    gt = gate_ref[...].astype(F32)
    gated = y_ref[...].astype(F32) * (gt * _sigmoid(gt))
    z = jnp.dot(x_ref[...], w_ref[...], preferred_element_type=F32)
    o_ref[...] = (gated * _sigmoid(z)).astype(o_ref.dtype)


def _glu(y, w, ug, tm=1024, tn=512):
    m, k = y.shape
    n = w.shape[1]
    gate_off = n // tn
    return pl.pallas_call(
        _glu_kernel,
        out_shape=jax.ShapeDtypeStruct((m, n), BF16),
        grid=(m // tm, n // tn),
        in_specs=[pl.BlockSpec((tm, k), lambda i, j: (i, 0)),
                  pl.BlockSpec((k, tn), lambda i, j: (0, j)),
                  pl.BlockSpec((tm, tn), lambda i, j: (i, j)),
                  pl.BlockSpec((tm, tn), lambda i, j: (i, j + gate_off))],
        out_specs=pl.BlockSpec((tm, tn), lambda i, j: (i, j)),
        compiler_params=_params(2),
        name="ssm_glu",
    )(y, w, y, ug)


def _ssm_out_kernel(x_ref, w_ref, r_ref, o_ref, cols_ref):
    nj = x_ref.shape[1]
    x = x_ref[...].reshape(CHUNK * nj, x_ref.shape[2])
    acc = jnp.dot(x, w_ref[...], preferred_element_type=F32)
    for c in range(acc.shape[1] // LANES):
        for s in range(CHUNK):
            cols_ref[c, pl.ds(s, nj, stride=CHUNK), :] = acc[
                s * nj:(s + 1) * nj, c * LANES:(c + 1) * LANES]
        o_ref[:, c * LANES:(c + 1) * LANES] = cols_ref[c] + r_ref[:, c * LANES:(c + 1) * LANES]


def _ssm_out_proj(x, w, resid, tm=1024, tn=512):
    m, k = x.shape
    n = w.shape[1]
    nj = tm // CHUNK
    return pl.pallas_call(
        _ssm_out_kernel,
        out_shape=jax.ShapeDtypeStruct((m, n), F32),
        grid=(m // tm, n // tn),
        in_specs=[pl.BlockSpec((CHUNK, nj, k), lambda i, j: (0, i, 0)),
                  pl.BlockSpec((k, tn), lambda i, j: (0, j)),
                  pl.BlockSpec((tm, tn), lambda i, j: (i, j))],
        out_specs=pl.BlockSpec((tm, tn), lambda i, j: (i, j)),
        scratch_shapes=[pltpu.VMEM((tn // LANES, tm, LANES), F32)],
        compiler_params=_params(2),
        name="ssm_out_proj",
    )(x.reshape(CHUNK, m // CHUNK, k), w, resid)


N_SCAN_STEPS = 8
SCAN_BLOCK = 8
N_LOCAL_STEPS = 3


def _ssm_prep_kernel(lr_ref, li_ref, ldt_ref, btr_ref, bti_ref, cr_ref, ci_ref,
                     m_ref, be_ref, cet_ref, apr_ref, api_ref, bpr_ref, bpi_ref, *, gb):
    tau = lax.broadcasted_iota(jnp.int32, (24, 2 * STATE_DIM), 0).astype(F32)
    quarter = jnp.where(lax.broadcasted_iota(jnp.int32, (24, 2 * STATE_DIM), 1) < STATE_DIM, 0.0, 0.5 * jnp.pi)
    lane = lax.broadcasted_iota(jnp.int32, (GROUP_SIZE, CHUNK_W), 1)

    def one_group(g, carry):
        lr = lr_ref[g]
        li = li_ref[g]
        dt = jnp.exp(ldt_ref[g])
        lr2 = jnp.concatenate([lr, lr], axis=1)
        li2 = jnp.concatenate([li, li], axis=1)
        dt2 = jnp.concatenate([dt, dt], axis=1)
        powers = jnp.exp(tau * (lr2 * dt2)) * jnp.cos(tau * (li2 * dt2) - quarter)
        pr = powers[:, :STATE_DIM]
        pi = powers[:, STATE_DIM:]
        ar, ai = pr[1:2], pi[1:2]
        den = lr * lr + li * li
        xr = ar - 1.0
        wr = (xr * lr + ai * li) / den
        wi = (ai * lr - xr * li) / den
        btr, bti = btr_ref[g], bti_ref[g]
        br = btr * wr - bti * wi
        bi = btr * wi + bti * wr
        cr, ci = cr_ref[g], ci_ref[g]

        ca_r, ca_i = [], []
        for t in range(CHUNK + 1):
            ca_r.append(cr * pr[t:t + 1] - ci * pi[t:t + 1])
            ca_i.append(cr * pi[t:t + 1] + ci * pr[t:t + 1])

        rt = jnp.concatenate([jnp.concatenate(ca_r[:CHUNK], axis=0),
                              jnp.concatenate(ca_i[:CHUNK], axis=0)], axis=1)
        bcat = jnp.concatenate([br, -bi], axis=1)
        kp = lax.dot_general(bcat.astype(BF16), rt.astype(BF16), NT,
                             preferred_element_type=F32)
        for s in range(CHUNK):
            blk = kp if s == 0 else pltpu.roll(kp, GROUP_SIZE * s, 1)
            blk = jnp.where(lane >= GROUP_SIZE * s, blk, 0.0)
            m_ref[g, s * GROUP_SIZE:(s + 1) * GROUP_SIZE, :] = blk.astype(m_ref.dtype)
            q = CHUNK - 1 - s
            be_re = br * pr[q:q + 1] - bi * pi[q:q + 1]
            be_im = br * pi[q:q + 1] + bi * pr[q:q + 1]
            be_ref[g, s * GROUP_SIZE:(s + 1) * GROUP_SIZE, :] = jnp.concatenate(
                [be_re, be_im], axis=1).astype(be_ref.dtype)
            cet_ref[g, s * GROUP_SIZE:(s + 1) * GROUP_SIZE, :] = jnp.concatenate(
                [ca_r[s + 1], -ca_i[s + 1]], axis=1).astype(cet_ref.dtype)

        sq_r, sq_i = [pr[CHUNK:CHUNK + 1]], [pi[CHUNK:CHUNK + 1]]
        for _ in range(N_SCAN_STEPS - 1):
            xr, xi = sq_r[-1], sq_i[-1]
            sq_r.append(xr * xr - xi * xi)
            sq_i.append(2.0 * (xr * xi))
        mul_r, mul_i = [jnp.ones_like(sq_r[0]), sq_r[0]], [jnp.zeros_like(sq_r[0]), sq_i[0]]
        for _ in range(SCAN_BLOCK - 2):
            xr, xi = mul_r[-1], mul_i[-1]
            mul_r.append(xr * sq_r[0] - xi * sq_i[0])
            mul_i.append(xr * sq_i[0] + xi * sq_r[0])
        sr, si = jnp.concatenate(sq_r, axis=0), jnp.concatenate(sq_i, axis=0)
        rr, ri = jnp.concatenate(mul_r, axis=0), jnp.concatenate(mul_i, axis=0)
        apr_ref[g] = jnp.concatenate([sr, sr], axis=1)
        api_ref[g] = jnp.concatenate([-si, si], axis=1)
        bpr_ref[g] = jnp.concatenate([rr, rr], axis=1)
        bpi_ref[g] = jnp.concatenate([-ri, ri], axis=1)
        return carry

    lax.fori_loop(0, gb, one_group, 0, unroll=4)


def _ssm_prep(lam_re, lam_im, log_dt, b_re, b_im, c_re, c_im, gb=8):
    g = N_GROUPS
    row = lambda a: a.reshape(g, 1, STATE_DIM)
    ldt = jnp.broadcast_to(log_dt.reshape(g, 1, 1), (g, 1, STATE_DIM))
    btr = jnp.swapaxes(b_re, 1, 2)
    bti = jnp.swapaxes(b_im, 1, 2)
    vec = pl.BlockSpec((gb, 1, STATE_DIM), lambda i: (i, 0, 0))
    mat = pl.BlockSpec((gb, GROUP_SIZE, STATE_DIM), lambda i: (i, 0, 0))
    return pl.pallas_call(
        functools.partial(_ssm_prep_kernel, gb=gb),
        out_shape=[jax.ShapeDtypeStruct((g, CHUNK_W, CHUNK_W), BF16),
                   jax.ShapeDtypeStruct((g, CHUNK_W, 2 * STATE_DIM), BF16),
                   jax.ShapeDtypeStruct((g, CHUNK_W, 2 * STATE_DIM), BF16),
                   jax.ShapeDtypeStruct((g, N_SCAN_STEPS, 2 * STATE_DIM), F32),
                   jax.ShapeDtypeStruct((g, N_SCAN_STEPS, 2 * STATE_DIM), F32),
                   jax.ShapeDtypeStruct((g, SCAN_BLOCK, 2 * STATE_DIM), F32),
                   jax.ShapeDtypeStruct((g, SCAN_BLOCK, 2 * STATE_DIM), F32)],
        grid=(g // gb,),
        in_specs=[vec, vec, vec, mat, mat, mat, mat],
        out_specs=[pl.BlockSpec((gb, CHUNK_W, CHUNK_W), lambda i: (i, 0, 0)),
                   pl.BlockSpec((gb, CHUNK_W, 2 * STATE_DIM), lambda i: (i, 0, 0)),
                   pl.BlockSpec((gb, CHUNK_W, 2 * STATE_DIM), lambda i: (i, 0, 0)),
                   pl.BlockSpec((gb, N_SCAN_STEPS, 2 * STATE_DIM), lambda i: (i, 0, 0)),
                   pl.BlockSpec((gb, N_SCAN_STEPS, 2 * STATE_DIM), lambda i: (i, 0, 0)),
                   pl.BlockSpec((gb, SCAN_BLOCK, 2 * STATE_DIM), lambda i: (i, 0, 0)),
                   pl.BlockSpec((gb, SCAN_BLOCK, 2 * STATE_DIM), lambda i: (i, 0, 0))],
        compiler_params=_params(1),
        name="ssm_prep",
    )(row(lam_re), row(lam_im), ldt, btr, bti, c_re, c_im)


XPOSE_ROWS = 256


def _piece_transpose(vs):
    lane = lax.broadcasted_iota(jnp.int32, vs[0].shape, 1)
    vs = list(vs)
    for d in (4, 2, 1):
        w = GROUP_SIZE * d
        keep = (lane & w) == 0
        for k in range(GROUPS_PER_BLOCK):
            if k & d:
                continue
            a, b = vs[k], vs[k + d]
            vs[k] = jnp.where(keep, a, _lane_roll(b, w))
            vs[k + d] = jnp.where(keep, _lane_roll(a, LANES - w), b)
    return vs


def _cmul(x, mr, mi):
    return x * mr + pltpu.roll(x, STATE_DIM, 1) * mi


def _ssm_kernel(x_ref, m_ref, be_ref, cet_ref, apr_ref, api_ref, bpr_ref, bpi_ref, d_ref, o_ref,
                uf_ref, yf_ref, xs_ref, cs_ref, *, batch, n_chunks):
    rows = batch * n_chunks
    n_blocks = n_chunks // SCAN_BLOCK

    def gather(i, carry):
        r0 = pl.multiple_of(i * XPOSE_ROWS, XPOSE_ROWS)
        for half in range(2):
            vs = [x_ref[GROUPS_PER_BLOCK * half + k, pl.ds(r0, XPOSE_ROWS), :]
                  for k in range(GROUPS_PER_BLOCK)]
            ts = _piece_transpose(vs)
            for g in range(GROUPS_PER_BLOCK):
                uf_ref[g, pl.ds(r0, XPOSE_ROWS), half * LANES:(half + 1) * LANES] = ts[g]
        return carry

    lax.fori_loop(0, rows // XPOSE_ROWS, gather, 0)

    n_all = rows // SCAN_BLOCK
    row = lax.broadcasted_iota(jnp.int32, (rows, 2 * STATE_DIM), 0)
    in_block = row & (SCAN_BLOCK - 1)
    brow = lax.broadcasted_iota(jnp.int32, (n_all, 2 * STATE_DIM), 0)
    in_seq = brow & (n_blocks - 1)

    def shift_rows(x, d, pos):
        return jnp.where(pos >= d, pltpu.roll(x, d, 0), 0.0)

    def one_group(g, carry):
        u = uf_ref[g]
        y = jnp.dot(u, m_ref[g], preferred_element_type=F32)
        x = jnp.dot(u, be_ref[g], preferred_element_type=F32)
        apr, api = apr_ref[g], api_ref[g]
        for k in range(N_LOCAL_STEPS):
            x = x + _cmul(shift_rows(x, 1 << k, in_block), apr[k:k + 1], api[k:k + 1])
        xs_ref[...] = x
        c = xs_ref[pl.ds(SCAN_BLOCK - 1, n_all, stride=SCAN_BLOCK), :]
        for k in range(N_LOCAL_STEPS, N_SCAN_STEPS):
            c = c + _cmul(shift_rows(c, 1 << (k - N_LOCAL_STEPS), in_seq), apr[k:k + 1], api[k:k + 1])
        cs_ref[...] = shift_rows(c, 1, in_seq)
        bpr, bpi = bpr_ref[g], bpi_ref[g]
        carried = [_cmul(jnp.broadcast_to(cs_ref[mblk:mblk + 1, :], (SCAN_BLOCK, 2 * STATE_DIM)), bpr, bpi)
                   for mblk in range(n_all)]
        e = (shift_rows(x, 1, in_block) + jnp.concatenate(carried, axis=0)).astype(BF16)
        y = y + lax.dot_general(e, cet_ref[g], NT, preferred_element_type=F32)
        y = y + d_ref[g] * u.astype(F32)
        yf_ref[g] = jax.nn.gelu(y).astype(yf_ref.dtype)
        return carry

    lax.fori_loop(0, GROUPS_PER_BLOCK, one_group, 0, unroll=4)

    def scatter(i, carry):
        r0 = pl.multiple_of(i * XPOSE_ROWS, XPOSE_ROWS)
        for half in range(2):
            vs = [yf_ref[g, pl.ds(r0, XPOSE_ROWS), half * LANES:(half + 1) * LANES]
                  for g in range(GROUPS_PER_BLOCK)]
            ts = _piece_transpose(vs)
            for k in range(GROUPS_PER_BLOCK):
                o_ref[GROUPS_PER_BLOCK * half + k, pl.ds(r0, XPOSE_ROWS), :] = ts[k]
        return carry

    lax.fori_loop(0, rows // XPOSE_ROWS, scatter, 0)


def _ssm(ug, m, be, cet, apr, api, bpr, bpi, dvec, batch):
    _, rows, _ = ug.shape
    n_chunks = rows // batch
    gb = GROUPS_PER_BLOCK
    col_blk = pl.BlockSpec((CHUNK, rows, LANES), lambda i: (0, 0, i))
    grp = lambda last2: pl.BlockSpec((gb,) + last2, lambda i: (i, 0, 0))
    return pl.pallas_call(
        functools.partial(_ssm_kernel, batch=batch, n_chunks=n_chunks),
        out_shape=jax.ShapeDtypeStruct((CHUNK, rows, SSM_WIDTH), BF16),
        grid=(N_GROUPS // gb,),
        in_specs=[col_blk, grp((CHUNK_W, CHUNK_W)), grp((CHUNK_W, 2 * STATE_DIM)),
                  grp((CHUNK_W, 2 * STATE_DIM)), grp((N_SCAN_STEPS, 2 * STATE_DIM)),
                  grp((N_SCAN_STEPS, 2 * STATE_DIM)), grp((SCAN_BLOCK, 2 * STATE_DIM)),
                  grp((SCAN_BLOCK, 2 * STATE_DIM)), grp((1, CHUNK_W))],
        out_specs=col_blk,
        scratch_shapes=[pltpu.VMEM((gb, rows, CHUNK_W), BF16),
                        pltpu.VMEM((gb, rows, CHUNK_W), BF16),
                        pltpu.VMEM((rows, 2 * STATE_DIM), F32),
                        pltpu.VMEM((rows // SCAN_BLOCK, 2 * STATE_DIM), F32)],
        compiler_params=_params(1),
        name="ssm_chunked",
    )(ug, m, be, cet, apr, api, bpr, bpi, dvec)


def kernel(x, norm_g, attn_w_in, attn_q_norm_g, attn_k_norm_g, attn_sinks, attn_w_out, ssm_w_in,
           ssm_log_dt, ssm_lam_re, ssm_lam_im, ssm_b_re, ssm_b_im, ssm_c_re, ssm_c_im, ssm_d,
           ssm_w_glu, ssm_w_out):
    batch, seq, d = x.shape
    m = batch * seq
    assert d == D_MODEL and seq % BLOCK == 0 and seq // CHUNK == 1 << N_SCAN_STEPS
    x2 = x.reshape(m, d)

    w_in = attn_w_in[0]
    q_gain = jnp.tile(attn_q_norm_g[0].astype(F32) * HEAD_DIM ** -0.5, Q_PER_KV).reshape(1, GROUP_W)
    k_gain = jnp.tile(attn_k_norm_g[0].astype(F32), N_KV_HEADS).reshape(1, KV_WIDTH)

    hn0 = _rmsnorm(x2, norm_g[0])
    q = _q_proj(hn0, w_in, q_gain)
    kv = _kv_proj(hn0, w_in, k_gain)
    gate = _gate_proj(hn0, w_in)
    og = _attention(q, kv, gate, attn_sinks[0].astype(F32), batch, seq)
    h1 = _attn_out_proj(og, attn_w_out[0].reshape(N_KV_HEADS, GROUP_W, D_MODEL), x2)

    nj = m // CHUNK
    hn1 = _rmsnorm_chunk_major(h1, norm_g[1])
    ug = _ssm_in_proj(hn1, ssm_w_in[0])
    mt, be, cet, apr, api, bpr, bpi = _ssm_prep(ssm_lam_re[0], ssm_lam_im[0], ssm_log_dt[0],
                                      ssm_b_re[0], ssm_b_im[0], ssm_c_re[0], ssm_c_im[0])
    dvec = jnp.tile(ssm_d[0].astype(F32).reshape(N_GROUPS, 1, GROUP_SIZE), (1, 1, CHUNK))
    y = _ssm(ug.reshape(CHUNK, nj, 2 * SSM_WIDTH), mt, be, cet, apr, api, bpr, bpi, dvec, batch)
    y = y.reshape(m, SSM_WIDTH)
    p = _glu(y, ssm_w_glu[0].astype(BF16), ug)
    out = _ssm_out_proj(p, ssm_w_out[0].astype(BF16), h1)
    return out.reshape(batch, seq, d)
```

```python
import functools

import jax
import jax.numpy as jnp
from jax import lax
from jax.experimental import pallas as pl
from jax.experimental.pallas import tpu as pltpu

F32 = jnp.float32
BF16 = jnp.bfloat16

D_MODEL = 4096
HEAD_DIM = 64
N_Q_HEADS = 64
N_KV_HEADS = 8
Q_PER_KV = 8
N_PAIRS = Q_PER_KV // 2
KV_WIDTH = N_KV_HEADS * HEAD_DIM
GROUP_W = Q_PER_KV * HEAD_DIM
BLOCK = 128
STACK = Q_PER_KV * BLOCK
SSM_WIDTH = 8192
GROUP_SIZE = 16
N_GROUPS = 512
STATE_DIM = 64
CHUNK = 16
CHUNK_W = CHUNK * GROUP_SIZE
RMS_EPS = 1e-6
NEG_INF = -1e30

LANES = 128
VMEM_LIMIT = 60 * 1024 * 1024
MXU_W = 256
GROUPS_PER_BLOCK = LANES // GROUP_SIZE

NT = (((1,), (1,)), ((), ()))


def _lane_roll(x, shift):
    return jnp.concatenate([x[:, LANES - shift:], x[:, :LANES - shift]], axis=1)


def _cast_weight_once(w_ref, wb_ref):
    @pl.when(pl.program_id(1) == 0)
    def _():
        wb_ref[...] = w_ref[...].astype(BF16)


def _sigmoid(x):
    return 0.5 * jnp.tanh(0.5 * x) + 0.5


def _params(n_grid_dims):
    return pltpu.CompilerParams(
        dimension_semantics=("arbitrary",) * n_grid_dims, vmem_limit_bytes=VMEM_LIMIT)


def _rms(x, g):
    var = jnp.mean(x * x, axis=-1, keepdims=True)
    return x * lax.rsqrt(var + RMS_EPS) * g


def _rmsnorm_kernel(x_ref, g_ref, o_ref):
    o_ref[...] = _rms(x_ref[...], g_ref[...]).astype(o_ref.dtype)


def _rmsnorm(x, g, tm=256):
    m, d = x.shape
    return pl.pallas_call(
        _rmsnorm_kernel,
        out_shape=jax.ShapeDtypeStruct((m, d), BF16),
        grid=(m // tm,),
        in_specs=[pl.BlockSpec((tm, d), lambda i: (i, 0)),
                  pl.BlockSpec((1, d), lambda i: (0, 0))],
        out_specs=pl.BlockSpec((tm, d), lambda i: (i, 0)),
        compiler_params=_params(1),
        name="rmsnorm",
    )(x, g.reshape(1, d))


def _rmsnorm_chunk_major_kernel(x_ref, g_ref, o_ref, cols_ref):
    hn = _rms(x_ref[...], g_ref[...])
    n_cols = hn.shape[1] // LANES
    nj = hn.shape[0] // CHUNK
    for c in range(n_cols):
        cols_ref[c] = hn[:, c * LANES:(c + 1) * LANES]
    for s in range(CHUNK):
        for c in range(n_cols):
            o_ref[s, :, c * LANES:(c + 1) * LANES] = cols_ref[
                c, pl.ds(s, nj, stride=CHUNK), :].astype(o_ref.dtype)


def _rmsnorm_chunk_major(x, g, tm=256):
    m, d = x.shape
    nj = tm // CHUNK
    out = pl.pallas_call(
        _rmsnorm_chunk_major_kernel,
        out_shape=jax.ShapeDtypeStruct((CHUNK, m // CHUNK, d), BF16),
        grid=(m // tm,),
        in_specs=[pl.BlockSpec((tm, d), lambda i: (i, 0)),
                  pl.BlockSpec((1, d), lambda i: (0, 0))],
        out_specs=pl.BlockSpec((CHUNK, nj, d), lambda i: (0, i, 0)),
        scratch_shapes=[pltpu.VMEM((d // LANES, tm, LANES), F32)],
        compiler_params=_params(1),
        name="rmsnorm_chunk_major",
    )(x, g.reshape(1, d))
    return out.reshape(m, d)


def _head_rmsnorm(acc, ones_bd, gain):
    outs = []
    for c in range(acc.shape[1] // MXU_W):
        a = acc[:, c * MXU_W:(c + 1) * MXU_W]
        sq = a * a
        hi = sq.astype(BF16)
        lo = (sq - hi.astype(F32)).astype(BF16)
        ss = (jnp.dot(hi, ones_bd, preferred_element_type=F32)
              + jnp.dot(lo, ones_bd, preferred_element_type=F32))
        inv = lax.rsqrt(ss * (1.0 / HEAD_DIM) + RMS_EPS)
        outs.append(a * inv * gain[:, c * MXU_W:(c + 1) * MXU_W])
    return outs[0] if len(outs) == 1 else jnp.concatenate(outs, axis=1)


def _q_proj_kernel(x_ref, w_ref, e_ref, g_ref, o_ref, wb_ref):
    _cast_weight_once(w_ref, wb_ref)
    acc = jnp.dot(x_ref[...], wb_ref[...], preferred_element_type=F32)
    o_ref[...] = _head_rmsnorm(acc, e_ref[...], g_ref[...]).astype(o_ref.dtype)


def _kv_proj_kernel(x_ref, w_ref, e_ref, g_ref, o_ref, wb_ref):
    _cast_weight_once(w_ref, wb_ref)
    acc = jnp.dot(x_ref[...], wb_ref[...], preferred_element_type=F32)
    kn = _head_rmsnorm(acc[:, :KV_WIDTH], e_ref[...], g_ref[...])
    for h in range(N_KV_HEADS):
        o_ref[h, :, 0:HEAD_DIM] = kn[:, h * HEAD_DIM:(h + 1) * HEAD_DIM].astype(o_ref.dtype)
        o_ref[h, :, HEAD_DIM:2 * HEAD_DIM] = acc[
            :, KV_WIDTH + h * HEAD_DIM:KV_WIDTH + (h + 1) * HEAD_DIM].astype(o_ref.dtype)


def _cast_proj_kernel(x_ref, w_ref, o_ref, wb_ref):
    _cast_weight_once(w_ref, wb_ref)
    o_ref[...] = jnp.dot(x_ref[...], wb_ref[...], preferred_element_type=F32).astype(o_ref.dtype)


def _ones_block_diag():
    r = jnp.arange(MXU_W) // HEAD_DIM
    return (r[:, None] == r[None, :]).astype(BF16)


Q_COL0 = 0
KV_COL0 = D_MODEL
GATE_COL0 = D_MODEL + 2 * KV_WIDTH


def _q_proj(hn, w_in, gain, tm=1024):
    m, k = hn.shape
    return pl.pallas_call(
        _q_proj_kernel,
        out_shape=jax.ShapeDtypeStruct((N_KV_HEADS, m, GROUP_W), BF16),
        grid=(N_KV_HEADS, m // tm),
        in_specs=[pl.BlockSpec((tm, k), lambda j, i: (i, 0)),
                  pl.BlockSpec((k, GROUP_W), lambda j, i: (0, Q_COL0 // GROUP_W + j)),
                  pl.BlockSpec((MXU_W, MXU_W), lambda j, i: (0, 0)),
                  pl.BlockSpec((1, GROUP_W), lambda j, i: (0, 0))],
        out_specs=pl.BlockSpec((None, tm, GROUP_W), lambda j, i: (j, i, 0)),
        scratch_shapes=[pltpu.VMEM((k, GROUP_W), BF16)],
        compiler_params=_params(2),
        name="attn_q_proj",
    )(hn, w_in, _ones_block_diag(), gain)


def _kv_proj(hn, w_in, gain, tm=512):
    m, k = hn.shape
    n = 2 * KV_WIDTH
    return pl.pallas_call(
        _kv_proj_kernel,
        out_shape=jax.ShapeDtypeStruct((N_KV_HEADS, m, 2 * HEAD_DIM), BF16),
        grid=(1, m // tm),
        in_specs=[pl.BlockSpec((tm, k), lambda j, i: (i, 0)),
                  pl.BlockSpec((k, n), lambda j, i: (0, KV_COL0 // n)),
                  pl.BlockSpec((MXU_W, MXU_W), lambda j, i: (0, 0)),
                  pl.BlockSpec((1, KV_WIDTH), lambda j, i: (0, 0))],
        out_specs=pl.BlockSpec((N_KV_HEADS, tm, 2 * HEAD_DIM), lambda j, i: (0, i, 0)),
        scratch_shapes=[pltpu.VMEM((k, n), BF16)],
        compiler_params=_params(2),
        name="attn_kv_proj",
    )(hn, w_in, _ones_block_diag(), gain)


def _gate_proj(hn, w_in, tm=1024):
    m, k = hn.shape
    return pl.pallas_call(
        _cast_proj_kernel,
        out_shape=jax.ShapeDtypeStruct((N_KV_HEADS, m, GROUP_W), BF16),
        grid=(N_KV_HEADS, m // tm),
        in_specs=[pl.BlockSpec((tm, k), lambda j, i: (i, 0)),
                  pl.BlockSpec((k, GROUP_W), lambda j, i: (0, GATE_COL0 // GROUP_W + j))],
        out_specs=pl.BlockSpec((None, tm, GROUP_W), lambda j, i: (j, i, 0)),
        scratch_shapes=[pltpu.VMEM((k, GROUP_W), BF16)],
        compiler_params=_params(2),
        name="attn_gate_proj",
    )(hn, w_in)


def _attn_kernel(sink_ref, alibi_ref, q_ref, kvc_ref, kvp_ref, gate_ref, o_ref):
    n = pl.program_id(1)
    row = lax.broadcasted_iota(jnp.int32, (BLOCK, BLOCK), 0)
    col = lax.broadcasted_iota(jnp.int32, (BLOCK, BLOCK), 1)
    lower = col <= row
    valid = jnp.logical_or(lower, n > 0)
    left = col < HEAD_DIM
    ones = jnp.ones((2 * BLOCK, LANES), BF16)

    def split_kv(kv):
        swapped = _lane_roll(kv, HEAD_DIM)
        zero = jnp.zeros_like(kv)
        return jnp.where(left, kv, zero), jnp.where(left, zero, swapped), jnp.where(left, swapped, kv)

    def kv_head(kvh, carry):
        kc_even, kc_odd, vvc = split_kv(kvc_ref[kvh])
        kp_even, kp_odd, vvp = split_kv(kvp_ref[kvh])
        qp = jnp.concatenate(
            [q_ref[kvh, :, a * LANES:(a + 1) * LANES] for a in range(N_PAIRS)], axis=0)
        keys = jnp.concatenate([kc_even, kc_odd, kp_even, kp_odd], axis=0)
        s_all = lax.dot_general(qp, keys, NT, preferred_element_type=F32)
        v_ones = jnp.concatenate([jnp.concatenate([vvc, vvp], axis=0), ones], axis=1)
        ps, sink_terms = [], []
        for par in range(2):
            for a in range(N_PAIRS):
                rows = slice(a * BLOCK, (a + 1) * BLOCK)
                sc = s_all[rows, par * LANES:(par + 1) * LANES]
                sp = s_all[rows, (2 + par) * LANES:(3 + par) * LANES]
                blk = par * N_PAIRS + a
                s = jnp.where(lower, sc, sp) + alibi_ref[kvh, blk * BLOCK:(blk + 1) * BLOCK, :]
                s = jnp.where(valid, s, NEG_INF)
                sink = sink_ref[kvh * Q_PER_KV + 2 * a + par]
                m = jnp.maximum(jnp.max(s, axis=-1, keepdims=True), sink)
                p = jnp.exp(s - m)
                ps.append(jnp.concatenate([jnp.where(lower, p, 0.0), jnp.where(lower, 0.0, p)],
                                          axis=1).astype(BF16))
                sink_terms.append(jnp.exp(sink - m))
        nd = jnp.dot(jnp.concatenate(ps, axis=0), v_ones, preferred_element_type=F32)
        for a in range(N_PAIRS):
            o = []
            for par in range(2):
                blk = par * N_PAIRS + a
                r = nd[blk * BLOCK:(blk + 1) * BLOCK]
                o.append(r[:, :LANES] / (r[:, LANES:] + sink_terms[blk]))
            gt = gate_ref[kvh, :, a * LANES:(a + 1) * LANES].astype(F32)
            o_ref[kvh, :, a * LANES:(a + 1) * LANES] = (
                jnp.where(left, o[0], o[1]) * (gt * _sigmoid(gt))).astype(o_ref.dtype)
        return carry

    lax.fori_loop(0, N_KV_HEADS, kv_head, 0, unroll=True)


def _alibi_table():
    qi = jnp.arange(BLOCK)[:, None]
    kj = jnp.arange(BLOCK)[None, :]
    dist = jnp.where(kj <= qi, qi - kj, BLOCK + qi - kj).astype(F32)
    slopes = jnp.exp2(-8.0 * jnp.arange(1, N_Q_HEADS + 1, dtype=F32) / N_Q_HEADS)
    slopes = slopes.reshape(N_KV_HEADS, N_PAIRS, 2).transpose(0, 2, 1)
    return (-slopes[:, :, :, None, None] * dist).reshape(N_KV_HEADS, STACK, BLOCK)


def _attention(q, kv, gate, sinks, batch, seq):
    nb = seq // BLOCK
    m = batch * seq
    blk = lambda b, n: (0, b * nb + n, 0)
    prev = lambda b, n: (0, b * nb + jnp.maximum(n - 1, 0), 0)
    return pl.pallas_call(
        _attn_kernel,
        out_shape=jax.ShapeDtypeStruct((N_KV_HEADS, m, GROUP_W), BF16),
        grid=(batch, nb),
        in_specs=[pl.BlockSpec(memory_space=pltpu.SMEM),
                  pl.BlockSpec((N_KV_HEADS, STACK, BLOCK), lambda b, n: (0, 0, 0)),
                  pl.BlockSpec((N_KV_HEADS, BLOCK, GROUP_W), blk),
                  pl.BlockSpec((N_KV_HEADS, BLOCK, 2 * HEAD_DIM), blk),
                  pl.BlockSpec((N_KV_HEADS, BLOCK, 2 * HEAD_DIM), prev),
                  pl.BlockSpec((N_KV_HEADS, BLOCK, GROUP_W), blk)],
        out_specs=pl.BlockSpec((N_KV_HEADS, BLOCK, GROUP_W), blk),
        compiler_params=_params(2),
        name="band_attention",
    )(sinks, _alibi_table(), q, kv, kv, gate)


def _attn_out_kernel(x_ref, w_ref, r_ref, o_ref, wb_ref):
    _cast_weight_once(w_ref, wb_ref)
    acc = r_ref[...]
    for h in range(N_KV_HEADS):
        acc = acc + jnp.dot(x_ref[h], wb_ref[h], preferred_element_type=F32)
    o_ref[...] = acc


def _attn_out_proj(og, w, resid, tm=1024, tn=512):
    _, m, _ = og.shape
    n = w.shape[2]
    return pl.pallas_call(
        _attn_out_kernel,
        out_shape=jax.ShapeDtypeStruct((m, n), F32),
        grid=(n // tn, m // tm),
        in_specs=[pl.BlockSpec((N_KV_HEADS, tm, GROUP_W), lambda j, i: (0, i, 0)),
                  pl.BlockSpec((N_KV_HEADS, GROUP_W, tn), lambda j, i: (0, 0, j)),
                  pl.BlockSpec((tm, tn), lambda j, i: (i, j))],
        out_specs=pl.BlockSpec((tm, tn), lambda j, i: (i, j)),
        scratch_shapes=[pltpu.VMEM((N_KV_HEADS, GROUP_W, tn), BF16)],
        compiler_params=_params(2),
        name="attn_out_proj",
    )(og, w, resid)


def _ssm_in_kernel(x_ref, w_ref, o_ref, wb_ref):
    jj, i = pl.program_id(0), pl.program_id(1)
    chunk = w_ref.shape[0]
    fill = jj % 2
    rows = pl.ds(pl.multiple_of(i * chunk, chunk), chunk)

    @pl.when(jj == 0)
    def _():
        wb_ref[fill, rows, :] = w_ref[...].astype(BF16)

    @pl.when(jj > 0)
    def _():
        wb_ref[fill, rows, :] = w_ref[...].astype(BF16)
        o_ref[...] = jnp.dot(x_ref[...], wb_ref[1 - fill], preferred_element_type=F32).astype(o_ref.dtype)


def _ssm_in_proj(x, w, tm=1024, tn=1024):
    m, k = x.shape
    n = w.shape[1]
    ni, nj = m // tm, n // tn
    chunk = k // ni
    row_tile = lambda jj, i: jnp.where(jj == 0, 0, i)
    return pl.pallas_call(
        _ssm_in_kernel,
        out_shape=jax.ShapeDtypeStruct((m, n), BF16),
        grid=(nj + 1, ni),
        in_specs=[pl.BlockSpec((tm, k), lambda jj, i: (row_tile(jj, i), 0)),
                  pl.BlockSpec((chunk, tn), lambda jj, i: (i, jnp.minimum(jj, nj - 1)))],
        out_specs=pl.BlockSpec((tm, tn), lambda jj, i: (row_tile(jj, i), jnp.maximum(jj - 1, 0))),
        scratch_shapes=[pltpu.VMEM((2, k, tn), BF16)],
        compiler_params=_params(2),
        name="ssm_in_proj",
    )(x, w)


def _glu_kernel(x_ref, w_ref, y_ref, gate_ref, o_ref):
    gt = gate_ref[...].astype(F32)
    gated = y_ref[...].astype(F32) * (gt * _sigmoid(gt))
    z = jnp.dot(x_ref[...], w_ref[...], preferred_element_type=F32)
    o_ref[...] = (gated * _sigmoid(z)).astype(o_ref.dtype)


def _glu(y, w, ug, tm=1024, tn=512):
    m, k = y.shape
    n = w.shape[1]
    gate_off = n // tn
    return pl.pallas_call(
        _glu_kernel,
        out_shape=jax.ShapeDtypeStruct((m, n), BF16),
        grid=(m // tm, n // tn),
        in_specs=[pl.BlockSpec((tm, k), lambda i, j: (i, 0)),
                  pl.BlockSpec((k, tn), lambda i, j: (0, j)),
                  pl.BlockSpec((tm, tn), lambda i, j: (i, j)),
                  pl.BlockSpec((tm, tn), lambda i, j: (i, j + gate_off))],
        out_specs=pl.BlockSpec((tm, tn), lambda i, j: (i, j)),
        compiler_params=_params(2),
        name="ssm_glu",
    )(y, w, y, ug)


def _ssm_out_kernel(x_ref, w_ref, r_ref, o_ref, cols_ref):
    nj = x_ref.shape[1]
    x = x_ref[...].reshape(CHUNK * nj, x_ref.shape[2])
    acc = jnp.dot(x, w_ref[...], preferred_element_type=F32)
    for c in range(acc.shape[1] // LANES):
        for s in range(CHUNK):
            cols_ref[c, pl.ds(s, nj, stride=CHUNK), :] = acc[
                s * nj:(s + 1) * nj, c * LANES:(c + 1) * LANES]
        o_ref[:, c * LANES:(c + 1) * LANES] = cols_ref[c] + r_ref[:, c * LANES:(c + 1) * LANES]


def _ssm_out_proj(x, w, resid, tm=1024, tn=512):
    m, k = x.shape
    n = w.shape[1]
    nj = tm // CHUNK
    return pl.pallas_call(
        _ssm_out_kernel,
        out_shape=jax.ShapeDtypeStruct((m, n), F32),
        grid=(m // tm, n // tn),
        in_specs=[pl.BlockSpec((CHUNK, nj, k), lambda i, j: (0, i, 0)),
                  pl.BlockSpec((k, tn), lambda i, j: (0, j)),
                  pl.BlockSpec((tm, tn), lambda i, j: (i, j))],
        out_specs=pl.BlockSpec((tm, tn), lambda i, j: (i, j)),
        scratch_shapes=[pltpu.VMEM((tn // LANES, tm, LANES), F32)],
        compiler_params=_params(2),
        name="ssm_out_proj",
    )(x.reshape(CHUNK, m // CHUNK, k), w, resid)


N_SCAN_STEPS = 8
SCAN_BLOCK = 8
N_LOCAL_STEPS = 3


def _ssm_prep_kernel(lr_ref, li_ref, ldt_ref, btr_ref, bti_ref, cr_ref, ci_ref,
                     m_ref, be_ref, cet_ref, apr_ref, api_ref, bpr_ref, bpi_ref):
    tau = lax.broadcasted_iota(jnp.int32, (24, 2 * STATE_DIM), 0).astype(F32)
    quarter = jnp.where(lax.broadcasted_iota(jnp.int32, (24, 2 * STATE_DIM), 1) < STATE_DIM, 0.0, 0.5 * jnp.pi)
    lane = lax.broadcasted_iota(jnp.int32, (GROUP_SIZE, CHUNK_W), 1)

    def one_group(g, carry):
        lr = lr_ref[g]
        li = li_ref[g]
        dt = jnp.exp(ldt_ref[g])
        lr2 = jnp.concatenate([lr, lr], axis=1)
        li2 = jnp.concatenate([li, li], axis=1)
        dt2 = jnp.concatenate([dt, dt], axis=1)
        powers = jnp.exp(tau * (lr2 * dt2)) * jnp.cos(tau * (li2 * dt2) - quarter)
        pr = powers[:, :STATE_DIM]
        pi = powers[:, STATE_DIM:]
        ar, ai = pr[1:2], pi[1:2]
        den = lr * lr + li * li
        xr = ar - 1.0
        wr = (xr * lr + ai * li) / den
        wi = (ai * lr - xr * li) / den
        btr, bti = btr_ref[g], bti_ref[g]
        br = btr * wr - bti * wi
        bi = btr * wi + bti * wr
        cr, ci = cr_ref[g], ci_ref[g]

        ca_r, ca_i = [], []
        for t in range(CHUNK + 1):
            ca_r.append(cr * pr[t:t + 1] - ci * pi[t:t + 1])
            ca_i.append(cr * pi[t:t + 1] + ci * pr[t:t + 1])

        rt = jnp.concatenate([jnp.concatenate(ca_r[:CHUNK], axis=0),
                              jnp.concatenate(ca_i[:CHUNK], axis=0)], axis=1)
        bcat = jnp.concatenate([br, -bi], axis=1)
        kp = lax.dot_general(bcat.astype(BF16), rt.astype(BF16), NT,
                             preferred_element_type=F32)
        for s in range(CHUNK):
            blk = kp if s == 0 else pltpu.roll(kp, GROUP_SIZE * s, 1)
            blk = jnp.where(lane >= GROUP_SIZE * s, blk, 0.0)
            blk = jnp.concatenate([pltpu.roll(blk[:, :LANES], GROUP_SIZE * g, 1),
                                   pltpu.roll(blk[:, LANES:], GROUP_SIZE * g, 1)], axis=1)
            dst = pl.ds(pl.multiple_of(_step_slot(s, g) * GROUP_SIZE, GROUP_SIZE), GROUP_SIZE)
            m_ref[g, dst, :] = blk.astype(m_ref.dtype)
            e = CHUNK - 1 - s
            be_re = br * pr[e:e + 1] - bi * pi[e:e + 1]
            be_im = br * pi[e:e + 1] + bi * pr[e:e + 1]
            be_ref[g, dst, :] = jnp.concatenate(
                [be_re, be_im], axis=1).astype(be_ref.dtype)
            cet_ref[g, dst, :] = jnp.concatenate(
                [ca_r[s + 1], -ca_i[s + 1]], axis=1).astype(cet_ref.dtype)

        sq_r, sq_i = [pr[CHUNK:CHUNK + 1]], [pi[CHUNK:CHUNK + 1]]
        for _ in range(N_SCAN_STEPS - 1):
            xr, xi = sq_r[-1], sq_i[-1]
            sq_r.append(xr * xr - xi * xi)
            sq_i.append(2.0 * (xr * xi))
        mul_r, mul_i = [jnp.ones_like(sq_r[0]), sq_r[0]], [jnp.zeros_like(sq_r[0]), sq_i[0]]
        for _ in range(SCAN_BLOCK - 2):
            xr, xi = mul_r[-1], mul_i[-1]
            mul_r.append(xr * sq_r[0] - xi * sq_i[0])
            mul_i.append(xr * sq_i[0] + xi * sq_r[0])
        sr, si = jnp.concatenate(sq_r, axis=0), jnp.concatenate(sq_i, axis=0)
        rr, ri = jnp.concatenate(mul_r, axis=0), jnp.concatenate(mul_i, axis=0)
        apr_ref[g] = jnp.concatenate([sr, sr], axis=1)
        api_ref[g] = jnp.concatenate([-si, si], axis=1)
        bpr_ref[g] = jnp.concatenate([rr, rr], axis=1)
        bpi_ref[g] = jnp.concatenate([-ri, ri], axis=1)
        return carry

    lax.fori_loop(0, GROUPS_PER_BLOCK, one_group, 0, unroll=4)


def _ssm_prep(lam_re, lam_im, log_dt, b_re, b_im, c_re, c_im):
    g = N_GROUPS
    gb = GROUPS_PER_BLOCK
    row = lambda a: a.reshape(g, 1, STATE_DIM)
    ldt = jnp.broadcast_to(log_dt.reshape(g, 1, 1), (g, 1, STATE_DIM))
    btr = jnp.swapaxes(b_re, 1, 2)
    bti = jnp.swapaxes(b_im, 1, 2)
    vec = pl.BlockSpec((gb, 1, STATE_DIM), lambda i: (i, 0, 0))
    mat = pl.BlockSpec((gb, GROUP_SIZE, STATE_DIM), lambda i: (i, 0, 0))
    return pl.pallas_call(
        _ssm_prep_kernel,
        out_shape=[jax.ShapeDtypeStruct((g, CHUNK_W, CHUNK_W), BF16),
                   jax.ShapeDtypeStruct((g, CHUNK_W, 2 * STATE_DIM), BF16),
                   jax.ShapeDtypeStruct((g, CHUNK_W, 2 * STATE_DIM), BF16),
                   jax.ShapeDtypeStruct((g, N_SCAN_STEPS, 2 * STATE_DIM), F32),
                   jax.ShapeDtypeStruct((g, N_SCAN_STEPS, 2 * STATE_DIM), F32),
                   jax.ShapeDtypeStruct((g, SCAN_BLOCK, 2 * STATE_DIM), F32),
                   jax.ShapeDtypeStruct((g, SCAN_BLOCK, 2 * STATE_DIM), F32)],
        grid=(g // gb,),
        in_specs=[vec, vec, vec, mat, mat, mat, mat],
        out_specs=[pl.BlockSpec((gb, CHUNK_W, CHUNK_W), lambda i: (i, 0, 0)),
                   pl.BlockSpec((gb, CHUNK_W, 2 * STATE_DIM), lambda i: (i, 0, 0)),
                   pl.BlockSpec((gb, CHUNK_W, 2 * STATE_DIM), lambda i: (i, 0, 0)),
                   pl.BlockSpec((gb, N_SCAN_STEPS, 2 * STATE_DIM), lambda i: (i, 0, 0)),
                   pl.BlockSpec((gb, N_SCAN_STEPS, 2 * STATE_DIM), lambda i: (i, 0, 0)),
                   pl.BlockSpec((gb, SCAN_BLOCK, 2 * STATE_DIM), lambda i: (i, 0, 0)),
                   pl.BlockSpec((gb, SCAN_BLOCK, 2 * STATE_DIM), lambda i: (i, 0, 0))],
        compiler_params=_params(1),
        name="ssm_prep",
    )(row(lam_re), row(lam_im), ldt, btr, bti, c_re, c_im)


XPOSE_ROWS = 256


def _step_slot(step, group):
    half, k = divmod(step, GROUPS_PER_BLOCK)
    return half * GROUPS_PER_BLOCK + (k + group) % GROUPS_PER_BLOCK


def _slot_masks(shape):
    slot = lax.broadcasted_iota(jnp.int32, shape, 1) // GROUP_SIZE
    return [slot == p for p in range(GROUPS_PER_BLOCK)]


def _merge_slots(pieces, masks):
    out = pieces[0]
    for p in range(1, GROUPS_PER_BLOCK):
        out = jnp.where(masks[p], pieces[p], out)
    return out


def _steps_to_groups(xs):
    masks = _slot_masks(xs[0].shape)
    n = GROUPS_PER_BLOCK
    rot = [xs[k] if k == 0 else _lane_roll(xs[k], GROUP_SIZE * k) for k in range(n)]
    return [_merge_slots([rot[(p - g) % n] for p in range(n)], masks) for g in range(n)]


def _groups_to_steps(ys):
    masks = _slot_masks(ys[0].shape)
    n = GROUPS_PER_BLOCK
    out = []
    for k in range(n):
        z = _merge_slots([ys[(p - k) % n] for p in range(n)], masks)
        out.append(z if k == 0 else _lane_roll(z, LANES - GROUP_SIZE * k))
    return out


def _cmul(x, mr, mi):
    return x * mr + pltpu.roll(x, STATE_DIM, 1) * mi


def _ssm_kernel(x_ref, m_ref, be_ref, cet_ref, apr_ref, api_ref, bpr_ref, bpi_ref, d_ref, o_ref,
                uf_ref, yf_ref, xs_ref, cs_ref, *, batch, n_chunks):
    rows = batch * n_chunks
    n_blocks = n_chunks // SCAN_BLOCK

    def gather(i, carry):
        r0 = pl.multiple_of(i * XPOSE_ROWS, XPOSE_ROWS)
        for half in range(2):
            vs = [x_ref[GROUPS_PER_BLOCK * half + k, pl.ds(r0, XPOSE_ROWS), :]
                  for k in range(GROUPS_PER_BLOCK)]
            ts = _steps_to_groups(vs)
            for g in range(GROUPS_PER_BLOCK):
                uf_ref[g, pl.ds(r0, XPOSE_ROWS), half * LANES:(half + 1) * LANES] = ts[g]
        return carry

    lax.fori_loop(0, rows // XPOSE_ROWS, gather, 0)

    n_all = rows // SCAN_BLOCK
    row = lax.broadcasted_iota(jnp.int32, (rows, 2 * STATE_DIM), 0)
    in_block = row & (SCAN_BLOCK - 1)
    brow = lax.broadcasted_iota(jnp.int32, (n_all, 2 * STATE_DIM), 0)
    in_seq = brow & (n_blocks - 1)

    def shift_rows(x, d, pos):
        return jnp.where(pos >= d, pltpu.roll(x, d, 0), 0.0)

    def one_group(g, carry):
        u = uf_ref[g]
        y = jnp.dot(u, m_ref[g], preferred_element_type=F32)
        x = jnp.dot(u, be_ref[g], preferred_element_type=F32)
        apr, api = apr_ref[g], api_ref[g]
        for k in range(N_LOCAL_STEPS):
            x = x + _cmul(shift_rows(x, 1 << k, in_block), apr[k:k + 1], api[k:k + 1])
        xs_ref[...] = x
        c = xs_ref[pl.ds(SCAN_BLOCK - 1, n_all, stride=SCAN_BLOCK), :]
        for k in range(N_LOCAL_STEPS, N_SCAN_STEPS):
            c = c + _cmul(shift_rows(c, 1 << (k - N_LOCAL_STEPS), in_seq), apr[k:k + 1], api[k:k + 1])
        cs_ref[...] = shift_rows(c, 1, in_seq)
        bpr, bpi = bpr_ref[g], bpi_ref[g]
        carried = [_cmul(jnp.broadcast_to(cs_ref[mblk:mblk + 1, :], (SCAN_BLOCK, 2 * STATE_DIM)), bpr, bpi)
                   for mblk in range(n_all)]
        e = (shift_rows(x, 1, in_block) + jnp.concatenate(carried, axis=0)).astype(BF16)
        y = y + lax.dot_general(e, cet_ref[g], NT, preferred_element_type=F32)
        y = y + d_ref[g] * u.astype(F32)
        yf_ref[g] = jax.nn.gelu(y).astype(yf_ref.dtype)
        return carry

    lax.fori_loop(0, GROUPS_PER_BLOCK, one_group, 0, unroll=4)

    def scatter(i, carry):
        r0 = pl.multiple_of(i * XPOSE_ROWS, XPOSE_ROWS)
        for half in range(2):
            vs = [yf_ref[g, pl.ds(r0, XPOSE_ROWS), half * LANES:(half + 1) * LANES]
                  for g in range(GROUPS_PER_BLOCK)]
            ts = _groups_to_steps(vs)
            for k in range(GROUPS_PER_BLOCK):
                o_ref[GROUPS_PER_BLOCK * half + k, pl.ds(r0, XPOSE_ROWS), :] = ts[k]
        return carry

    lax.fori_loop(0, rows // XPOSE_ROWS, scatter, 0)


def _ssm(ug, m, be, cet, apr, api, bpr, bpi, dvec, batch):
    _, rows, _ = ug.shape
    n_chunks = rows // batch
    gb = GROUPS_PER_BLOCK
    col_blk = pl.BlockSpec((CHUNK, rows, LANES), lambda i: (0, 0, i))
    grp = lambda last2: pl.BlockSpec((gb,) + last2, lambda i: (i, 0, 0))
    return pl.pallas_call(
        functools.partial(_ssm_kernel, batch=batch, n_chunks=n_chunks),
        out_shape=jax.ShapeDtypeStruct((CHUNK, rows, SSM_WIDTH), BF16),
        grid=(N_GROUPS // gb,),
        in_specs=[col_blk, grp((CHUNK_W, CHUNK_W)), grp((CHUNK_W, 2 * STATE_DIM)),
                  grp((CHUNK_W, 2 * STATE_DIM)), grp((N_SCAN_STEPS, 2 * STATE_DIM)),
                  grp((N_SCAN_STEPS, 2 * STATE_DIM)), grp((SCAN_BLOCK, 2 * STATE_DIM)),
                  grp((SCAN_BLOCK, 2 * STATE_DIM)), grp((1, CHUNK_W))],
        out_specs=col_blk,
        scratch_shapes=[pltpu.VMEM((gb, rows, CHUNK_W), BF16),
                        pltpu.VMEM((gb, rows, CHUNK_W), BF16),
                        pltpu.VMEM((rows, 2 * STATE_DIM), F32),
                        pltpu.VMEM((rows // SCAN_BLOCK, 2 * STATE_DIM), F32)],
        compiler_params=_params(1),
        name="ssm_chunked",
    )(ug, m, be, cet, apr, api, bpr, bpi, dvec)


def kernel(x, norm_g, attn_w_in, attn_q_norm_g, attn_k_norm_g, attn_sinks, attn_w_out, ssm_w_in,
           ssm_log_dt, ssm_lam_re, ssm_lam_im, ssm_b_re, ssm_b_im, ssm_c_re, ssm_c_im, ssm_d,
           ssm_w_glu, ssm_w_out):
    batch, seq, d = x.shape
    m = batch * seq
    assert d == D_MODEL and seq % BLOCK == 0 and seq // CHUNK == 1 << N_SCAN_STEPS
    x2 = x.reshape(m, d)

    w_in = attn_w_in[0]
    q_gain = jnp.tile(attn_q_norm_g[0].astype(F32) * HEAD_DIM ** -0.5, Q_PER_KV).reshape(1, GROUP_W)
    k_gain = jnp.tile(attn_k_norm_g[0].astype(F32), N_KV_HEADS).reshape(1, KV_WIDTH)

    hn0 = _rmsnorm(x2, norm_g[0])
    q = _q_proj(hn0, w_in, q_gain)
    kv = _kv_proj(hn0, w_in, k_gain)
    gate = _gate_proj(hn0, w_in)
    og = _attention(q, kv, gate, attn_sinks[0].astype(F32), batch, seq)
    h1 = _attn_out_proj(og, attn_w_out[0].reshape(N_KV_HEADS, GROUP_W, D_MODEL), x2)

    nj = m // CHUNK
    hn1 = _rmsnorm_chunk_major(h1, norm_g[1])
    ug = _ssm_in_proj(hn1, ssm_w_in[0])
    mt, be, cet, apr, api, bpr, bpi = _ssm_prep(ssm_lam_re[0], ssm_lam_im[0], ssm_log_dt[0],
                                      ssm_b_re[0], ssm_b_im[0], ssm_c_re[0], ssm_c_im[0])
    dvec = jnp.tile(ssm_d[0].astype(F32).reshape(N_GROUPS, 1, GROUP_SIZE), (1, 1, CHUNK))
    y = _ssm(ug.reshape(CHUNK, nj, 2 * SSM_WIDTH), mt, be, cet, apr, api, bpr, bpi, dvec, batch)
    y = y.reshape(m, SSM_WIDTH)
    p = _glu(y, ssm_w_glu[0].astype(BF16), ug)
    out = _ssm_out_proj(p, ssm_w_out[0].astype(BF16), h1)
    return out.reshape(batch, seq, d)
```

```python
import functools

import jax
import jax.numpy as jnp
from jax import lax
from jax.experimental import pallas as pl
from jax.experimental.pallas import tpu as pltpu

F32 = jnp.float32
BF16 = jnp.bfloat16

D_MODEL = 4096
HEAD_DIM = 64
N_Q_HEADS = 64
N_KV_HEADS = 8
Q_PER_KV = 8
N_PAIRS = Q_PER_KV // 2
KV_WIDTH = N_KV_HEADS * HEAD_DIM
GROUP_W = Q_PER_KV * HEAD_DIM
BLOCK = 128
STACK = Q_PER_KV * BLOCK
SSM_WIDTH = 8192
GROUP_SIZE = 16
N_GROUPS = 512
STATE_DIM = 64
CHUNK = 16
CHUNK_W = CHUNK * GROUP_SIZE
RMS_EPS = 1e-6
NEG_INF = -1e30

LANES = 128
VMEM_LIMIT = 60 * 1024 * 1024
MXU_W = 256
GROUPS_PER_BLOCK = LANES // GROUP_SIZE

NT = (((1,), (1,)), ((), ()))


def _lane_roll(x, shift):
    return jnp.concatenate([x[:, LANES - shift:], x[:, :LANES - shift]], axis=1)


def _cast_weight_once(w_ref, wb_ref):
    @pl.when(pl.program_id(1) == 0)
    def _():
        wb_ref[...] = w_ref[...].astype(BF16)


def _sigmoid(x):
    return 0.5 * jnp.tanh(0.5 * x) + 0.5


def _params(n_grid_dims):
    return pltpu.CompilerParams(
        dimension_semantics=("arbitrary",) * n_grid_dims, vmem_limit_bytes=VMEM_LIMIT)


def _rms(x, g):
    var = jnp.mean(x * x, axis=-1, keepdims=True)
    return x * lax.rsqrt(var + RMS_EPS) * g


def _rmsnorm_kernel(x_ref, g_ref, o_ref):
    o_ref[...] = _rms(x_ref[...], g_ref[...]).astype(o_ref.dtype)


def _rmsnorm(x, g, tm=256):
    m, d = x.shape
    return pl.pallas_call(
        _rmsnorm_kernel,
        out_shape=jax.ShapeDtypeStruct((m, d), BF16),
        grid=(m // tm,),
        in_specs=[pl.BlockSpec((tm, d), lambda i: (i, 0)),
                  pl.BlockSpec((1, d), lambda i: (0, 0))],
        out_specs=pl.BlockSpec((tm, d), lambda i: (i, 0)),
        compiler_params=_params(1),
        name="rmsnorm",
    )(x, g.reshape(1, d))


def _rmsnorm_chunk_major_kernel(x_ref, g_ref, o_ref, cols_ref):
    hn = _rms(x_ref[...], g_ref[...])
    n_cols = hn.shape[1] // LANES
    nj = hn.shape[0] // CHUNK
    for c in range(n_cols):
        cols_ref[c] = hn[:, c * LANES:(c + 1) * LANES]
    for s in range(CHUNK):
        for c in range(n_cols):
            o_ref[s, :, c * LANES:(c + 1) * LANES] = cols_ref[
                c, pl.ds(s, nj, stride=CHUNK), :].astype(o_ref.dtype)


def _rmsnorm_chunk_major(x, g, tm=256):
    m, d = x.shape
    nj = tm // CHUNK
    out = pl.pallas_call(
        _rmsnorm_chunk_major_kernel,
        out_shape=jax.ShapeDtypeStruct((CHUNK, m // CHUNK, d), BF16),
        grid=(m // tm,),
        in_specs=[pl.BlockSpec((tm, d), lambda i: (i, 0)),
                  pl.BlockSpec((1, d), lambda i: (0, 0))],
        out_specs=pl.BlockSpec((CHUNK, nj, d), lambda i: (0, i, 0)),
        scratch_shapes=[pltpu.VMEM((d // LANES, tm, LANES), F32)],
        compiler_params=_params(1),
        name="rmsnorm_chunk_major",
    )(x, g.reshape(1, d))
    return out.reshape(m, d)


def _head_rmsnorm(acc, ones_bd, gain):
    outs = []
    for c in range(acc.shape[1] // MXU_W):
        a = acc[:, c * MXU_W:(c + 1) * MXU_W]
        ss = jnp.dot((a * a).astype(BF16), ones_bd, preferred_element_type=F32)
        inv = lax.rsqrt(ss * (1.0 / HEAD_DIM) + RMS_EPS)
        outs.append(a * inv * gain[:, c * MXU_W:(c + 1) * MXU_W])
    return outs[0] if len(outs) == 1 else jnp.concatenate(outs, axis=1)


def _q_proj_kernel(x_ref, w_ref, e_ref, g_ref, o_ref, wb_ref):
    _cast_weight_once(w_ref, wb_ref)
    acc = jnp.dot(x_ref[...], wb_ref[...], preferred_element_type=F32)
    o_ref[...] = _head_rmsnorm(acc, e_ref[...], g_ref[...]).astype(o_ref.dtype)


def _kv_proj_kernel(x_ref, w_ref, e_ref, g_ref, o_ref, wb_ref):
    _cast_weight_once(w_ref, wb_ref)
    acc = jnp.dot(x_ref[...], wb_ref[...], preferred_element_type=F32)
    kn = _head_rmsnorm(acc[:, :KV_WIDTH], e_ref[...], g_ref[...])
    for h in range(N_KV_HEADS):
        o_ref[h, :, 0:HEAD_DIM] = kn[:, h * HEAD_DIM:(h + 1) * HEAD_DIM].astype(o_ref.dtype)
        o_ref[h, :, HEAD_DIM:2 * HEAD_DIM] = acc[
            :, KV_WIDTH + h * HEAD_DIM:KV_WIDTH + (h + 1) * HEAD_DIM].astype(o_ref.dtype)


def _cast_proj_kernel(x_ref, w_ref, o_ref, wb_ref):
    _cast_weight_once(w_ref, wb_ref)
    o_ref[...] = jnp.dot(x_ref[...], wb_ref[...], preferred_element_type=F32).astype(o_ref.dtype)


def _ones_block_diag():
    r = jnp.arange(MXU_W) // HEAD_DIM
    return (r[:, None] == r[None, :]).astype(BF16)


Q_COL0 = 0
KV_COL0 = D_MODEL
GATE_COL0 = D_MODEL + 2 * KV_WIDTH


def _q_proj(hn, w_in, gain, tm=1024):
    m, k = hn.shape
    return pl.pallas_call(
        _q_proj_kernel,
        out_shape=jax.ShapeDtypeStruct((N_KV_HEADS, m, GROUP_W), BF16),
        grid=(N_KV_HEADS, m // tm),
        in_specs=[pl.BlockSpec((tm, k), lambda j, i: (i, 0)),
                  pl.BlockSpec((k, GROUP_W), lambda j, i: (0, Q_COL0 // GROUP_W + j)),
                  pl.BlockSpec((MXU_W, MXU_W), lambda j, i: (0, 0)),
                  pl.BlockSpec((1, GROUP_W), lambda j, i: (0, 0))],
        out_specs=pl.BlockSpec((None, tm, GROUP_W), lambda j, i: (j, i, 0)),
        scratch_shapes=[pltpu.VMEM((k, GROUP_W), BF16)],
        compiler_params=_params(2),
        name="attn_q_proj",
    )(hn, w_in, _ones_block_diag(), gain)


def _kv_proj(hn, w_in, gain, tm=512):
    m, k = hn.shape
    n = 2 * KV_WIDTH
    return pl.pallas_call(
        _kv_proj_kernel,
        out_shape=jax.ShapeDtypeStruct((N_KV_HEADS, m, 2 * HEAD_DIM), BF16),
        grid=(1, m // tm),
        in_specs=[pl.BlockSpec((tm, k), lambda j, i: (i, 0)),
                  pl.BlockSpec((k, n), lambda j, i: (0, KV_COL0 // n)),
                  pl.BlockSpec((MXU_W, MXU_W), lambda j, i: (0, 0)),
                  pl.BlockSpec((1, KV_WIDTH), lambda j, i: (0, 0))],
        out_specs=pl.BlockSpec((N_KV_HEADS, tm, 2 * HEAD_DIM), lambda j, i: (0, i, 0)),
        scratch_shapes=[pltpu.VMEM((k, n), BF16)],
        compiler_params=_params(2),
        name="attn_kv_proj",
    )(hn, w_in, _ones_block_diag(), gain)


def _gate_proj(hn, w_in, tm=1024):
    m, k = hn.shape
    return pl.pallas_call(
        _cast_proj_kernel,
        out_shape=jax.ShapeDtypeStruct((N_KV_HEADS, m, GROUP_W), BF16),
        grid=(N_KV_HEADS, m // tm),
        in_specs=[pl.BlockSpec((tm, k), lambda j, i: (i, 0)),
                  pl.BlockSpec((k, GROUP_W), lambda j, i: (0, GATE_COL0 // GROUP_W + j))],
        out_specs=pl.BlockSpec((None, tm, GROUP_W), lambda j, i: (j, i, 0)),
        scratch_shapes=[pltpu.VMEM((k, GROUP_W), BF16)],
        compiler_params=_params(2),
        name="attn_gate_proj",
    )(hn, w_in)


def _attn_kernel(sink_ref, alibi_ref, q_ref, kvc_ref, kvp_ref, gate_ref, o_ref):
    row = lax.broadcasted_iota(jnp.int32, (BLOCK, BLOCK), 0)
    col = lax.broadcasted_iota(jnp.int32, (BLOCK, BLOCK), 1)
    lower = col <= row
    left = col < HEAD_DIM
    ones = jnp.ones((2 * BLOCK, LANES), BF16)

    def split_kv(kv):
        swapped = _lane_roll(kv, HEAD_DIM)
        zero = jnp.zeros_like(kv)
        return jnp.where(left, kv, zero), jnp.where(left, zero, swapped), jnp.where(left, swapped, kv)

    def kv_head(kvh, carry):
        kc_even, kc_odd, vvc = split_kv(kvc_ref[kvh])
        kp_even, kp_odd, vvp = split_kv(kvp_ref[kvh])
        qp = jnp.concatenate(
            [q_ref[kvh, :, a * LANES:(a + 1) * LANES] for a in range(N_PAIRS)], axis=0)
        keys = jnp.concatenate([kc_even, kc_odd, kp_even, kp_odd], axis=0)
        s_all = lax.dot_general(qp, keys, NT, preferred_element_type=F32)
        v_ones = jnp.concatenate([jnp.concatenate([vvc, vvp], axis=0), ones], axis=1)
        ps, sink_terms = [], []
        for par in range(2):
            for a in range(N_PAIRS):
                rows = slice(a * BLOCK, (a + 1) * BLOCK)
                sc = s_all[rows, par * LANES:(par + 1) * LANES]
                sp = s_all[rows, (2 + par) * LANES:(3 + par) * LANES]
                blk = par * N_PAIRS + a
                s = jnp.where(lower, sc, sp) + alibi_ref[kvh, blk * BLOCK:(blk + 1) * BLOCK, :]
                sink = sink_ref[kvh * Q_PER_KV + 2 * a + par]
                m = jnp.maximum(jnp.max(s, axis=-1, keepdims=True), sink)
                p = jnp.exp2(s - m)
                ps.append(jnp.concatenate([jnp.where(lower, p, 0.0), jnp.where(lower, 0.0, p)],
                                          axis=1).astype(BF16))
                sink_terms.append(jnp.exp2(sink - m))
        nd = jnp.dot(jnp.concatenate(ps, axis=0), v_ones, preferred_element_type=F32)
        for a in range(N_PAIRS):
            o = []
            for par in range(2):
                blk = par * N_PAIRS + a
                r = nd[blk * BLOCK:(blk + 1) * BLOCK]
                o.append(r[:, :LANES] / (r[:, LANES:] + sink_terms[blk]))
            gt = gate_ref[kvh, :, a * LANES:(a + 1) * LANES].astype(F32)
            o_ref[kvh, :, a * LANES:(a + 1) * LANES] = (
                jnp.where(left, o[0], o[1]) * (gt * _sigmoid(gt))).astype(o_ref.dtype)
        return carry

    lax.fori_loop(0, N_KV_HEADS, kv_head, 0, unroll=True)


LOG2E = 1.4426950408889634


def _alibi_table():
    qi = jnp.arange(BLOCK)[:, None]
    kj = jnp.arange(BLOCK)[None, :]
    dist = jnp.where(kj <= qi, qi - kj, BLOCK + qi - kj).astype(F32)
    slopes = jnp.exp2(-8.0 * jnp.arange(1, N_Q_HEADS + 1, dtype=F32) / N_Q_HEADS)
    slopes = slopes.reshape(N_KV_HEADS, N_PAIRS, 2).transpose(0, 2, 1)
    bias = (-LOG2E * slopes[:, :, :, None, None] * dist)
    first = jnp.where(kj <= qi, bias, NEG_INF)
    return jnp.stack([first, bias]).reshape(2, N_KV_HEADS, STACK, BLOCK)


def _attention(q, kv, gate, sinks, batch, seq):
    nb = seq // BLOCK
    m = batch * seq
    blk = lambda b, n: (0, b * nb + n, 0)
    prev = lambda b, n: (0, b * nb + jnp.maximum(n - 1, 0), 0)
    return pl.pallas_call(
        _attn_kernel,
        out_shape=jax.ShapeDtypeStruct((N_KV_HEADS, m, GROUP_W), BF16),
        grid=(batch, nb),
        in_specs=[pl.BlockSpec(memory_space=pltpu.SMEM),
                  pl.BlockSpec((None, N_KV_HEADS, STACK, BLOCK), lambda b, n: (jnp.minimum(n, 1), 0, 0, 0)),
                  pl.BlockSpec((N_KV_HEADS, BLOCK, GROUP_W), blk),
                  pl.BlockSpec((N_KV_HEADS, BLOCK, 2 * HEAD_DIM), blk),
                  pl.BlockSpec((N_KV_HEADS, BLOCK, 2 * HEAD_DIM), prev),
                  pl.BlockSpec((N_KV_HEADS, BLOCK, GROUP_W), blk)],
        out_specs=pl.BlockSpec((N_KV_HEADS, BLOCK, GROUP_W), blk),
        compiler_params=_params(2),
        name="band_attention",
    )(sinks * LOG2E, _alibi_table(), q, kv, kv, gate)


def _attn_out_kernel(x_ref, w_ref, r_ref, o_ref, wb_ref):
    _cast_weight_once(w_ref, wb_ref)
    acc = r_ref[...]
    for h in range(N_KV_HEADS):
        acc = acc + jnp.dot(x_ref[h], wb_ref[h], preferred_element_type=F32)
    o_ref[...] = acc


def _attn_out_proj(og, w, resid, tm=1024, tn=512):
    _, m, _ = og.shape
    n = w.shape[2]
    return pl.pallas_call(
        _attn_out_kernel,
        out_shape=jax.ShapeDtypeStruct((m, n), F32),
        grid=(n // tn, m // tm),
        in_specs=[pl.BlockSpec((N_KV_HEADS, tm, GROUP_W), lambda j, i: (0, i, 0)),
                  pl.BlockSpec((N_KV_HEADS, GROUP_W, tn), lambda j, i: (0, 0, j)),
                  pl.BlockSpec((tm, tn), lambda j, i: (i, j))],
        out_specs=pl.BlockSpec((tm, tn), lambda j, i: (i, j)),
        scratch_shapes=[pltpu.VMEM((N_KV_HEADS, GROUP_W, tn), BF16)],
        compiler_params=_params(2),
        name="attn_out_proj",
    )(og, w, resid)


def _ssm_in_kernel(x_ref, w_ref, o_ref, wb_ref):
    jj, i = pl.program_id(0), pl.program_id(1)
    chunk = w_ref.shape[0]
    fill = jj % 2
    rows = pl.ds(pl.multiple_of(i * chunk, chunk), chunk)

    @pl.when(jj == 0)
    def _():
        wb_ref[fill, rows, :] = w_ref[...].astype(BF16)

    @pl.when(jj > 0)
    def _():
        wb_ref[fill, rows, :] = w_ref[...].astype(BF16)
        o_ref[...] = jnp.dot(x_ref[...], wb_ref[1 - fill], preferred_element_type=F32).astype(o_ref.dtype)


def _ssm_in_proj(x, w, tm=1024, tn=1024):
    m, k = x.shape
    n = w.shape[1]
    ni, nj = m // tm, n // tn
    chunk = k // ni
    row_tile = lambda jj, i: jnp.where(jj == 0, 0, i)
    return pl.pallas_call(
        _ssm_in_kernel,
        out_shape=jax.ShapeDtypeStruct((m, n), BF16),
        grid=(nj + 1, ni),
        in_specs=[pl.BlockSpec((tm, k), lambda jj, i: (row_tile(jj, i), 0)),
                  pl.BlockSpec((chunk, tn), lambda jj, i: (i, jnp.minimum(jj, nj - 1)))],
        out_specs=pl.BlockSpec((tm, tn), lambda jj, i: (row_tile(jj, i), jnp.maximum(jj - 1, 0))),
        scratch_shapes=[pltpu.VMEM((2, k, tn), BF16)],
        compiler_params=_params(2),
        name="ssm_in_proj",
    )(x, w)


def _glu_kernel(x_ref, w_ref, y_ref, gate_ref, o_ref):
    gt = gate_ref[...]
    gated = y_ref[...] * (gt * _sigmoid(gt))
    z = jnp.dot(x_ref[...], w_ref[...], preferred_element_type=F32)
    o_ref[...] = gated * _sigmoid(z.astype(BF16))


def _glu(y, w, ug, tm=1024, tn=512):
    m, k = y.shape
    n = w.shape[1]
    gate_off = n // tn
    return pl.pallas_call(
        _glu_kernel,
        out_shape=jax.ShapeDtypeStruct((m, n), BF16),
        grid=(m // tm, n // tn),
        in_specs=[pl.BlockSpec((tm, k), lambda i, j: (i, 0)),
                  pl.BlockSpec((k, tn), lambda i, j: (0, j)),
                  pl.BlockSpec((tm, tn), lambda i, j: (i, j)),
                  pl.BlockSpec((tm, tn), lambda i, j: (i, j + gate_off))],
        out_specs=pl.BlockSpec((tm, tn), lambda i, j: (i, j)),
        compiler_params=_params(2),
        name="ssm_glu",
    )(y, w, y, ug)


def _ssm_out_kernel(x_ref, w_ref, r_ref, o_ref, cols_ref):
    nj = x_ref.shape[1]
    x = x_ref[...].reshape(CHUNK * nj, x_ref.shape[2])
    acc = jnp.dot(x, w_ref[...], preferred_element_type=F32)
    for c in range(acc.shape[1] // LANES):
        for s in range(CHUNK):
            cols_ref[c, pl.ds(s, nj, stride=CHUNK), :] = acc[
                s * nj:(s + 1) * nj, c * LANES:(c + 1) * LANES]
        o_ref[:, c * LANES:(c + 1) * LANES] = cols_ref[c] + r_ref[:, c * LANES:(c + 1) * LANES]


def _ssm_out_proj(x, w, resid, tm=1024, tn=512):
    m, k = x.shape
    n = w.shape[1]
    nj = tm // CHUNK
    return pl.pallas_call(
        _ssm_out_kernel,
        out_shape=jax.ShapeDtypeStruct((m, n), F32),
        grid=(m // tm, n // tn),
        in_specs=[pl.BlockSpec((CHUNK, nj, k), lambda i, j: (0, i, 0)),
                  pl.BlockSpec((k, tn), lambda i, j: (0, j)),
                  pl.BlockSpec((tm, tn), lambda i, j: (i, j))],
        out_specs=pl.BlockSpec((tm, tn), lambda i, j: (i, j)),
        scratch_shapes=[pltpu.VMEM((tn // LANES, tm, LANES), F32)],
        compiler_params=_params(2),
        name="ssm_out_proj",
    )(x.reshape(CHUNK, m // CHUNK, k), w, resid)


N_SCAN_STEPS = 8
SCAN_BLOCK = 8
N_LOCAL_STEPS = 3


def _ssm_prep_kernel(lr_ref, li_ref, ldt_ref, btr_ref, bti_ref, cr_ref, ci_ref,
                     m_ref, be_ref, cet_ref, apr_ref, api_ref, bpr_ref, bpi_ref):
    tau = lax.broadcasted_iota(jnp.int32, (24, 2 * STATE_DIM), 0).astype(F32)
    quarter = jnp.where(lax.broadcasted_iota(jnp.int32, (24, 2 * STATE_DIM), 1) < STATE_DIM, 0.0, 0.5 * jnp.pi)
    lane = lax.broadcasted_iota(jnp.int32, (GROUP_SIZE, CHUNK_W), 1)

    def one_group(g, carry):
        lr = lr_ref[g]
        li = li_ref[g]
        dt = jnp.exp(ldt_ref[g])
        lr2 = jnp.concatenate([lr, lr], axis=1)
        li2 = jnp.concatenate([li, li], axis=1)
        dt2 = jnp.concatenate([dt, dt], axis=1)
        powers = jnp.exp(tau * (lr2 * dt2)) * jnp.cos(tau * (li2 * dt2) - quarter)
        pr = powers[:, :STATE_DIM]
        pi = powers[:, STATE_DIM:]
        ar, ai = pr[1:2], pi[1:2]
        den = lr * lr + li * li
        xr = ar - 1.0
        wr = (xr * lr + ai * li) / den
        wi = (ai * lr - xr * li) / den
        btr, bti = btr_ref[g], bti_ref[g]
        br = btr * wr - bti * wi
        bi = btr * wi + bti * wr
        cr, ci = cr_ref[g], ci_ref[g]

        ca_r, ca_i = [], []
        for t in range(CHUNK + 1):
            ca_r.append(cr * pr[t:t + 1] - ci * pi[t:t + 1])
            ca_i.append(cr * pi[t:t + 1] + ci * pr[t:t + 1])

        rt = jnp.concatenate([jnp.concatenate(ca_r[:CHUNK], axis=0),
                              jnp.concatenate(ca_i[:CHUNK], axis=0)], axis=1)
        bcat = jnp.concatenate([br, -bi], axis=1)
        kp = lax.dot_general(bcat.astype(BF16), rt.astype(BF16), NT,
                             preferred_element_type=F32)
        for s in range(CHUNK):
            blk = kp if s == 0 else pltpu.roll(kp, GROUP_SIZE * s, 1)
            blk = jnp.where(lane >= GROUP_SIZE * s, blk, 0.0)
            blk = jnp.concatenate([pltpu.roll(blk[:, :LANES], GROUP_SIZE * g, 1),
                                   pltpu.roll(blk[:, LANES:], GROUP_SIZE * g, 1)], axis=1)
            dst = pl.ds(pl.multiple_of(_step_slot(s, g) * GROUP_SIZE, GROUP_SIZE), GROUP_SIZE)
            m_ref[g, dst, :] = blk.astype(m_ref.dtype)
            e = CHUNK - 1 - s
            be_re = br * pr[e:e + 1] - bi * pi[e:e + 1]
            be_im = br * pi[e:e + 1] + bi * pr[e:e + 1]
            be_ref[g, dst, :] = jnp.concatenate(
                [be_re, be_im], axis=1).astype(be_ref.dtype)
            cet_ref[g, dst, :] = jnp.concatenate(
                [ca_r[s + 1], -ca_i[s + 1]], axis=1).astype(cet_ref.dtype)

        sq_r, sq_i = [pr[CHUNK:CHUNK + 1]], [pi[CHUNK:CHUNK + 1]]
        for _ in range(N_SCAN_STEPS - 1):
            xr, xi = sq_r[-1], sq_i[-1]
            sq_r.append(xr * xr - xi * xi)
            sq_i.append(2.0 * (xr * xi))
        mul_r, mul_i = [jnp.ones_like(sq_r[0]), sq_r[0]], [jnp.zeros_like(sq_r[0]), sq_i[0]]
        for _ in range(SCAN_BLOCK - 2):
            xr, xi = mul_r[-1], mul_i[-1]
            mul_r.append(xr * sq_r[0] - xi * sq_i[0])
            mul_i.append(xr * sq_i[0] + xi * sq_r[0])
        sr, si = jnp.concatenate(sq_r, axis=0), jnp.concatenate(sq_i, axis=0)
        rr, ri = jnp.concatenate(mul_r, axis=0), jnp.concatenate(mul_i, axis=0)
        apr_ref[g] = jnp.concatenate([sr, sr], axis=1)
        api_ref[g] = jnp.concatenate([-si, si], axis=1)
        bpr_ref[g] = jnp.concatenate([rr, rr], axis=1)
        bpi_ref[g] = jnp.concatenate([-ri, ri], axis=1)
        return carry

    lax.fori_loop(0, GROUPS_PER_BLOCK, one_group, 0, unroll=4)


def _ssm_prep(lam_re, lam_im, log_dt, b_re, b_im, c_re, c_im):
    g = N_GROUPS
    gb = GROUPS_PER_BLOCK
    row = lambda a: a.reshape(g, 1, STATE_DIM)
    ldt = jnp.broadcast_to(log_dt.reshape(g, 1, 1), (g, 1, STATE_DIM))
    btr = jnp.swapaxes(b_re, 1, 2)
    bti = jnp.swapaxes(b_im, 1, 2)
    vec = pl.BlockSpec((gb, 1, STATE_DIM), lambda i: (i, 0, 0))
    mat = pl.BlockSpec((gb, GROUP_SIZE, STATE_DIM), lambda i: (i, 0, 0))
    return pl.pallas_call(
        _ssm_prep_kernel,
        out_shape=[jax.ShapeDtypeStruct((g, CHUNK_W, CHUNK_W), BF16),
                   jax.ShapeDtypeStruct((g, CHUNK_W, 2 * STATE_DIM), BF16),
                   jax.ShapeDtypeStruct((g, CHUNK_W, 2 * STATE_DIM), BF16),
                   jax.ShapeDtypeStruct((g, N_SCAN_STEPS, 2 * STATE_DIM), F32),
                   jax.ShapeDtypeStruct((g, N_SCAN_STEPS, 2 * STATE_DIM), F32),
                   jax.ShapeDtypeStruct((g, SCAN_BLOCK, 2 * STATE_DIM), F32),
                   jax.ShapeDtypeStruct((g, SCAN_BLOCK, 2 * STATE_DIM), F32)],
        grid=(g // gb,),
        in_specs=[vec, vec, vec, mat, mat, mat, mat],
        out_specs=[pl.BlockSpec((gb, CHUNK_W, CHUNK_W), lambda i: (i, 0, 0)),
                   pl.BlockSpec((gb, CHUNK_W, 2 * STATE_DIM), lambda i: (i, 0, 0)),
                   pl.BlockSpec((gb, CHUNK_W, 2 * STATE_DIM), lambda i: (i, 0, 0)),
                   pl.BlockSpec((gb, N_SCAN_STEPS, 2 * STATE_DIM), lambda i: (i, 0, 0)),
                   pl.BlockSpec((gb, N_SCAN_STEPS, 2 * STATE_DIM), lambda i: (i, 0, 0)),
                   pl.BlockSpec((gb, SCAN_BLOCK, 2 * STATE_DIM), lambda i: (i, 0, 0)),
                   pl.BlockSpec((gb, SCAN_BLOCK, 2 * STATE_DIM), lambda i: (i, 0, 0))],
        compiler_params=_params(1),
        name="ssm_prep",
    )(row(lam_re), row(lam_im), ldt, btr, bti, c_re, c_im)


XPOSE_ROWS = 256


def _step_slot(step, group):
    half, k = divmod(step, GROUPS_PER_BLOCK)
    return half * GROUPS_PER_BLOCK + (k + group) % GROUPS_PER_BLOCK


def _slot_masks(shape):
    slot = lax.broadcasted_iota(jnp.int32, shape, 1) // GROUP_SIZE
    return [slot == p for p in range(GROUPS_PER_BLOCK)]


def _merge_slots(pieces, masks):
    out = pieces[0]
    for p in range(1, GROUPS_PER_BLOCK):
        out = jnp.where(masks[p], pieces[p], out)
    return out


def _steps_to_groups(xs):
    masks = _slot_masks(xs[0].shape)
    n = GROUPS_PER_BLOCK
    rot = [xs[k] if k == 0 else _lane_roll(xs[k], GROUP_SIZE * k) for k in range(n)]
    return [_merge_slots([rot[(p - g) % n] for p in range(n)], masks) for g in range(n)]


def _groups_to_steps(ys):
    masks = _slot_masks(ys[0].shape)
    n = GROUPS_PER_BLOCK
    out = []
    for k in range(n):
        z = _merge_slots([ys[(p - k) % n] for p in range(n)], masks)
        out.append(z if k == 0 else _lane_roll(z, LANES - GROUP_SIZE * k))
    return out


def _cmul(x, mr, mi):
    return x * mr + pltpu.roll(x, STATE_DIM, 1) * mi


def _ssm_kernel(x_ref, m_ref, be_ref, cet_ref, apr_ref, api_ref, bpr_ref, bpi_ref, d_ref, o_ref,
                uf_ref, yf_ref, xs_ref, cs_ref, *, batch, n_chunks):
    rows = batch * n_chunks
    n_blocks = n_chunks // SCAN_BLOCK

    def gather(i, carry):
        r0 = pl.multiple_of(i * XPOSE_ROWS, XPOSE_ROWS)
        for half in range(2):
            vs = [x_ref[GROUPS_PER_BLOCK * half + k, pl.ds(r0, XPOSE_ROWS), :]
                  for k in range(GROUPS_PER_BLOCK)]
            ts = _steps_to_groups(vs)
            for g in range(GROUPS_PER_BLOCK):
                uf_ref[g, pl.ds(r0, XPOSE_ROWS), half * LANES:(half + 1) * LANES] = ts[g]
        return carry

    lax.fori_loop(0, rows // XPOSE_ROWS, gather, 0)

    n_all = rows // SCAN_BLOCK
    row = lax.broadcasted_iota(jnp.int32, (rows, 2 * STATE_DIM), 0)
    in_block = row & (SCAN_BLOCK - 1)
    brow = lax.broadcasted_iota(jnp.int32, (n_all, 2 * STATE_DIM), 0)
    in_seq = brow & (n_blocks - 1)

    def shift_rows(x, d, pos):
        return jnp.where(pos >= d, pltpu.roll(x, d, 0), 0.0)

    def one_group(g, carry):
        u = uf_ref[g]
        y = jnp.dot(u, m_ref[g], preferred_element_type=F32)
        x = jnp.dot(u, be_ref[g], preferred_element_type=F32)
        apr, api = apr_ref[g], api_ref[g]
        for k in range(N_LOCAL_STEPS):
            x = x + _cmul(shift_rows(x, 1 << k, in_block), apr[k:k + 1], api[k:k + 1])
        xs_ref[...] = x
        c = xs_ref[pl.ds(SCAN_BLOCK - 1, n_all, stride=SCAN_BLOCK), :]
        for k in range(N_LOCAL_STEPS, N_SCAN_STEPS):
            c = c + _cmul(shift_rows(c, 1 << (k - N_LOCAL_STEPS), in_seq), apr[k:k + 1], api[k:k + 1])
        cs_ref[...] = shift_rows(c, 1, in_seq)
        bpr, bpi = bpr_ref[g], bpi_ref[g]
        carried = [_cmul(jnp.broadcast_to(cs_ref[mblk:mblk + 1, :], (SCAN_BLOCK, 2 * STATE_DIM)), bpr, bpi)
                   for mblk in range(n_all)]
        e = (shift_rows(x, 1, in_block) + jnp.concatenate(carried, axis=0)).astype(BF16)
        y = y + lax.dot_general(e, cet_ref[g], NT, preferred_element_type=F32)
        y = y + d_ref[g] * u.astype(F32)
        yf_ref[g] = jax.nn.gelu(y).astype(yf_ref.dtype)
        return carry

    lax.fori_loop(0, GROUPS_PER_BLOCK, one_group, 0, unroll=4)

    def scatter(i, carry):
        r0 = pl.multiple_of(i * XPOSE_ROWS, XPOSE_ROWS)
        for half in range(2):
            vs = [yf_ref[g, pl.ds(r0, XPOSE_ROWS), half * LANES:(half + 1) * LANES]
                  for g in range(GROUPS_PER_BLOCK)]
            ts = _groups_to_steps(vs)
            for k in range(GROUPS_PER_BLOCK):
                o_ref[GROUPS_PER_BLOCK * half + k, pl.ds(r0, XPOSE_ROWS), :] = ts[k]
        return carry

    lax.fori_loop(0, rows // XPOSE_ROWS, scatter, 0)


def _ssm(ug, m, be, cet, apr, api, bpr, bpi, dvec, batch):
    _, rows, _ = ug.shape
    n_chunks = rows // batch
    gb = GROUPS_PER_BLOCK
    col_blk = pl.BlockSpec((CHUNK, rows, LANES), lambda i: (0, 0, i))
    grp = lambda last2: pl.BlockSpec((gb,) + last2, lambda i: (i, 0, 0))
    return pl.pallas_call(
        functools.partial(_ssm_kernel, batch=batch, n_chunks=n_chunks),
        out_shape=jax.ShapeDtypeStruct((CHUNK, rows, SSM_WIDTH), BF16),
        grid=(N_GROUPS // gb,),
        in_specs=[col_blk, grp((CHUNK_W, CHUNK_W)), grp((CHUNK_W, 2 * STATE_DIM)),
                  grp((CHUNK_W, 2 * STATE_DIM)), grp((N_SCAN_STEPS, 2 * STATE_DIM)),
                  grp((N_SCAN_STEPS, 2 * STATE_DIM)), grp((SCAN_BLOCK, 2 * STATE_DIM)),
                  grp((SCAN_BLOCK, 2 * STATE_DIM)), grp((1, CHUNK_W))],
        out_specs=col_blk,
        scratch_shapes=[pltpu.VMEM((gb, rows, CHUNK_W), BF16),
                        pltpu.VMEM((gb, rows, CHUNK_W), BF16),
                        pltpu.VMEM((rows, 2 * STATE_DIM), F32),
                        pltpu.VMEM((rows // SCAN_BLOCK, 2 * STATE_DIM), F32)],
        compiler_params=_params(1),
        name="ssm_chunked",
    )(ug, m, be, cet, apr, api, bpr, bpi, dvec)


def kernel(x, norm_g, attn_w_in, attn_q_norm_g, attn_k_norm_g, attn_sinks, attn_w_out, ssm_w_in,
           ssm_log_dt, ssm_lam_re, ssm_lam_im, ssm_b_re, ssm_b_im, ssm_c_re, ssm_c_im, ssm_d,
           ssm_w_glu, ssm_w_out):
    batch, seq, d = x.shape
    m = batch * seq
    assert d == D_MODEL and seq % BLOCK == 0 and seq // CHUNK == 1 << N_SCAN_STEPS
    x2 = x.reshape(m, d)

    w_in = attn_w_in[0]
    q_gain = jnp.tile(attn_q_norm_g[0].astype(F32) * (HEAD_DIM ** -0.5 * LOG2E), Q_PER_KV).reshape(1, GROUP_W)
    k_gain = jnp.tile(attn_k_norm_g[0].astype(F32), N_KV_HEADS).reshape(1, KV_WIDTH)

    hn0 = _rmsnorm(x2, norm_g[0])
    q = _q_proj(hn0, w_in, q_gain)
    kv = _kv_proj(hn0, w_in, k_gain)
    gate = _gate_proj(hn0, w_in)
    og = _attention(q, kv, gate, attn_sinks[0].astype(F32), batch, seq)
    h1 = _attn_out_proj(og, attn_w_out[0].reshape(N_KV_HEADS, GROUP_W, D_MODEL), x2)

    nj = m // CHUNK
    hn1 = _rmsnorm_chunk_major(h1, norm_g[1])
    ug = _ssm_in_proj(hn1, ssm_w_in[0])
    mt, be, cet, apr, api, bpr, bpi = _ssm_prep(ssm_lam_re[0], ssm_lam_im[0], ssm_log_dt[0],
                                      ssm_b_re[0], ssm_b_im[0], ssm_c_re[0], ssm_c_im[0])
    dvec = jnp.tile(ssm_d[0].astype(F32).reshape(N_GROUPS, 1, GROUP_SIZE), (1, 1, CHUNK))
    y = _ssm(ug.reshape(CHUNK, nj, 2 * SSM_WIDTH), mt, be, cet, apr, api, bpr, bpi, dvec, batch)
    y = y.reshape(m, SSM_WIDTH)
    p = _glu(y, ssm_w_glu[0].astype(BF16), ug)
    out = _ssm_out_proj(p, ssm_w_out[0].astype(BF16), h1)
    return out.reshape(batch, seq, d)
```

```python
import functools

import jax
import jax.numpy as jnp
from jax import lax
from jax.experimental import pallas as pl
from jax.experimental.pallas import tpu as pltpu

F32 = jnp.float32
BF16 = jnp.bfloat16

D_MODEL = 4096
HEAD_DIM = 64
N_Q_HEADS = 64
N_KV_HEADS = 8
Q_PER_KV = 8
N_PAIRS = Q_PER_KV // 2
KV_WIDTH = N_KV_HEADS * HEAD_DIM
GROUP_W = Q_PER_KV * HEAD_DIM
BLOCK = 128
STACK = Q_PER_KV * BLOCK
SSM_WIDTH = 8192
GROUP_SIZE = 16
N_GROUPS = 512
STATE_DIM = 64
CHUNK = 16
CHUNK_W = CHUNK * GROUP_SIZE
RMS_EPS = 1e-6
NEG_INF = -1e30

LANES = 128
VMEM_LIMIT = 60 * 1024 * 1024
MXU_W = 256
GROUPS_PER_BLOCK = LANES // GROUP_SIZE

NT = (((1,), (1,)), ((), ()))


def _lane_roll(x, shift):
    return jnp.concatenate([x[:, LANES - shift:], x[:, :LANES - shift]], axis=1)


def _cast_weight_once(w_ref, wb_ref):
    @pl.when(pl.program_id(1) == 0)
    def _():
        wb_ref[...] = w_ref[...].astype(BF16)


def _sigmoid(x):
    return 0.5 * jnp.tanh(0.5 * x) + 0.5


def _params(n_grid_dims):
    return pltpu.CompilerParams(
        dimension_semantics=("arbitrary",) * n_grid_dims, vmem_limit_bytes=VMEM_LIMIT)


def _rms(x, g):
    var = jnp.mean(x * x, axis=-1, keepdims=True)
    return x * lax.rsqrt(var + RMS_EPS) * g


def _rmsnorm_kernel(x_ref, g_ref, o_ref):
    o_ref[...] = _rms(x_ref[...], g_ref[...]).astype(o_ref.dtype)


def _rmsnorm(x, g, tm=256):
    m, d = x.shape
    return pl.pallas_call(
        _rmsnorm_kernel,
        out_shape=jax.ShapeDtypeStruct((m, d), BF16),
        grid=(m // tm,),
        in_specs=[pl.BlockSpec((tm, d), lambda i: (i, 0)),
                  pl.BlockSpec((1, d), lambda i: (0, 0))],
        out_specs=pl.BlockSpec((tm, d), lambda i: (i, 0)),
        compiler_params=_params(1),
        name="rmsnorm",
    )(x, g.reshape(1, d))


def _rmsnorm_chunk_major_kernel(x_ref, g_ref, o_ref, cols_ref):
    hn = _rms(x_ref[...], g_ref[...])
    n_cols = hn.shape[1] // LANES
    nj = hn.shape[0] // CHUNK
    for c in range(n_cols):
        cols_ref[c] = hn[:, c * LANES:(c + 1) * LANES]
    for s in range(CHUNK):
        for c in range(n_cols):
            o_ref[s, :, c * LANES:(c + 1) * LANES] = cols_ref[
                c, pl.ds(s, nj, stride=CHUNK), :].astype(o_ref.dtype)


def _rmsnorm_chunk_major(x, g, tm=256):
    m, d = x.shape
    nj = tm // CHUNK
    out = pl.pallas_call(
        _rmsnorm_chunk_major_kernel,
        out_shape=jax.ShapeDtypeStruct((CHUNK, m // CHUNK, d), BF16),
        grid=(m // tm,),
        in_specs=[pl.BlockSpec((tm, d), lambda i: (i, 0)),
                  pl.BlockSpec((1, d), lambda i: (0, 0))],
        out_specs=pl.BlockSpec((CHUNK, nj, d), lambda i: (0, i, 0)),
        scratch_shapes=[pltpu.VMEM((d // LANES, tm, LANES), F32)],
        compiler_params=_params(1),
        name="rmsnorm_chunk_major",
    )(x, g.reshape(1, d))
    return out.reshape(m, d)


def _head_rmsnorm(acc, ones_bd, gain):
    outs = []
    for c in range(acc.shape[1] // MXU_W):
        a = acc[:, c * MXU_W:(c + 1) * MXU_W]
        ss = jnp.dot((a * a).astype(BF16), ones_bd, preferred_element_type=F32)
        inv = lax.rsqrt(ss * (1.0 / HEAD_DIM) + RMS_EPS)
        outs.append(a * inv * gain[:, c * MXU_W:(c + 1) * MXU_W])
    return outs[0] if len(outs) == 1 else jnp.concatenate(outs, axis=1)


def _ones_block_diag():
    r = jnp.arange(MXU_W) // HEAD_DIM
    return (r[:, None] == r[None, :]).astype(BF16)


IN_TILE = 1024
N_Q_TILES = D_MODEL // IN_TILE
KV_TILE = N_Q_TILES
GATE_TILE0 = KV_TILE + 1
N_IN_TILES = (2 * D_MODEL + 2 * KV_WIDTH) // IN_TILE


def _attn_in_kernel(x_ref, w_ref, e_ref, qg_ref, kg_ref, o_ref, wb_ref):
    jj, i = pl.program_id(0), pl.program_id(1)
    tile = jj - 1
    chunk = w_ref.shape[0]
    fill = jj % 2
    rows = pl.ds(pl.multiple_of(i * chunk, chunk), chunk)

    def stage_next_tile():
        wb_ref[fill, rows, :] = w_ref[...].astype(BF16)

    def product():
        return jnp.dot(x_ref[...], wb_ref[1 - fill], preferred_element_type=F32)

    @pl.when(tile < 0)
    def _():
        stage_next_tile()

    @pl.when(jnp.logical_and(tile >= 0, tile < KV_TILE))
    def _():
        stage_next_tile()
        o_ref[...] = _head_rmsnorm(product(), e_ref[...], qg_ref[...]).astype(o_ref.dtype)

    @pl.when(tile == KV_TILE)
    def _():
        stage_next_tile()
        acc = product()
        kn = _head_rmsnorm(acc[:, :KV_WIDTH], e_ref[...], kg_ref[...])
        for h in range(N_KV_HEADS):
            o_ref[:, h * LANES:h * LANES + HEAD_DIM] = kn[:, h * HEAD_DIM:(h + 1) * HEAD_DIM].astype(o_ref.dtype)
            o_ref[:, h * LANES + HEAD_DIM:(h + 1) * LANES] = acc[
                :, KV_WIDTH + h * HEAD_DIM:KV_WIDTH + (h + 1) * HEAD_DIM].astype(o_ref.dtype)

    @pl.when(tile > KV_TILE)
    def _():
        stage_next_tile()
        o_ref[...] = product().astype(o_ref.dtype)


def _attn_in_proj(hn, w_in, q_gain, k_gain, tm=1024):
    m, k = hn.shape
    ni = m // tm
    chunk = k // ni
    row_tile = lambda jj, i: jnp.where(jj == 0, 0, i)
    const = lambda jj, i: (0, 0)
    return pl.pallas_call(
        _attn_in_kernel,
        out_shape=jax.ShapeDtypeStruct((N_IN_TILES, m, IN_TILE), BF16),
        grid=(N_IN_TILES + 1, ni),
        in_specs=[pl.BlockSpec((tm, k), lambda jj, i: (row_tile(jj, i), 0)),
                  pl.BlockSpec((chunk, IN_TILE), lambda jj, i: (i, jnp.minimum(jj, N_IN_TILES - 1))),
                  pl.BlockSpec((MXU_W, MXU_W), const),
                  pl.BlockSpec((1, IN_TILE), const),
                  pl.BlockSpec((1, KV_WIDTH), const)],
        out_specs=pl.BlockSpec((None, tm, IN_TILE),
                               lambda jj, i: (jnp.maximum(jj - 1, 0), row_tile(jj, i), 0)),
        scratch_shapes=[pltpu.VMEM((2, k, IN_TILE), BF16)],
        compiler_params=_params(2),
        name="attn_in_proj",
    )(hn, w_in, _ones_block_diag(), q_gain, k_gain)


def _attn_kernel(sink_ref, alibi_ref, q_ref, kvc_ref, kvp_ref, g0_ref, g1_ref, g2_ref, g3_ref, o_ref):
    gate_refs = (g0_ref, g1_ref, g2_ref, g3_ref)
    row = lax.broadcasted_iota(jnp.int32, (BLOCK, BLOCK), 0)
    col = lax.broadcasted_iota(jnp.int32, (BLOCK, BLOCK), 1)
    lower = col <= row
    left = col < HEAD_DIM
    ones = jnp.ones((2 * BLOCK, LANES), BF16)

    def split_kv(kv):
        swapped = _lane_roll(kv, HEAD_DIM)
        zero = jnp.zeros_like(kv)
        return jnp.where(left, kv, zero), jnp.where(left, zero, swapped), jnp.where(left, swapped, kv)

    for kvh in range(N_KV_HEADS):
        tile, col0 = kvh // 2, (kvh % 2) * GROUP_W
        kc_even, kc_odd, vvc = split_kv(kvc_ref[:, kvh * LANES:(kvh + 1) * LANES])
        kp_even, kp_odd, vvp = split_kv(kvp_ref[:, kvh * LANES:(kvh + 1) * LANES])
        qp = jnp.concatenate(
            [q_ref[tile, :, col0 + a * LANES:col0 + (a + 1) * LANES] for a in range(N_PAIRS)], axis=0)
        keys = jnp.concatenate([kc_even, kc_odd, kp_even, kp_odd], axis=0)
        s_all = lax.dot_general(qp, keys, NT, preferred_element_type=F32)
        v_ones = jnp.concatenate([jnp.concatenate([vvc, vvp], axis=0), ones], axis=1)
        ps, sink_terms = [], []
        for par in range(2):
            for a in range(N_PAIRS):
                rows = slice(a * BLOCK, (a + 1) * BLOCK)
                sc = s_all[rows, par * LANES:(par + 1) * LANES]
                sp = s_all[rows, (2 + par) * LANES:(3 + par) * LANES]
                blk = par * N_PAIRS + a
                s = jnp.where(lower, sc, sp) + alibi_ref[kvh, blk * BLOCK:(blk + 1) * BLOCK, :]
                sink = sink_ref[kvh * Q_PER_KV + 2 * a + par]
                m = jnp.maximum(jnp.max(s, axis=-1, keepdims=True), sink)
                p = jnp.exp2(s - m)
                ps.append(jnp.concatenate([jnp.where(lower, p, 0.0), jnp.where(lower, 0.0, p)],
                                          axis=1).astype(BF16))
                sink_terms.append(jnp.exp2(sink - m))
        nd = jnp.dot(jnp.concatenate(ps, axis=0), v_ones, preferred_element_type=F32)
        for a in range(N_PAIRS):
            o = []
            for par in range(2):
                blk = par * N_PAIRS + a
                r = nd[blk * BLOCK:(blk + 1) * BLOCK]
                o.append(r[:, :LANES] / (r[:, LANES:] + sink_terms[blk]))
            gt = gate_refs[tile][:, col0 + a * LANES:col0 + (a + 1) * LANES].astype(F32)
            o_ref[kvh, :, a * LANES:(a + 1) * LANES] = (
                jnp.where(left, o[0], o[1]) * (gt * _sigmoid(gt))).astype(o_ref.dtype)


LOG2E = 1.4426950408889634


def _alibi_table():
    qi = jnp.arange(BLOCK)[:, None]
    kj = jnp.arange(BLOCK)[None, :]
    dist = jnp.where(kj <= qi, qi - kj, BLOCK + qi - kj).astype(F32)
    slopes = jnp.exp2(-8.0 * jnp.arange(1, N_Q_HEADS + 1, dtype=F32) / N_Q_HEADS)
    slopes = slopes.reshape(N_KV_HEADS, N_PAIRS, 2).transpose(0, 2, 1)
    bias = (-LOG2E * slopes[:, :, :, None, None] * dist)
    first = jnp.where(kj <= qi, bias, NEG_INF)
    return jnp.stack([first, bias]).reshape(2, N_KV_HEADS, STACK, BLOCK)


def _attention(proj, sinks, batch, seq):
    nb = seq // BLOCK
    m = batch * seq
    cur = lambda b, n: b * nb + n
    prev = lambda b, n: b * nb + jnp.maximum(n - 1, 0)
    tile = lambda t, rows: pl.BlockSpec((None, BLOCK, IN_TILE), lambda b, n: (t, rows(b, n), 0))
    return pl.pallas_call(
        _attn_kernel,
        out_shape=jax.ShapeDtypeStruct((N_KV_HEADS, m, GROUP_W), BF16),
        grid=(batch, nb),
        in_specs=[pl.BlockSpec(memory_space=pltpu.SMEM),
                  pl.BlockSpec((None, N_KV_HEADS, STACK, BLOCK), lambda b, n: (jnp.minimum(n, 1), 0, 0, 0)),
                  pl.BlockSpec((N_Q_TILES, BLOCK, IN_TILE), lambda b, n: (0, cur(b, n), 0)),
                  tile(KV_TILE, cur), tile(KV_TILE, prev)]
                 + [tile(GATE_TILE0 + t, cur) for t in range(N_Q_TILES)],
        out_specs=pl.BlockSpec((N_KV_HEADS, BLOCK, GROUP_W), lambda b, n: (0, cur(b, n), 0)),
        compiler_params=_params(2),
        name="band_attention",
    )(sinks * LOG2E, _alibi_table(), *([proj] * (3 + N_Q_TILES)))


def _attn_out_kernel(x_ref, w_ref, r_ref, o_ref, wb_ref):
    _cast_weight_once(w_ref, wb_ref)
    acc = r_ref[...]
    for h in range(N_KV_HEADS):
        acc = acc + jnp.dot(x_ref[h], wb_ref[h], preferred_element_type=F32)
    o_ref[...] = acc


def _attn_out_proj(og, w, resid, tm=1024, tn=512):
    _, m, _ = og.shape
    n = w.shape[2]
    return pl.pallas_call(
        _attn_out_kernel,
        out_shape=jax.ShapeDtypeStruct((m, n), F32),
        grid=(n // tn, m // tm),
        in_specs=[pl.BlockSpec((N_KV_HEADS, tm, GROUP_W), lambda j, i: (0, i, 0)),
                  pl.BlockSpec((N_KV_HEADS, GROUP_W, tn), lambda j, i: (0, 0, j)),
                  pl.BlockSpec((tm, tn), lambda j, i: (i, j))],
        out_specs=pl.BlockSpec((tm, tn), lambda j, i: (i, j)),
        scratch_shapes=[pltpu.VMEM((N_KV_HEADS, GROUP_W, tn), BF16)],
        compiler_params=_params(2),
        name="attn_out_proj",
    )(og, w, resid)


def _ssm_in_kernel(x_ref, w_ref, o_ref, wb_ref):
    jj, i = pl.program_id(0), pl.program_id(1)
    chunk = w_ref.shape[0]
    fill = jj % 2
    rows = pl.ds(pl.multiple_of(i * chunk, chunk), chunk)

    @pl.when(jj == 0)
    def _():
        wb_ref[fill, rows, :] = w_ref[...].astype(BF16)

    @pl.when(jj > 0)
    def _():
        wb_ref[fill, rows, :] = w_ref[...].astype(BF16)
        o_ref[...] = jnp.dot(x_ref[...], wb_ref[1 - fill], preferred_element_type=F32).astype(o_ref.dtype)


def _ssm_in_proj(x, w, tm=1024, tn=1024):
    m, k = x.shape
    n = w.shape[1]
    ni, nj = m // tm, n // tn
    chunk = k // ni
    row_tile = lambda jj, i: jnp.where(jj == 0, 0, i)
    return pl.pallas_call(
        _ssm_in_kernel,
        out_shape=jax.ShapeDtypeStruct((m, n), BF16),
        grid=(nj + 1, ni),
        in_specs=[pl.BlockSpec((tm, k), lambda jj, i: (row_tile(jj, i), 0)),
                  pl.BlockSpec((chunk, tn), lambda jj, i: (i, jnp.minimum(jj, nj - 1)))],
        out_specs=pl.BlockSpec((tm, tn), lambda jj, i: (row_tile(jj, i), jnp.maximum(jj - 1, 0))),
        scratch_shapes=[pltpu.VMEM((2, k, tn), BF16)],
        compiler_params=_params(2),
        name="ssm_in_proj",
    )(x, w)


def _glu_kernel(x_ref, w_ref, y_ref, gate_ref, o_ref):
    gt = gate_ref[...]
    gated = y_ref[...] * (gt * _sigmoid(gt))
    z = jnp.dot(x_ref[...], w_ref[...], preferred_element_type=F32)
    o_ref[...] = gated * _sigmoid(z.astype(BF16))


def _glu(y, w, ug, tm=1024, tn=512):
    m, k = y.shape
    n = w.shape[1]
    gate_off = n // tn
    return pl.pallas_call(
        _glu_kernel,
        out_shape=jax.ShapeDtypeStruct((m, n), BF16),
        grid=(m // tm, n // tn),
        in_specs=[pl.BlockSpec((tm, k), lambda i, j: (i, 0)),
                  pl.BlockSpec((k, tn), lambda i, j: (0, j)),
                  pl.BlockSpec((tm, tn), lambda i, j: (i, j)),
                  pl.BlockSpec((tm, tn), lambda i, j: (i, j + gate_off))],
        out_specs=pl.BlockSpec((tm, tn), lambda i, j: (i, j)),
        compiler_params=_params(2),
        name="ssm_glu",
    )(y, w, y, ug)


def _ssm_out_kernel(x_ref, w_ref, r_ref, o_ref, cols_ref):
    nj = x_ref.shape[1]
    x = x_ref[...].reshape(CHUNK * nj, x_ref.shape[2])
    acc = jnp.dot(x, w_ref[...], preferred_element_type=F32)
    for c in range(acc.shape[1] // LANES):
        for s in range(CHUNK):
            cols_ref[c, pl.ds(s, nj, stride=CHUNK), :] = acc[
                s * nj:(s + 1) * nj, c * LANES:(c + 1) * LANES]
        o_ref[:, c * LANES:(c + 1) * LANES] = cols_ref[c] + r_ref[:, c * LANES:(c + 1) * LANES]


def _ssm_out_proj(x, w, resid, tm=1024, tn=512):
    m, k = x.shape
    n = w.shape[1]
    nj = tm // CHUNK
    return pl.pallas_call(
        _ssm_out_kernel,
        out_shape=jax.ShapeDtypeStruct((m, n), F32),
        grid=(m // tm, n // tn),
        in_specs=[pl.BlockSpec((CHUNK, nj, k), lambda i, j: (0, i, 0)),
                  pl.BlockSpec((k, tn), lambda i, j: (0, j)),
                  pl.BlockSpec((tm, tn), lambda i, j: (i, j))],
        out_specs=pl.BlockSpec((tm, tn), lambda i, j: (i, j)),
        scratch_shapes=[pltpu.VMEM((tn // LANES, tm, LANES), F32)],
        compiler_params=_params(2),
        name="ssm_out_proj",
    )(x.reshape(CHUNK, m // CHUNK, k), w, resid)


N_SCAN_STEPS = 8
SCAN_BLOCK = 8
N_LOCAL_STEPS = 3


def _ssm_prep_kernel(lr_ref, li_ref, ldt_ref, btr_ref, bti_ref, cr_ref, ci_ref,
                     m_ref, be_ref, cet_ref, apr_ref, api_ref, bpr_ref, bpi_ref):
    tau = lax.broadcasted_iota(jnp.int32, (24, 2 * STATE_DIM), 0).astype(F32)
    quarter = jnp.where(lax.broadcasted_iota(jnp.int32, (24, 2 * STATE_DIM), 1) < STATE_DIM, 0.0, 0.5 * jnp.pi)
    lane = lax.broadcasted_iota(jnp.int32, (GROUP_SIZE, CHUNK_W), 1)

    def one_group(g, carry):
        lr = lr_ref[g]
        li = li_ref[g]
        dt = jnp.exp(ldt_ref[g])
        lr2 = jnp.concatenate([lr, lr], axis=1)
        li2 = jnp.concatenate([li, li], axis=1)
        dt2 = jnp.concatenate([dt, dt], axis=1)
        powers = jnp.exp(tau * (lr2 * dt2)) * jnp.cos(tau * (li2 * dt2) - quarter)
        pr = powers[:, :STATE_DIM]
        pi = powers[:, STATE_DIM:]
        ar, ai = pr[1:2], pi[1:2]
        den = lr * lr + li * li
        xr = ar - 1.0
        wr = (xr * lr + ai * li) / den
        wi = (ai * lr - xr * li) / den
        btr, bti = btr_ref[g], bti_ref[g]
        br = btr * wr - bti * wi
        bi = btr * wi + bti * wr
        cr, ci = cr_ref[g], ci_ref[g]

        ca_r, ca_i = [], []
        for t in range(CHUNK + 1):
            ca_r.append(cr * pr[t:t + 1] - ci * pi[t:t + 1])
            ca_i.append(cr * pi[t:t + 1] + ci * pr[t:t + 1])

        rt = jnp.concatenate([jnp.concatenate(ca_r[:CHUNK], axis=0),
                              jnp.concatenate(ca_i[:CHUNK], axis=0)], axis=1)
        bcat = jnp.concatenate([br, -bi], axis=1)
        kp = lax.dot_general(bcat.astype(BF16), rt.astype(BF16), NT,
                             preferred_element_type=F32)
        for s in range(CHUNK):
            blk = kp if s == 0 else pltpu.roll(kp, GROUP_SIZE * s, 1)
            blk = jnp.where(lane >= GROUP_SIZE * s, blk, 0.0)
            blk = jnp.concatenate([pltpu.roll(blk[:, :LANES], GROUP_SIZE * g, 1),
                                   pltpu.roll(blk[:, LANES:], GROUP_SIZE * g, 1)], axis=1)
            dst = pl.ds(pl.multiple_of(_step_slot(s, g) * GROUP_SIZE, GROUP_SIZE), GROUP_SIZE)
            m_ref[g, dst, :] = blk.astype(m_ref.dtype)
            e = CHUNK - 1 - s
            be_re = br * pr[e:e + 1] - bi * pi[e:e + 1]
            be_im = br * pi[e:e + 1] + bi * pr[e:e + 1]
            be_ref[g, dst, :] = jnp.concatenate(
                [be_re, be_im], axis=1).astype(be_ref.dtype)
            cet_ref[g, dst, :] = jnp.concatenate(
                [ca_r[s + 1], -ca_i[s + 1]], axis=1).astype(cet_ref.dtype)

        sq_r, sq_i = [pr[CHUNK:CHUNK + 1]], [pi[CHUNK:CHUNK + 1]]
        for _ in range(N_SCAN_STEPS - 1):
            xr, xi = sq_r[-1], sq_i[-1]
            sq_r.append(xr * xr - xi * xi)
            sq_i.append(2.0 * (xr * xi))
        mul_r, mul_i = [jnp.ones_like(sq_r[0]), sq_r[0]], [jnp.zeros_like(sq_r[0]), sq_i[0]]
        for _ in range(SCAN_BLOCK - 2):
            xr, xi = mul_r[-1], mul_i[-1]
            mul_r.append(xr * sq_r[0] - xi * sq_i[0])
            mul_i.append(xr * sq_i[0] + xi * sq_r[0])
        sr, si = jnp.concatenate(sq_r, axis=0), jnp.concatenate(sq_i, axis=0)
        rr, ri = jnp.concatenate(mul_r, axis=0), jnp.concatenate(mul_i, axis=0)
        apr_ref[g] = jnp.concatenate([sr, sr], axis=1)
        api_ref[g] = jnp.concatenate([-si, si], axis=1)
        bpr_ref[g] = jnp.concatenate([rr, rr], axis=1)
        bpi_ref[g] = jnp.concatenate([-ri, ri], axis=1)
        return carry

    lax.fori_loop(0, GROUPS_PER_BLOCK, one_group, 0, unroll=4)


def _ssm_prep(lam_re, lam_im, log_dt, b_re, b_im, c_re, c_im):
    g = N_GROUPS
    gb = GROUPS_PER_BLOCK
    row = lambda a: a.reshape(g, 1, STATE_DIM)
    ldt = jnp.broadcast_to(log_dt.reshape(g, 1, 1), (g, 1, STATE_DIM))
    btr = jnp.swapaxes(b_re, 1, 2)
    bti = jnp.swapaxes(b_im, 1, 2)
    vec = pl.BlockSpec((gb, 1, STATE_DIM), lambda i: (i, 0, 0))
    mat = pl.BlockSpec((gb, GROUP_SIZE, STATE_DIM), lambda i: (i, 0, 0))
    return pl.pallas_call(
        _ssm_prep_kernel,
        out_shape=[jax.ShapeDtypeStruct((g, CHUNK_W, CHUNK_W), BF16),
                   jax.ShapeDtypeStruct((g, CHUNK_W, 2 * STATE_DIM), BF16),
                   jax.ShapeDtypeStruct((g, CHUNK_W, 2 * STATE_DIM), BF16),
                   jax.ShapeDtypeStruct((g, N_SCAN_STEPS, 2 * STATE_DIM), F32),
                   jax.ShapeDtypeStruct((g, N_SCAN_STEPS, 2 * STATE_DIM), F32),
                   jax.ShapeDtypeStruct((g, SCAN_BLOCK, 2 * STATE_DIM), F32),
                   jax.ShapeDtypeStruct((g, SCAN_BLOCK, 2 * STATE_DIM), F32)],
        grid=(g // gb,),
        in_specs=[vec, vec, vec, mat, mat, mat, mat],
        out_specs=[pl.BlockSpec((gb, CHUNK_W, CHUNK_W), lambda i: (i, 0, 0)),
                   pl.BlockSpec((gb, CHUNK_W, 2 * STATE_DIM), lambda i: (i, 0, 0)),
                   pl.BlockSpec((gb, CHUNK_W, 2 * STATE_DIM), lambda i: (i, 0, 0)),
                   pl.BlockSpec((gb, N_SCAN_STEPS, 2 * STATE_DIM), lambda i: (i, 0, 0)),
                   pl.BlockSpec((gb, N_SCAN_STEPS, 2 * STATE_DIM), lambda i: (i, 0, 0)),
                   pl.BlockSpec((gb, SCAN_BLOCK, 2 * STATE_DIM), lambda i: (i, 0, 0)),
                   pl.BlockSpec((gb, SCAN_BLOCK, 2 * STATE_DIM), lambda i: (i, 0, 0))],
        compiler_params=_params(1),
        name="ssm_prep",
    )(row(lam_re), row(lam_im), ldt, btr, bti, c_re, c_im)


XPOSE_ROWS = 256


def _step_slot(step, group):
    half, k = divmod(step, GROUPS_PER_BLOCK)
    return half * GROUPS_PER_BLOCK + (k + group) % GROUPS_PER_BLOCK


def _slot_masks(shape):
    slot = lax.broadcasted_iota(jnp.int32, shape, 1) // GROUP_SIZE
    return [slot == p for p in range(GROUPS_PER_BLOCK)]


def _merge_slots(pieces, masks):
    out = pieces[0]
    for p in range(1, GROUPS_PER_BLOCK):
        out = jnp.where(masks[p], pieces[p], out)
    return out


def _steps_to_groups(xs):
    masks = _slot_masks(xs[0].shape)
    n = GROUPS_PER_BLOCK
    rot = [xs[k] if k == 0 else _lane_roll(xs[k], GROUP_SIZE * k) for k in range(n)]
    return [_merge_slots([rot[(p - g) % n] for p in range(n)], masks) for g in range(n)]


def _groups_to_steps(ys):
    masks = _slot_masks(ys[0].shape)
    n = GROUPS_PER_BLOCK
    out = []
    for k in range(n):
        z = _merge_slots([ys[(p - k) % n] for p in range(n)], masks)
        out.append(z if k == 0 else _lane_roll(z, LANES - GROUP_SIZE * k))
    return out


def _cmul(x, mr, mi):
    return x * mr + pltpu.roll(x, STATE_DIM, 1) * mi


def _ssm_kernel(x_ref, m_ref, be_ref, cet_ref, apr_ref, api_ref, bpr_ref, bpi_ref, d_ref, o_ref,
                uf_ref, yf_ref, xs_ref, cs_ref, *, batch, n_chunks):
    rows = batch * n_chunks
    n_blocks = n_chunks // SCAN_BLOCK

    def gather(i, carry):
        r0 = pl.multiple_of(i * XPOSE_ROWS, XPOSE_ROWS)
        for half in range(2):
            vs = [x_ref[GROUPS_PER_BLOCK * half + k, pl.ds(r0, XPOSE_ROWS), :]
                  for k in range(GROUPS_PER_BLOCK)]
            ts = _steps_to_groups(vs)
            for g in range(GROUPS_PER_BLOCK):
                uf_ref[g, pl.ds(r0, XPOSE_ROWS), half * LANES:(half + 1) * LANES] = ts[g]
        return carry

    lax.fori_loop(0, rows // XPOSE_ROWS, gather, 0)

    n_all = rows // SCAN_BLOCK
    row = lax.broadcasted_iota(jnp.int32, (rows, 2 * STATE_DIM), 0)
    in_block = row & (SCAN_BLOCK - 1)
    brow = lax.broadcasted_iota(jnp.int32, (n_all, 2 * STATE_DIM), 0)
    in_seq = brow & (n_blocks - 1)

    def shift_rows(x, d, pos):
        return jnp.where(pos >= d, pltpu.roll(x, d, 0), 0.0)

    def one_group(g, carry):
        u = uf_ref[g]
        y = jnp.dot(u, m_ref[g], preferred_element_type=F32)
        x = jnp.dot(u, be_ref[g], preferred_element_type=F32)
        apr, api = apr_ref[g], api_ref[g]
        for k in range(N_LOCAL_STEPS):
            x = x + _cmul(shift_rows(x, 1 << k, in_block), apr[k:k + 1], api[k:k + 1])
        xs_ref[...] = x
        c = xs_ref[pl.ds(SCAN_BLOCK - 1, n_all, stride=SCAN_BLOCK), :]
        for k in range(N_LOCAL_STEPS, N_SCAN_STEPS):
            c = c + _cmul(shift_rows(c, 1 << (k - N_LOCAL_STEPS), in_seq), apr[k:k + 1], api[k:k + 1])
        cs_ref[...] = shift_rows(c, 1, in_seq)
        bpr, bpi = bpr_ref[g], bpi_ref[g]
        carried = [_cmul(jnp.broadcast_to(cs_ref[mblk:mblk + 1, :], (SCAN_BLOCK, 2 * STATE_DIM)), bpr, bpi)
                   for mblk in range(n_all)]
        e = (shift_rows(x, 1, in_block) + jnp.concatenate(carried, axis=0)).astype(BF16)
        y = y + lax.dot_general(e, cet_ref[g], NT, preferred_element_type=F32)
        y = y + d_ref[g] * u.astype(F32)
        yf_ref[g] = jax.nn.gelu(y).astype(yf_ref.dtype)
        return carry

    lax.fori_loop(0, GROUPS_PER_BLOCK, one_group, 0, unroll=4)

    def scatter(i, carry):
        r0 = pl.multiple_of(i * XPOSE_ROWS, XPOSE_ROWS)
        for half in range(2):
            vs = [yf_ref[g, pl.ds(r0, XPOSE_ROWS), half * LANES:(half + 1) * LANES]
                  for g in range(GROUPS_PER_BLOCK)]
            ts = _groups_to_steps(vs)
            for k in range(GROUPS_PER_BLOCK):
                o_ref[GROUPS_PER_BLOCK * half + k, pl.ds(r0, XPOSE_ROWS), :] = ts[k]
        return carry

    lax.fori_loop(0, rows // XPOSE_ROWS, scatter, 0)


def _ssm(ug, m, be, cet, apr, api, bpr, bpi, dvec, batch):
    _, rows, _ = ug.shape
    n_chunks = rows // batch
    gb = GROUPS_PER_BLOCK
    col_blk = pl.BlockSpec((CHUNK, rows, LANES), lambda i: (0, 0, i))
    grp = lambda last2: pl.BlockSpec((gb,) + last2, lambda i: (i, 0, 0))
    return pl.pallas_call(
        functools.partial(_ssm_kernel, batch=batch, n_chunks=n_chunks),
        out_shape=jax.ShapeDtypeStruct((CHUNK, rows, SSM_WIDTH), BF16),
        grid=(N_GROUPS // gb,),
        in_specs=[col_blk, grp((CHUNK_W, CHUNK_W)), grp((CHUNK_W, 2 * STATE_DIM)),
                  grp((CHUNK_W, 2 * STATE_DIM)), grp((N_SCAN_STEPS, 2 * STATE_DIM)),
                  grp((N_SCAN_STEPS, 2 * STATE_DIM)), grp((SCAN_BLOCK, 2 * STATE_DIM)),
                  grp((SCAN_BLOCK, 2 * STATE_DIM)), grp((1, CHUNK_W))],
        out_specs=col_blk,
        scratch_shapes=[pltpu.VMEM((gb, rows, CHUNK_W), BF16),
                        pltpu.VMEM((gb, rows, CHUNK_W), BF16),
                        pltpu.VMEM((rows, 2 * STATE_DIM), F32),
                        pltpu.VMEM((rows // SCAN_BLOCK, 2 * STATE_DIM), F32)],
        compiler_params=_params(1),
        name="ssm_chunked",
    )(ug, m, be, cet, apr, api, bpr, bpi, dvec)


def kernel(x, norm_g, attn_w_in, attn_q_norm_g, attn_k_norm_g, attn_sinks, attn_w_out, ssm_w_in,
           ssm_log_dt, ssm_lam_re, ssm_lam_im, ssm_b_re, ssm_b_im, ssm_c_re, ssm_c_im, ssm_d,
           ssm_w_glu, ssm_w_out):
    batch, seq, d = x.shape
    m = batch * seq
    assert d == D_MODEL and seq % BLOCK == 0 and seq // CHUNK == 1 << N_SCAN_STEPS
    x2 = x.reshape(m, d)

    q_gain = jnp.tile(attn_q_norm_g[0].astype(F32) * (HEAD_DIM ** -0.5 * LOG2E), IN_TILE // HEAD_DIM)
    k_gain = jnp.tile(attn_k_norm_g[0].astype(F32), N_KV_HEADS)

    hn0 = _rmsnorm(x2, norm_g[0])
    proj = _attn_in_proj(hn0, attn_w_in[0], q_gain.reshape(1, IN_TILE), k_gain.reshape(1, KV_WIDTH))
    og = _attention(proj, attn_sinks[0].astype(F32), batch, seq)
    h1 = _attn_out_proj(og, attn_w_out[0].reshape(N_KV_HEADS, GROUP_W, D_MODEL), x2)

    nj = m // CHUNK
    hn1 = _rmsnorm_chunk_major(h1, norm_g[1])
    ug = _ssm_in_proj(hn1, ssm_w_in[0])
    mt, be, cet, apr, api, bpr, bpi = _ssm_prep(ssm_lam_re[0], ssm_lam_im[0], ssm_log_dt[0],
                                      ssm_b_re[0], ssm_b_im[0], ssm_c_re[0], ssm_c_im[0])
    dvec = jnp.tile(ssm_d[0].astype(F32).reshape(N_GROUPS, 1, GROUP_SIZE), (1, 1, CHUNK))
    y = _ssm(ug.reshape(CHUNK, nj, 2 * SSM_WIDTH), mt, be, cet, apr, api, bpr, bpi, dvec, batch)
    y = y.reshape(m, SSM_WIDTH)
    p = _glu(y, ssm_w_glu[0].astype(BF16), ug)
    out = _ssm_out_proj(p, ssm_w_out[0].astype(BF16), h1)
    return out.reshape(batch, seq, d)
```

```python
import functools

import jax
import jax.numpy as jnp
from jax import lax
from jax.experimental import pallas as pl
from jax.experimental.pallas import tpu as pltpu

F32 = jnp.float32
BF16 = jnp.bfloat16

D_MODEL = 4096
HEAD_DIM = 64
N_Q_HEADS = 64
N_KV_HEADS = 8
Q_PER_KV = 8
N_PAIRS = Q_PER_KV // 2
KV_WIDTH = N_KV_HEADS * HEAD_DIM
GROUP_W = Q_PER_KV * HEAD_DIM
BLOCK = 128
STACK = Q_PER_KV * BLOCK
SSM_WIDTH = 8192
GROUP_SIZE = 16
N_GROUPS = 512
STATE_DIM = 64
CHUNK = 16
CHUNK_W = CHUNK * GROUP_SIZE
RMS_EPS = 1e-6
NEG_INF = -1e30

LANES = 128
VMEM_LIMIT = 60 * 1024 * 1024
MXU_W = 256
GROUPS_PER_BLOCK = LANES // GROUP_SIZE

NT = (((1,), (1,)), ((), ()))


def _lane_roll(x, shift):
    return jnp.concatenate([x[:, LANES - shift:], x[:, :LANES - shift]], axis=1)


def _sigmoid(x):
    return 0.5 * jnp.tanh(0.5 * x) + 0.5


def _params(n_grid_dims):
    return pltpu.CompilerParams(
        dimension_semantics=("arbitrary",) * n_grid_dims, vmem_limit_bytes=VMEM_LIMIT)


def _rms(x, g):
    var = jnp.mean(x * x, axis=-1, keepdims=True)
    return x * lax.rsqrt(var + RMS_EPS) * g


def _rmsnorm_kernel(x_ref, g_ref, o_ref):
    o_ref[...] = _rms(x_ref[...], g_ref[...]).astype(o_ref.dtype)


def _rmsnorm(x, g, tm=256):
    m, d = x.shape
    return pl.pallas_call(
        _rmsnorm_kernel,
        out_shape=jax.ShapeDtypeStruct((m, d), BF16),
        grid=(m // tm,),
        in_specs=[pl.BlockSpec((tm, d), lambda i: (i, 0)),
                  pl.BlockSpec((1, d), lambda i: (0, 0))],
        out_specs=pl.BlockSpec((tm, d), lambda i: (i, 0)),
        compiler_params=_params(1),
        name="rmsnorm",
    )(x, g.reshape(1, d))


def _rmsnorm_chunk_major_kernel(x_ref, g_ref, o_ref, cols_ref):
    hn = _rms(x_ref[...], g_ref[...])
    n_cols = hn.shape[1] // LANES
    nj = hn.shape[0] // CHUNK
    for c in range(n_cols):
        cols_ref[c] = hn[:, c * LANES:(c + 1) * LANES]
    for s in range(CHUNK):
        for c in range(n_cols):
            o_ref[s, :, c * LANES:(c + 1) * LANES] = cols_ref[
                c, pl.ds(s, nj, stride=CHUNK), :].astype(o_ref.dtype)


def _rmsnorm_chunk_major(x, g, tm=256):
    m, d = x.shape
    nj = tm // CHUNK
    out = pl.pallas_call(
        _rmsnorm_chunk_major_kernel,
        out_shape=jax.ShapeDtypeStruct((CHUNK, m // CHUNK, d), BF16),
        grid=(m // tm,),
        in_specs=[pl.BlockSpec((tm, d), lambda i: (i, 0)),
                  pl.BlockSpec((1, d), lambda i: (0, 0))],
        out_specs=pl.BlockSpec((CHUNK, nj, d), lambda i: (0, i, 0)),
        scratch_shapes=[pltpu.VMEM((d // LANES, tm, LANES), F32)],
        compiler_params=_params(1),
        name="rmsnorm_chunk_major",
    )(x, g.reshape(1, d))
    return out.reshape(m, d)


def _head_rmsnorm(acc, ones_bd, gain):
    outs = []
    for c in range(acc.shape[1] // MXU_W):
        a = acc[:, c * MXU_W:(c + 1) * MXU_W]
        ss = jnp.dot((a * a).astype(BF16), ones_bd, preferred_element_type=F32)
        inv = lax.rsqrt(ss * (1.0 / HEAD_DIM) + RMS_EPS)
        outs.append(a * inv * gain[:, c * MXU_W:(c + 1) * MXU_W])
    return outs[0] if len(outs) == 1 else jnp.concatenate(outs, axis=1)


def _ones_block_diag():
    r = jnp.arange(MXU_W) // HEAD_DIM
    return (r[:, None] == r[None, :]).astype(BF16)


IN_TILE = 1024
N_Q_TILES = D_MODEL // IN_TILE
KV_TILE = N_Q_TILES
GATE_TILE0 = KV_TILE + 1
N_IN_TILES = (2 * D_MODEL + 2 * KV_WIDTH) // IN_TILE


def _attn_in_kernel(x_ref, w_ref, e_ref, qg_ref, kg_ref, o_ref, wb_ref):
    jj, i = pl.program_id(0), pl.program_id(1)
    tile = jj - 1
    chunk = w_ref.shape[0]
    fill = jj % 2
    rows = pl.ds(pl.multiple_of(i * chunk, chunk), chunk)

    def stage_next_tile():
        wb_ref[fill, rows, :] = w_ref[...].astype(BF16)

    def product():
        return jnp.dot(x_ref[...], wb_ref[1 - fill], preferred_element_type=F32)

    @pl.when(tile < 0)
    def _():
        stage_next_tile()

    @pl.when(jnp.logical_and(tile >= 0, tile < KV_TILE))
    def _():
        stage_next_tile()
        o_ref[...] = _head_rmsnorm(product(), e_ref[...], qg_ref[...]).astype(o_ref.dtype)

    @pl.when(tile == KV_TILE)
    def _():
        stage_next_tile()
        acc = product()
        kn = _head_rmsnorm(acc[:, :KV_WIDTH], e_ref[...], kg_ref[...])
        for h in range(N_KV_HEADS):
            o_ref[:, h * LANES:h * LANES + HEAD_DIM] = kn[:, h * HEAD_DIM:(h + 1) * HEAD_DIM].astype(o_ref.dtype)
            o_ref[:, h * LANES + HEAD_DIM:(h + 1) * LANES] = acc[
                :, KV_WIDTH + h * HEAD_DIM:KV_WIDTH + (h + 1) * HEAD_DIM].astype(o_ref.dtype)

    @pl.when(tile > KV_TILE)
    def _():
        stage_next_tile()
        o_ref[...] = product().astype(o_ref.dtype)


def _attn_in_proj(hn, w_in, q_gain, k_gain, tm=1024):
    m, k = hn.shape
    ni = m // tm
    chunk = k // ni
    row_tile = lambda jj, i: jnp.where(jj == 0, 0, i)
    const = lambda jj, i: (0, 0)
    return pl.pallas_call(
        _attn_in_kernel,
        out_shape=jax.ShapeDtypeStruct((N_IN_TILES, m, IN_TILE), BF16),
        grid=(N_IN_TILES + 1, ni),
        in_specs=[pl.BlockSpec((tm, k), lambda jj, i: (row_tile(jj, i), 0)),
                  pl.BlockSpec((chunk, IN_TILE), lambda jj, i: (i, jnp.minimum(jj, N_IN_TILES - 1))),
                  pl.BlockSpec((MXU_W, MXU_W), const),
                  pl.BlockSpec((1, IN_TILE), const),
                  pl.BlockSpec((1, KV_WIDTH), const)],
        out_specs=pl.BlockSpec((None, tm, IN_TILE),
                               lambda jj, i: (jnp.maximum(jj - 1, 0), row_tile(jj, i), 0)),
        scratch_shapes=[pltpu.VMEM((2, k, IN_TILE), BF16)],
        compiler_params=_params(2),
        name="attn_in_proj",
    )(hn, w_in, _ones_block_diag(), q_gain, k_gain)


def _attn_kernel(sink_ref, alibi_ref, q_ref, kvc_ref, kvp_ref, g0_ref, g1_ref, g2_ref, g3_ref, o_ref):
    gate_refs = (g0_ref, g1_ref, g2_ref, g3_ref)
    row = lax.broadcasted_iota(jnp.int32, (BLOCK, BLOCK), 0)
    col = lax.broadcasted_iota(jnp.int32, (BLOCK, BLOCK), 1)
    lower = col <= row
    left = col < HEAD_DIM
    ones = jnp.ones((2 * BLOCK, LANES), BF16)

    def split_kv(kv):
        swapped = _lane_roll(kv, HEAD_DIM)
        zero = jnp.zeros_like(kv)
        return jnp.where(left, kv, zero), jnp.where(left, zero, swapped), jnp.where(left, swapped, kv)

    for kvh in range(N_KV_HEADS):
        tile, col0 = kvh // 2, (kvh % 2) * GROUP_W
        kc_even, kc_odd, vvc = split_kv(kvc_ref[:, kvh * LANES:(kvh + 1) * LANES])
        kp_even, kp_odd, vvp = split_kv(kvp_ref[:, kvh * LANES:(kvh + 1) * LANES])
        qp = jnp.concatenate(
            [q_ref[tile, :, col0 + a * LANES:col0 + (a + 1) * LANES] for a in range(N_PAIRS)], axis=0)
        keys = jnp.concatenate([kc_even, kc_odd, kp_even, kp_odd], axis=0)
        s_all = lax.dot_general(qp, keys, NT, preferred_element_type=F32)
        v_ones = jnp.concatenate([jnp.concatenate([vvc, vvp], axis=0), ones], axis=1)
        ps, sink_terms = [], []
        for par in range(2):
            for a in range(N_PAIRS):
                rows = slice(a * BLOCK, (a + 1) * BLOCK)
                sc = s_all[rows, par * LANES:(par + 1) * LANES]
                sp = s_all[rows, (2 + par) * LANES:(3 + par) * LANES]
                blk = par * N_PAIRS + a
                s = jnp.where(lower, sc, sp) + alibi_ref[kvh, blk * BLOCK:(blk + 1) * BLOCK, :]
                sink = sink_ref[kvh * Q_PER_KV + 2 * a + par]
                m = jnp.maximum(jnp.max(s, axis=-1, keepdims=True), sink)
                p = jnp.exp2(s - m)
                ps.append(jnp.concatenate([jnp.where(lower, p, 0.0), jnp.where(lower, 0.0, p)],
                                          axis=1).astype(BF16))
                sink_terms.append(jnp.exp2(sink - m))
        nd = jnp.dot(jnp.concatenate(ps, axis=0), v_ones, preferred_element_type=F32)
        for a in range(N_PAIRS):
            o = []
            for par in range(2):
                blk = par * N_PAIRS + a
                r = nd[blk * BLOCK:(blk + 1) * BLOCK]
                o.append(r[:, :LANES] / (r[:, LANES:] + sink_terms[blk]))
            gt = gate_refs[tile][:, col0 + a * LANES:col0 + (a + 1) * LANES].astype(F32)
            o_ref[kvh, :, a * LANES:(a + 1) * LANES] = (
                jnp.where(left, o[0], o[1]) * (gt * _sigmoid(gt))).astype(o_ref.dtype)


LOG2E = 1.4426950408889634


def _alibi_table():
    qi = jnp.arange(BLOCK)[:, None]
    kj = jnp.arange(BLOCK)[None, :]
    dist = jnp.where(kj <= qi, qi - kj, BLOCK + qi - kj).astype(F32)
    slopes = jnp.exp2(-8.0 * jnp.arange(1, N_Q_HEADS + 1, dtype=F32) / N_Q_HEADS)
    slopes = slopes.reshape(N_KV_HEADS, N_PAIRS, 2).transpose(0, 2, 1)
    bias = (-LOG2E * slopes[:, :, :, None, None] * dist)
    first = jnp.where(kj <= qi, bias, NEG_INF)
    return jnp.stack([first, bias]).reshape(2, N_KV_HEADS, STACK, BLOCK)


def _attention(proj, sinks, batch, seq):
    nb = seq // BLOCK
    m = batch * seq
    cur = lambda b, n: b * nb + n
    prev = lambda b, n: b * nb + jnp.maximum(n - 1, 0)
    tile = lambda t, rows: pl.BlockSpec((None, BLOCK, IN_TILE), lambda b, n: (t, rows(b, n), 0))
    return pl.pallas_call(
        _attn_kernel,
        out_shape=jax.ShapeDtypeStruct((N_KV_HEADS, m, GROUP_W), BF16),
        grid=(batch, nb),
        in_specs=[pl.BlockSpec(memory_space=pltpu.SMEM),
                  pl.BlockSpec((None, N_KV_HEADS, STACK, BLOCK), lambda b, n: (jnp.minimum(n, 1), 0, 0, 0)),
                  pl.BlockSpec((N_Q_TILES, BLOCK, IN_TILE), lambda b, n: (0, cur(b, n), 0)),
                  tile(KV_TILE, cur), tile(KV_TILE, prev)]
                 + [tile(GATE_TILE0 + t, cur) for t in range(N_Q_TILES)],
        out_specs=pl.BlockSpec((N_KV_HEADS, BLOCK, GROUP_W), lambda b, n: (0, cur(b, n), 0)),
        compiler_params=_params(2),
        name="band_attention",
    )(sinks * LOG2E, _alibi_table(), *([proj] * (3 + N_Q_TILES)))


def _attn_out_kernel(x_ref, w_ref, r_ref, o_ref, wb_ref):
    jj, i = pl.program_id(0), pl.program_id(1)
    fill = jj % 2

    @pl.when(jj == 0)
    def _():
        wb_ref[fill, i] = w_ref[...].astype(BF16)

    @pl.when(jj > 0)
    def _():
        wb_ref[fill, i] = w_ref[...].astype(BF16)
        acc = r_ref[...]
        for h in range(N_KV_HEADS):
            acc = acc + jnp.dot(x_ref[h], wb_ref[1 - fill, h], preferred_element_type=F32)
        o_ref[...] = acc


def _attn_out_proj(og, w, resid, tm=1024, tn=1024):
    _, m, _ = og.shape
    n = w.shape[2]
    ni, nj = m // tm, n // tn
    assert ni == N_KV_HEADS
    row_tile = lambda jj, i: jnp.where(jj == 0, 0, i)
    out_blk = pl.BlockSpec((tm, tn), lambda jj, i: (row_tile(jj, i), jnp.maximum(jj - 1, 0)))
    return pl.pallas_call(
        _attn_out_kernel,
        out_shape=jax.ShapeDtypeStruct((m, n), F32),
        grid=(nj + 1, ni),
        in_specs=[pl.BlockSpec((N_KV_HEADS, tm, GROUP_W), lambda jj, i: (0, row_tile(jj, i), 0)),
                  pl.BlockSpec((None, GROUP_W, tn), lambda jj, i: (i, 0, jnp.minimum(jj, nj - 1))),
                  out_blk],
        out_specs=out_blk,
        scratch_shapes=[pltpu.VMEM((2, N_KV_HEADS, GROUP_W, tn), BF16)],
        compiler_params=_params(2),
        name="attn_out_proj",
    )(og, w, resid)


def _ssm_in_kernel(x_ref, w_ref, o_ref, wb_ref):
    jj, i = pl.program_id(0), pl.program_id(1)
    chunk = w_ref.shape[0]
    fill = jj % 2
    rows = pl.ds(pl.multiple_of(i * chunk, chunk), chunk)

    @pl.when(jj == 0)
    def _():
        wb_ref[fill, rows, :] = w_ref[...].astype(BF16)

    @pl.when(jj > 0)
    def _():
        wb_ref[fill, rows, :] = w_ref[...].astype(BF16)
        o_ref[...] = jnp.dot(x_ref[...], wb_ref[1 - fill], preferred_element_type=F32).astype(o_ref.dtype)


def _ssm_in_proj(x, w, tm=1024, tn=1024):
    m, k = x.shape
    n = w.shape[1]
    ni, nj = m // tm, n // tn
    chunk = k // ni
    row_tile = lambda jj, i: jnp.where(jj == 0, 0, i)
    return pl.pallas_call(
        _ssm_in_kernel,
        out_shape=jax.ShapeDtypeStruct((m, n), BF16),
        grid=(nj + 1, ni),
        in_specs=[pl.BlockSpec((tm, k), lambda jj, i: (row_tile(jj, i), 0)),
                  pl.BlockSpec((chunk, tn), lambda jj, i: (i, jnp.minimum(jj, nj - 1)))],
        out_specs=pl.BlockSpec((tm, tn), lambda jj, i: (row_tile(jj, i), jnp.maximum(jj - 1, 0))),
        scratch_shapes=[pltpu.VMEM((2, k, tn), BF16)],
        compiler_params=_params(2),
        name="ssm_in_proj",
    )(x, w)


def _glu_kernel(x_ref, w_ref, y_ref, gate_ref, o_ref):
    gt = gate_ref[...]
    gated = y_ref[...] * (gt * _sigmoid(gt))
    z = jnp.dot(x_ref[...], w_ref[...], preferred_element_type=F32)
    o_ref[...] = gated * _sigmoid(z.astype(BF16))


def _glu(y, w, ug, tm=1024, tn=512):
    m, k = y.shape
    n = w.shape[1]
    gate_off = n // tn
    return pl.pallas_call(
        _glu_kernel,
        out_shape=jax.ShapeDtypeStruct((m, n), BF16),
        grid=(m // tm, n // tn),
        in_specs=[pl.BlockSpec((tm, k), lambda i, j: (i, 0)),
                  pl.BlockSpec((k, tn), lambda i, j: (0, j)),
                  pl.BlockSpec((tm, tn), lambda i, j: (i, j)),
                  pl.BlockSpec((tm, tn), lambda i, j: (i, j + gate_off))],
        out_specs=pl.BlockSpec((tm, tn), lambda i, j: (i, j)),
        compiler_params=_params(2),
        name="ssm_glu",
    )(y, w, y, ug)


def _ssm_out_kernel(x_ref, w_ref, r_ref, o_ref, cols_ref):
    nj = x_ref.shape[1]
    x = x_ref[...].reshape(CHUNK * nj, x_ref.shape[2])
    acc = jnp.dot(x, w_ref[...], preferred_element_type=F32)
    for c in range(acc.shape[1] // LANES):
        for s in range(CHUNK):
            cols_ref[c, pl.ds(s, nj, stride=CHUNK), :] = acc[
                s * nj:(s + 1) * nj, c * LANES:(c + 1) * LANES]
        o_ref[:, c * LANES:(c + 1) * LANES] = cols_ref[c] + r_ref[:, c * LANES:(c + 1) * LANES]


def _ssm_out_proj(x, w, resid, tm=1024, tn=512):
    m, k = x.shape
    n = w.shape[1]
    nj = tm // CHUNK
    return pl.pallas_call(
        _ssm_out_kernel,
        out_shape=jax.ShapeDtypeStruct((m, n), F32),
        grid=(m // tm, n // tn),
        in_specs=[pl.BlockSpec((CHUNK, nj, k), lambda i, j: (0, i, 0)),
                  pl.BlockSpec((k, tn), lambda i, j: (0, j)),
                  pl.BlockSpec((tm, tn), lambda i, j: (i, j))],
        out_specs=pl.BlockSpec((tm, tn), lambda i, j: (i, j)),
        scratch_shapes=[pltpu.VMEM((tn // LANES, tm, LANES), F32)],
        compiler_params=_params(2),
        name="ssm_out_proj",
    )(x.reshape(CHUNK, m // CHUNK, k), w, resid)


N_SCAN_STEPS = 8
SCAN_BLOCK = 8
N_LOCAL_STEPS = 3


def _ssm_prep_kernel(lr_ref, li_ref, ldt_ref, btr_ref, bti_ref, cr_ref, ci_ref,
                     m_ref, be_ref, cet_ref, apr_ref, api_ref, bpr_ref, bpi_ref):
    tau = lax.broadcasted_iota(jnp.int32, (24, 2 * STATE_DIM), 0).astype(F32)
    quarter = jnp.where(lax.broadcasted_iota(jnp.int32, (24, 2 * STATE_DIM), 1) < STATE_DIM, 0.0, 0.5 * jnp.pi)
    lane = lax.broadcasted_iota(jnp.int32, (GROUP_SIZE, CHUNK_W), 1)

    def one_group(g, carry):
        lr = lr_ref[g]
        li = li_ref[g]
        dt = jnp.exp(ldt_ref[g])
        lr2 = jnp.concatenate([lr, lr], axis=1)
        li2 = jnp.concatenate([li, li], axis=1)
        dt2 = jnp.concatenate([dt, dt], axis=1)
        powers = jnp.exp(tau * (lr2 * dt2)) * jnp.cos(tau * (li2 * dt2) - quarter)
        pr = powers[:, :STATE_DIM]
        pi = powers[:, STATE_DIM:]
        ar, ai = pr[1:2], pi[1:2]
        den = lr * lr + li * li
        xr = ar - 1.0
        wr = (xr * lr + ai * li) / den
        wi = (ai * lr - xr * li) / den
        btr, bti = btr_ref[g], bti_ref[g]
        br = btr * wr - bti * wi
        bi = btr * wi + bti * wr
        cr, ci = cr_ref[g], ci_ref[g]

        ca_r, ca_i = [], []
        for t in range(CHUNK + 1):
            ca_r.append(cr * pr[t:t + 1] - ci * pi[t:t + 1])
            ca_i.append(cr * pi[t:t + 1] + ci * pr[t:t + 1])

        rt = jnp.concatenate([jnp.concatenate(ca_r[:CHUNK], axis=0),
                              jnp.concatenate(ca_i[:CHUNK], axis=0)], axis=1)
        bcat = jnp.concatenate([br, -bi], axis=1)
        kp = lax.dot_general(bcat.astype(BF16), rt.astype(BF16), NT,
                             preferred_element_type=F32)
        for s in range(CHUNK):
            blk = kp if s == 0 else pltpu.roll(kp, GROUP_SIZE * s, 1)
            blk = jnp.where(lane >= GROUP_SIZE * s, blk, 0.0)
            blk = jnp.concatenate([pltpu.roll(blk[:, :LANES], GROUP_SIZE * g, 1),
                                   pltpu.roll(blk[:, LANES:], GROUP_SIZE * g, 1)], axis=1)
            dst = pl.ds(pl.multiple_of(_step_slot(s, g) * GROUP_SIZE, GROUP_SIZE), GROUP_SIZE)
            m_ref[g, dst, :] = blk.astype(m_ref.dtype)
            e = CHUNK - 1 - s
            be_re = br * pr[e:e + 1] - bi * pi[e:e + 1]
            be_im = br * pi[e:e + 1] + bi * pr[e:e + 1]
            be_ref[g, dst, :] = jnp.concatenate(
                [be_re, be_im], axis=1).astype(be_ref.dtype)
            cet_ref[g, dst, :] = jnp.concatenate(
                [ca_r[s + 1], -ca_i[s + 1]], axis=1).astype(cet_ref.dtype)

        sq_r, sq_i = [pr[CHUNK:CHUNK + 1]], [pi[CHUNK:CHUNK + 1]]
        for _ in range(N_SCAN_STEPS - 1):
            xr, xi = sq_r[-1], sq_i[-1]
            sq_r.append(xr * xr - xi * xi)
            sq_i.append(2.0 * (xr * xi))
        mul_r, mul_i = [jnp.ones_like(sq_r[0]), sq_r[0]], [jnp.zeros_like(sq_r[0]), sq_i[0]]
        for _ in range(SCAN_BLOCK - 2):
            xr, xi = mul_r[-1], mul_i[-1]
            mul_r.append(xr * sq_r[0] - xi * sq_i[0])
            mul_i.append(xr * sq_i[0] + xi * sq_r[0])
        sr, si = jnp.concatenate(sq_r, axis=0), jnp.concatenate(sq_i, axis=0)
        rr, ri = jnp.concatenate(mul_r, axis=0), jnp.concatenate(mul_i, axis=0)
        apr_ref[g] = jnp.concatenate([sr, sr], axis=1)
        api_ref[g] = jnp.concatenate([-si, si], axis=1)
        bpr_ref[g] = jnp.concatenate([rr, rr], axis=1)
        bpi_ref[g] = jnp.concatenate([-ri, ri], axis=1)
        return carry

    lax.fori_loop(0, GROUPS_PER_BLOCK, one_group, 0, unroll=4)


def _ssm_prep(lam_re, lam_im, log_dt, b_re, b_im, c_re, c_im):
    g = N_GROUPS
    gb = GROUPS_PER_BLOCK
    row = lambda a: a.reshape(g, 1, STATE_DIM)
    ldt = jnp.broadcast_to(log_dt.reshape(g, 1, 1), (g, 1, STATE_DIM))
    btr = jnp.swapaxes(b_re, 1, 2)
    bti = jnp.swapaxes(b_im, 1, 2)
    vec = pl.BlockSpec((gb, 1, STATE_DIM), lambda i: (i, 0, 0))
    mat = pl.BlockSpec((gb, GROUP_SIZE, STATE_DIM), lambda i: (i, 0, 0))
    return pl.pallas_call(
        _ssm_prep_kernel,
        out_shape=[jax.ShapeDtypeStruct((g, CHUNK_W, CHUNK_W), BF16),
                   jax.ShapeDtypeStruct((g, CHUNK_W, 2 * STATE_DIM), BF16),
                   jax.ShapeDtypeStruct((g, CHUNK_W, 2 * STATE_DIM), BF16),
                   jax.ShapeDtypeStruct((g, N_SCAN_STEPS, 2 * STATE_DIM), F32),
                   jax.ShapeDtypeStruct((g, N_SCAN_STEPS, 2 * STATE_DIM), F32),
                   jax.ShapeDtypeStruct((g, SCAN_BLOCK, 2 * STATE_DIM), F32),
                   jax.ShapeDtypeStruct((g, SCAN_BLOCK, 2 * STATE_DIM), F32)],
        grid=(g // gb,),
        in_specs=[vec, vec, vec, mat, mat, mat, mat],
        out_specs=[pl.BlockSpec((gb, CHUNK_W, CHUNK_W), lambda i: (i, 0, 0)),
                   pl.BlockSpec((gb, CHUNK_W, 2 * STATE_DIM), lambda i: (i, 0, 0)),
                   pl.BlockSpec((gb, CHUNK_W, 2 * STATE_DIM), lambda i: (i, 0, 0)),
                   pl.BlockSpec((gb, N_SCAN_STEPS, 2 * STATE_DIM), lambda i: (i, 0, 0)),
                   pl.BlockSpec((gb, N_SCAN_STEPS, 2 * STATE_DIM), lambda i: (i, 0, 0)),
                   pl.BlockSpec((gb, SCAN_BLOCK, 2 * STATE_DIM), lambda i: (i, 0, 0)),
                   pl.BlockSpec((gb, SCAN_BLOCK, 2 * STATE_DIM), lambda i: (i, 0, 0))],
        compiler_params=_params(1),
        name="ssm_prep",
    )(row(lam_re), row(lam_im), ldt, btr, bti, c_re, c_im)


XPOSE_ROWS = 256


def _step_slot(step, group):
    half, k = divmod(step, GROUPS_PER_BLOCK)
    return half * GROUPS_PER_BLOCK + (k + group) % GROUPS_PER_BLOCK


def _slot_masks(shape):
    slot = lax.broadcasted_iota(jnp.int32, shape, 1) // GROUP_SIZE
    return [slot == p for p in range(GROUPS_PER_BLOCK)]


def _merge_slots(pieces, masks):
    out = pieces[0]
    for p in range(1, GROUPS_PER_BLOCK):
        out = jnp.where(masks[p], pieces[p], out)
    return out


def _steps_to_groups(xs):
    masks = _slot_masks(xs[0].shape)
    n = GROUPS_PER_BLOCK
    rot = [xs[k] if k == 0 else _lane_roll(xs[k], GROUP_SIZE * k) for k in range(n)]
    return [_merge_slots([rot[(p - g) % n] for p in range(n)], masks) for g in range(n)]


def _groups_to_steps(ys):
    masks = _slot_masks(ys[0].shape)
    n = GROUPS_PER_BLOCK
    out = []
    for k in range(n):
        z = _merge_slots([ys[(p - k) % n] for p in range(n)], masks)
        out.append(z if k == 0 else _lane_roll(z, LANES - GROUP_SIZE * k))
    return out


def _cmul(x, mr, mi):
    return x * mr + pltpu.roll(x, STATE_DIM, 1) * mi


def _ssm_kernel(x_ref, m_ref, be_ref, cet_ref, apr_ref, api_ref, bpr_ref, bpi_ref, d_ref, o_ref,
                uf_ref, yf_ref, xs_ref, cs_ref, *, batch, n_chunks):
    rows = batch * n_chunks
    n_blocks = n_chunks // SCAN_BLOCK

    def gather(i, carry):
        r0 = pl.multiple_of(i * XPOSE_ROWS, XPOSE_ROWS)
        for half in range(2):
            vs = [x_ref[GROUPS_PER_BLOCK * half + k, pl.ds(r0, XPOSE_ROWS), :]
                  for k in range(GROUPS_PER_BLOCK)]
            ts = _steps_to_groups(vs)
            for g in range(GROUPS_PER_BLOCK):
                uf_ref[g, pl.ds(r0, XPOSE_ROWS), half * LANES:(half + 1) * LANES] = ts[g]
        return carry

    lax.fori_loop(0, rows // XPOSE_ROWS, gather, 0)

    n_all = rows // SCAN_BLOCK
    row = lax.broadcasted_iota(jnp.int32, (rows, 2 * STATE_DIM), 0)
    in_block = row & (SCAN_BLOCK - 1)
    brow = lax.broadcasted_iota(jnp.int32, (n_all, 2 * STATE_DIM), 0)
    in_seq = brow & (n_blocks - 1)

    def shift_rows(x, d, pos):
        return jnp.where(pos >= d, pltpu.roll(x, d, 0), 0.0)

    def one_group(g, carry):
        u = uf_ref[g]
        y = jnp.dot(u, m_ref[g], preferred_element_type=F32)
        x = jnp.dot(u, be_ref[g], preferred_element_type=F32)
        apr, api = apr_ref[g], api_ref[g]
        for k in range(N_LOCAL_STEPS):
            x = x + _cmul(shift_rows(x, 1 << k, in_block), apr[k:k + 1], api[k:k + 1])
        xs_ref[...] = x
        c = xs_ref[pl.ds(SCAN_BLOCK - 1, n_all, stride=SCAN_BLOCK), :]
        for k in range(N_LOCAL_STEPS, N_SCAN_STEPS):
            c = c + _cmul(shift_rows(c, 1 << (k - N_LOCAL_STEPS), in_seq), apr[k:k + 1], api[k:k + 1])
        cs_ref[...] = shift_rows(c, 1, in_seq)
        bpr, bpi = bpr_ref[g], bpi_ref[g]
        carried = [_cmul(jnp.broadcast_to(cs_ref[mblk:mblk + 1, :], (SCAN_BLOCK, 2 * STATE_DIM)), bpr, bpi)
                   for mblk in range(n_all)]
        e = (shift_rows(x, 1, in_block) + jnp.concatenate(carried, axis=0)).astype(BF16)
        y = y + lax.dot_general(e, cet_ref[g], NT, preferred_element_type=F32)
        y = y + d_ref[g] * u.astype(F32)
        yf_ref[g] = jax.nn.gelu(y).astype(yf_ref.dtype)
        return carry

    lax.fori_loop(0, GROUPS_PER_BLOCK, one_group, 0, unroll=4)

    def scatter(i, carry):
        r0 = pl.multiple_of(i * XPOSE_ROWS, XPOSE_ROWS)
        for half in range(2):
            vs = [yf_ref[g, pl.ds(r0, XPOSE_ROWS), half * LANES:(half + 1) * LANES]
                  for g in range(GROUPS_PER_BLOCK)]
            ts = _groups_to_steps(vs)
            for k in range(GROUPS_PER_BLOCK):
                o_ref[GROUPS_PER_BLOCK * half + k, pl.ds(r0, XPOSE_ROWS), :] = ts[k]
        return carry

    lax.fori_loop(0, rows // XPOSE_ROWS, scatter, 0)


def _ssm(ug, m, be, cet, apr, api, bpr, bpi, dvec, batch):
    _, rows, _ = ug.shape
    n_chunks = rows // batch
    gb = GROUPS_PER_BLOCK
    col_blk = pl.BlockSpec((CHUNK, rows, LANES), lambda i: (0, 0, i))
    grp = lambda last2: pl.BlockSpec((gb,) + last2, lambda i: (i, 0, 0))
    return pl.pallas_call(
        functools.partial(_ssm_kernel, batch=batch, n_chunks=n_chunks),
        out_shape=jax.ShapeDtypeStruct((CHUNK, rows, SSM_WIDTH), BF16),
        grid=(N_GROUPS // gb,),
        in_specs=[col_blk, grp((CHUNK_W, CHUNK_W)), grp((CHUNK_W, 2 * STATE_DIM)),
                  grp((CHUNK_W, 2 * STATE_DIM)), grp((N_SCAN_STEPS, 2 * STATE_DIM)),
                  grp((N_SCAN_STEPS, 2 * STATE_DIM)), grp((SCAN_BLOCK, 2 * STATE_DIM)),
                  grp((SCAN_BLOCK, 2 * STATE_DIM)), grp((1, CHUNK_W))],
        out_specs=col_blk,
        scratch_shapes=[pltpu.VMEM((gb, rows, CHUNK_W), BF16),
                        pltpu.VMEM((gb, rows, CHUNK_W), BF16),
                        pltpu.VMEM((rows, 2 * STATE_DIM), F32),
                        pltpu.VMEM((rows // SCAN_BLOCK, 2 * STATE_DIM), F32)],
        compiler_params=_params(1),
        name="ssm_chunked",
    )(ug, m, be, cet, apr, api, bpr, bpi, dvec)


def kernel(x, norm_g, attn_w_in, attn_q_norm_g, attn_k_norm_g, attn_sinks, attn_w_out, ssm_w_in,
           ssm_log_dt, ssm_lam_re, ssm_lam_im, ssm_b_re, ssm_b_im, ssm_c_re, ssm_c_im, ssm_d,
           ssm_w_glu, ssm_w_out):
    batch, seq, d = x.shape
    m = batch * seq
    assert d == D_MODEL and seq % BLOCK == 0 and seq // CHUNK == 1 << N_SCAN_STEPS
    x2 = x.reshape(m, d)

    q_gain = jnp.tile(attn_q_norm_g[0].astype(F32) * (HEAD_DIM ** -0.5 * LOG2E), IN_TILE // HEAD_DIM)
    k_gain = jnp.tile(attn_k_norm_g[0].astype(F32), N_KV_HEADS)

    hn0 = _rmsnorm(x2, norm_g[0])
    proj = _attn_in_proj(hn0, attn_w_in[0], q_gain.reshape(1, IN_TILE), k_gain.reshape(1, KV_WIDTH))
    og = _attention(proj, attn_sinks[0].astype(F32), batch, seq)
    h1 = _attn_out_proj(og, attn_w_out[0].reshape(N_KV_HEADS, GROUP_W, D_MODEL), x2)

    nj = m // CHUNK
    hn1 = _rmsnorm_chunk_major(h1, norm_g[1])
    ug = _ssm_in_proj(hn1, ssm_w_in[0])
    mt, be, cet, apr, api, bpr, bpi = _ssm_prep(ssm_lam_re[0], ssm_lam_im[0], ssm_log_dt[0],
                                      ssm_b_re[0], ssm_b_im[0], ssm_c_re[0], ssm_c_im[0])
    dvec = jnp.tile(ssm_d[0].astype(F32).reshape(N_GROUPS, 1, GROUP_SIZE), (1, 1, CHUNK))
    y = _ssm(ug.reshape(CHUNK, nj, 2 * SSM_WIDTH), mt, be, cet, apr, api, bpr, bpi, dvec, batch)
    y = y.reshape(m, SSM_WIDTH)
    p = _glu(y, ssm_w_glu[0].astype(BF16), ug)
    out = _ssm_out_proj(p, ssm_w_out[0].astype(BF16), h1)
    return out.reshape(batch, seq, d)
```

```python
import functools

import jax
import jax.numpy as jnp
from jax import lax
from jax.experimental import pallas as pl
from jax.experimental.pallas import tpu as pltpu

F32 = jnp.float32
BF16 = jnp.bfloat16

D_MODEL = 4096
HEAD_DIM = 64
N_Q_HEADS = 64
N_KV_HEADS = 8
Q_PER_KV = 8
N_PAIRS = Q_PER_KV // 2
KV_WIDTH = N_KV_HEADS * HEAD_DIM
GROUP_W = Q_PER_KV * HEAD_DIM
BLOCK = 128
STACK = Q_PER_KV * BLOCK
SSM_WIDTH = 8192
GROUP_SIZE = 16
N_GROUPS = 512
STATE_DIM = 64
CHUNK = 16
CHUNK_W = CHUNK * GROUP_SIZE
RMS_EPS = 1e-6
NEG_INF = -1e30

LANES = 128
VMEM_LIMIT = 60 * 1024 * 1024
MXU_W = 256
GROUPS_PER_BLOCK = LANES // GROUP_SIZE

NT = (((1,), (1,)), ((), ()))


def _lane_roll(x, shift):
    return jnp.concatenate([x[:, LANES - shift:], x[:, :LANES - shift]], axis=1)


def _sigmoid(x):
    return 0.5 * jnp.tanh(0.5 * x) + 0.5


def _params(n_grid_dims):
    return pltpu.CompilerParams(
        dimension_semantics=("arbitrary",) * n_grid_dims, vmem_limit_bytes=VMEM_LIMIT)


def _rms(x, g):
    var = jnp.mean(x * x, axis=-1, keepdims=True)
    return x * lax.rsqrt(var + RMS_EPS) * g


def _rmsnorm_kernel(x_ref, g_ref, o_ref):
    o_ref[...] = _rms(x_ref[...], g_ref[...]).astype(o_ref.dtype)


def _rmsnorm(x, g, tm=256):
    m, d = x.shape
    return pl.pallas_call(
        _rmsnorm_kernel,
        out_shape=jax.ShapeDtypeStruct((m, d), BF16),
        grid=(m // tm,),
        in_specs=[pl.BlockSpec((tm, d), lambda i: (i, 0)),
                  pl.BlockSpec((1, d), lambda i: (0, 0))],
        out_specs=pl.BlockSpec((tm, d), lambda i: (i, 0)),
        compiler_params=_params(1),
        name="rmsnorm",
    )(x, g.reshape(1, d))


def _rmsnorm_chunk_major_kernel(x_ref, g_ref, o_ref, cols_ref):
    hn = _rms(x_ref[...], g_ref[...])
    n_cols = hn.shape[1] // LANES
    nj = hn.shape[0] // CHUNK
    for c in range(n_cols):
        cols_ref[c] = hn[:, c * LANES:(c + 1) * LANES]
    for s in range(CHUNK):
        for c in range(n_cols):
            o_ref[s, :, c * LANES:(c + 1) * LANES] = cols_ref[
                c, pl.ds(s, nj, stride=CHUNK), :].astype(o_ref.dtype)


def _rmsnorm_chunk_major(x, g, tm=256):
    m, d = x.shape
    nj = tm // CHUNK
    out = pl.pallas_call(
        _rmsnorm_chunk_major_kernel,
        out_shape=jax.ShapeDtypeStruct((CHUNK, m // CHUNK, d), BF16),
        grid=(m // tm,),
        in_specs=[pl.BlockSpec((tm, d), lambda i: (i, 0)),
                  pl.BlockSpec((1, d), lambda i: (0, 0))],
        out_specs=pl.BlockSpec((CHUNK, nj, d), lambda i: (0, i, 0)),
        scratch_shapes=[pltpu.VMEM((d // LANES, tm, LANES), F32)],
        compiler_params=_params(1),
        name="rmsnorm_chunk_major",
    )(x, g.reshape(1, d))
    return out.reshape(m, d)


def _head_rmsnorm(acc, ones_bd, gain):
    outs = []
    for c in range(acc.shape[1] // MXU_W):
        a = acc[:, c * MXU_W:(c + 1) * MXU_W]
        ss = jnp.dot((a * a).astype(BF16), ones_bd, preferred_element_type=F32)
        inv = lax.rsqrt(ss * (1.0 / HEAD_DIM) + RMS_EPS)
        outs.append(a * inv * gain[:, c * MXU_W:(c + 1) * MXU_W])
    return outs[0] if len(outs) == 1 else jnp.concatenate(outs, axis=1)


def _ones_block_diag():
    r = jnp.arange(MXU_W) // HEAD_DIM
    return (r[:, None] == r[None, :]).astype(BF16)


IN_TILE = 1024
N_Q_TILES = D_MODEL // IN_TILE
KV_TILE = N_Q_TILES
GATE_TILE0 = KV_TILE + 1
N_IN_TILES = (2 * D_MODEL + 2 * KV_WIDTH) // IN_TILE


def _attn_in_kernel(x_ref, w_ref, e_ref, qg_ref, kg_ref, o_ref, wb_ref):
    jj, i = pl.program_id(0), pl.program_id(1)
    tile = jj - 1
    chunk = w_ref.shape[0]
    fill = jj % 2
    rows = pl.ds(pl.multiple_of(i * chunk, chunk), chunk)

    def stage_next_tile():
        wb_ref[fill, rows, :] = w_ref[...].astype(BF16)

    def product():
        return jnp.dot(x_ref[...], wb_ref[1 - fill], preferred_element_type=F32)

    @pl.when(tile < 0)
    def _():
        stage_next_tile()

    @pl.when(jnp.logical_and(tile >= 0, tile < KV_TILE))
    def _():
        stage_next_tile()
        o_ref[...] = _head_rmsnorm(product(), e_ref[...], qg_ref[...]).astype(o_ref.dtype)

    @pl.when(tile == KV_TILE)
    def _():
        stage_next_tile()
        acc = product()
        kn = _head_rmsnorm(acc[:, :KV_WIDTH], e_ref[...], kg_ref[...])
        for h in range(N_KV_HEADS):
            o_ref[:, h * LANES:h * LANES + HEAD_DIM] = kn[:, h * HEAD_DIM:(h + 1) * HEAD_DIM].astype(o_ref.dtype)
            o_ref[:, h * LANES + HEAD_DIM:(h + 1) * LANES] = acc[
                :, KV_WIDTH + h * HEAD_DIM:KV_WIDTH + (h + 1) * HEAD_DIM].astype(o_ref.dtype)

    @pl.when(tile > KV_TILE)
    def _():
        stage_next_tile()
        o_ref[...] = product().astype(o_ref.dtype)


def _attn_in_proj(hn, w_in, q_gain, k_gain, tm=1024):
    m, k = hn.shape
    ni = m // tm
    chunk = k // ni
    row_tile = lambda jj, i: jnp.where(jj == 0, 0, i)
    const = lambda jj, i: (0, 0)
    return pl.pallas_call(
        _attn_in_kernel,
        out_shape=jax.ShapeDtypeStruct((N_IN_TILES, m, IN_TILE), BF16),
        grid=(N_IN_TILES + 1, ni),
        in_specs=[pl.BlockSpec((tm, k), lambda jj, i: (row_tile(jj, i), 0)),
                  pl.BlockSpec((chunk, IN_TILE), lambda jj, i: (i, jnp.minimum(jj, N_IN_TILES - 1))),
                  pl.BlockSpec((MXU_W, MXU_W), const),
                  pl.BlockSpec((1, IN_TILE), const),
                  pl.BlockSpec((1, KV_WIDTH), const)],
        out_specs=pl.BlockSpec((None, tm, IN_TILE),
                               lambda jj, i: (jnp.maximum(jj - 1, 0), row_tile(jj, i), 0)),
        scratch_shapes=[pltpu.VMEM((2, k, IN_TILE), BF16)],
        compiler_params=_params(2),
        name="attn_in_proj",
    )(hn, w_in, _ones_block_diag(), q_gain, k_gain)


def _attn_kernel(sink_ref, alibi_ref, q_ref, kvc_ref, kvp_ref, g0_ref, g1_ref, g2_ref, g3_ref, o_ref):
    gate_refs = (g0_ref, g1_ref, g2_ref, g3_ref)
    row = lax.broadcasted_iota(jnp.int32, (BLOCK, BLOCK), 0)
    col = lax.broadcasted_iota(jnp.int32, (BLOCK, BLOCK), 1)
    lower = col <= row
    left = col < HEAD_DIM
    ones = jnp.ones((2 * BLOCK, LANES), BF16)

    def split_kv(kv):
        swapped = _lane_roll(kv, HEAD_DIM)
        zero = jnp.zeros_like(kv)
        return jnp.where(left, kv, zero), jnp.where(left, zero, swapped), jnp.where(left, swapped, kv)

    for kvh in range(N_KV_HEADS):
        tile, col0 = kvh // 2, (kvh % 2) * GROUP_W
        kc_even, kc_odd, vvc = split_kv(kvc_ref[:, kvh * LANES:(kvh + 1) * LANES])
        kp_even, kp_odd, vvp = split_kv(kvp_ref[:, kvh * LANES:(kvh + 1) * LANES])
        qp = jnp.concatenate(
            [q_ref[tile, :, col0 + a * LANES:col0 + (a + 1) * LANES] for a in range(N_PAIRS)], axis=0)
        keys = jnp.concatenate([kc_even, kc_odd, kp_even, kp_odd], axis=0)
        s_all = lax.dot_general(qp, keys, NT, preferred_element_type=F32)
        v_ones = jnp.concatenate([jnp.concatenate([vvc, vvp], axis=0), ones], axis=1)
        ps, sink_terms = [], []
        for par in range(2):
            for a in range(N_PAIRS):
                rows = slice(a * BLOCK, (a + 1) * BLOCK)
                sc = s_all[rows, par * LANES:(par + 1) * LANES]
                sp = s_all[rows, (2 + par) * LANES:(3 + par) * LANES]
                blk = par * N_PAIRS + a
                s = jnp.where(lower, sc, sp) + alibi_ref[kvh, blk * BLOCK:(blk + 1) * BLOCK, :]
                sink = sink_ref[kvh * Q_PER_KV + 2 * a + par]
                m = jnp.maximum(jnp.max(s, axis=-1, keepdims=True), sink)
                p = jnp.exp2(s - m)
                ps.append(jnp.concatenate([jnp.where(lower, p, 0.0), jnp.where(lower, 0.0, p)],
                                          axis=1).astype(BF16))
                sink_terms.append(jnp.exp2(sink - m))
        nd = jnp.dot(jnp.concatenate(ps, axis=0), v_ones, preferred_element_type=F32)
        for a in range(N_PAIRS):
            o = []
            for par in range(2):
                blk = par * N_PAIRS + a
                r = nd[blk * BLOCK:(blk + 1) * BLOCK]
                o.append(r[:, :LANES] / (r[:, LANES:] + sink_terms[blk]))
            gt = gate_refs[tile][:, col0 + a * LANES:col0 + (a + 1) * LANES].astype(F32)
            o_ref[kvh, :, a * LANES:(a + 1) * LANES] = (
                jnp.where(left, o[0], o[1]) * (gt * _sigmoid(gt))).astype(o_ref.dtype)


LOG2E = 1.4426950408889634


def _alibi_table():
    qi = jnp.arange(BLOCK)[:, None]
    kj = jnp.arange(BLOCK)[None, :]
    dist = jnp.where(kj <= qi, qi - kj, BLOCK + qi - kj).astype(F32)
    slopes = jnp.exp2(-8.0 * jnp.arange(1, N_Q_HEADS + 1, dtype=F32) / N_Q_HEADS)
    slopes = slopes.reshape(N_KV_HEADS, N_PAIRS, 2).transpose(0, 2, 1)
    bias = (-LOG2E * slopes[:, :, :, None, None] * dist)
    first = jnp.where(kj <= qi, bias, NEG_INF)
    return jnp.stack([first, bias]).reshape(2, N_KV_HEADS, STACK, BLOCK)


def _attention(proj, sinks, batch, seq):
    nb = seq // BLOCK
    m = batch * seq
    cur = lambda b, n: b * nb + n
    prev = lambda b, n: b * nb + jnp.maximum(n - 1, 0)
    tile = lambda t, rows: pl.BlockSpec((None, BLOCK, IN_TILE), lambda b, n: (t, rows(b, n), 0))
    return pl.pallas_call(
        _attn_kernel,
        out_shape=jax.ShapeDtypeStruct((N_KV_HEADS, m, GROUP_W), BF16),
        grid=(batch, nb),
        in_specs=[pl.BlockSpec(memory_space=pltpu.SMEM),
                  pl.BlockSpec((None, N_KV_HEADS, STACK, BLOCK), lambda b, n: (jnp.minimum(n, 1), 0, 0, 0)),
                  pl.BlockSpec((N_Q_TILES, BLOCK, IN_TILE), lambda b, n: (0, cur(b, n), 0)),
                  tile(KV_TILE, cur), tile(KV_TILE, prev)]
                 + [tile(GATE_TILE0 + t, cur) for t in range(N_Q_TILES)],
        out_specs=pl.BlockSpec((N_KV_HEADS, BLOCK, GROUP_W), lambda b, n: (0, cur(b, n), 0)),
        compiler_params=_params(2),
        name="band_attention",
    )(sinks * LOG2E, _alibi_table(), *([proj] * (3 + N_Q_TILES)))


def _attn_out_kernel(x_ref, w_ref, r_ref, o_ref, wb_ref):
    jj, i = pl.program_id(0), pl.program_id(1)
    fill = jj % 2

    @pl.when(jj == 0)
    def _():
        wb_ref[fill, i] = w_ref[...].astype(BF16)

    @pl.when(jj > 0)
    def _():
        wb_ref[fill, i] = w_ref[...].astype(BF16)
        acc = r_ref[...]
        for h in range(N_KV_HEADS):
            acc = acc + jnp.dot(x_ref[h], wb_ref[1 - fill, h], preferred_element_type=F32)
        o_ref[...] = acc


def _attn_out_proj(og, w, resid, tm=1024, tn=1024):
    _, m, _ = og.shape
    n = w.shape[2]
    ni, nj = m // tm, n // tn
    assert ni == N_KV_HEADS
    row_tile = lambda jj, i: jnp.where(jj == 0, 0, i)
    out_blk = pl.BlockSpec((tm, tn), lambda jj, i: (row_tile(jj, i), jnp.maximum(jj - 1, 0)))
    return pl.pallas_call(
        _attn_out_kernel,
        out_shape=jax.ShapeDtypeStruct((m, n), F32),
        grid=(nj + 1, ni),
        in_specs=[pl.BlockSpec((N_KV_HEADS, tm, GROUP_W), lambda jj, i: (0, row_tile(jj, i), 0)),
                  pl.BlockSpec((None, GROUP_W, tn), lambda jj, i: (i, 0, jnp.minimum(jj, nj - 1))),
                  out_blk],
        out_specs=out_blk,
        scratch_shapes=[pltpu.VMEM((2, N_KV_HEADS, GROUP_W, tn), BF16)],
        compiler_params=_params(2),
        name="attn_out_proj",
    )(og, w, resid)


def _ssm_in_kernel(x_ref, w_ref, o_ref, wb_ref):
    jj, i = pl.program_id(0), pl.program_id(1)
    chunk = w_ref.shape[0]
    fill = jj % 2
    rows = pl.ds(pl.multiple_of(i * chunk, chunk), chunk)

    @pl.when(jj == 0)
    def _():
        wb_ref[fill, rows, :] = w_ref[...].astype(BF16)

    @pl.when(jj > 0)
    def _():
        wb_ref[fill, rows, :] = w_ref[...].astype(BF16)
        o_ref[...] = jnp.dot(x_ref[...], wb_ref[1 - fill], preferred_element_type=F32).astype(o_ref.dtype)


def _ssm_in_proj(x, w, tm=1024, tn=1024):
    m, k = x.shape
    n = w.shape[1]
    ni, nj = m // tm, n // tn
    chunk = k // ni
    row_tile = lambda jj, i: jnp.where(jj == 0, 0, i)
    return pl.pallas_call(
        _ssm_in_kernel,
        out_shape=jax.ShapeDtypeStruct((m, n), BF16),
        grid=(nj + 1, ni),
        in_specs=[pl.BlockSpec((tm, k), lambda jj, i: (row_tile(jj, i), 0)),
                  pl.BlockSpec((chunk, tn), lambda jj, i: (i, jnp.minimum(jj, nj - 1)))],
        out_specs=pl.BlockSpec((tm, tn), lambda jj, i: (row_tile(jj, i), jnp.maximum(jj - 1, 0))),
        scratch_shapes=[pltpu.VMEM((2, k, tn), BF16)],
        compiler_params=_params(2),
        name="ssm_in_proj",
    )(x, w)


def _glu_kernel(x_ref, w_ref, y_ref, gate_ref, o_ref):
    gt = gate_ref[...]
    gated = y_ref[...] * (gt * _sigmoid(gt))
    z = jnp.dot(x_ref[...], w_ref[...], preferred_element_type=F32)
    o_ref[...] = gated * _sigmoid(z.astype(BF16))


def _glu(y, w, ug, tm=1024, tn=512):
    m, k = y.shape
    n = w.shape[1]
    gate_off = n // tn
    return pl.pallas_call(
        _glu_kernel,
        out_shape=jax.ShapeDtypeStruct((m, n), BF16),
        grid=(m // tm, n // tn),
        in_specs=[pl.BlockSpec((tm, k), lambda i, j: (i, 0)),
                  pl.BlockSpec((k, tn), lambda i, j: (0, j)),
                  pl.BlockSpec((tm, tn), lambda i, j: (i, j)),
                  pl.BlockSpec((tm, tn), lambda i, j: (i, j + gate_off))],
        out_specs=pl.BlockSpec((tm, tn), lambda i, j: (i, j)),
        compiler_params=_params(2),
        name="ssm_glu",
    )(y, w, y, ug)


def _ssm_out_kernel(x_ref, w_ref, r_ref, o_ref, cols_ref):
    nj = x_ref.shape[1]
    x = x_ref[...].reshape(CHUNK * nj, x_ref.shape[2])
    acc = jnp.dot(x, w_ref[...], preferred_element_type=F32)
    for c in range(acc.shape[1] // LANES):
        for s in range(CHUNK):
            cols_ref[c, pl.ds(s, nj, stride=CHUNK), :] = acc[
                s * nj:(s + 1) * nj, c * LANES:(c + 1) * LANES]
        o_ref[:, c * LANES:(c + 1) * LANES] = cols_ref[c] + r_ref[:, c * LANES:(c + 1) * LANES]


def _ssm_out_proj(x, w, resid, tm=1024, tn=512):
    m, k = x.shape
    n = w.shape[1]
    nj = tm // CHUNK
    return pl.pallas_call(
        _ssm_out_kernel,
        out_shape=jax.ShapeDtypeStruct((m, n), F32),
        grid=(m // tm, n // tn),
        in_specs=[pl.BlockSpec((CHUNK, nj, k), lambda i, j: (0, i, 0)),
                  pl.BlockSpec((k, tn), lambda i, j: (0, j)),
                  pl.BlockSpec((tm, tn), lambda i, j: (i, j))],
        out_specs=pl.BlockSpec((tm, tn), lambda i, j: (i, j)),
        scratch_shapes=[pltpu.VMEM((tn // LANES, tm, LANES), F32)],
        compiler_params=_params(2),
        name="ssm_out_proj",
    )(x.reshape(CHUNK, m // CHUNK, k), w, resid)


N_SCAN_STEPS = 8
SCAN_BLOCK = 8
N_LOCAL_STEPS = 3


def _ssm_prep_kernel(lr_ref, li_ref, ldt_ref, btr_ref, bti_ref, cr_ref, ci_ref,
                     m_ref, be_ref, cet_ref, apr_ref, api_ref, bpr_ref, bpi_ref):
    tau = lax.broadcasted_iota(jnp.int32, (24, 2 * STATE_DIM), 0).astype(F32)
    quarter = jnp.where(lax.broadcasted_iota(jnp.int32, (24, 2 * STATE_DIM), 1) < STATE_DIM, 0.0, 0.5 * jnp.pi)
    lane = lax.broadcasted_iota(jnp.int32, (GROUP_SIZE, CHUNK_W), 1)

    def one_group(g, carry):
        lr = lr_ref[g]
        li = li_ref[g]
        dt = jnp.exp(ldt_ref[g])
        lr2 = jnp.concatenate([lr, lr], axis=1)
        li2 = jnp.concatenate([li, li], axis=1)
        dt2 = jnp.concatenate([dt, dt], axis=1)
        powers = jnp.exp(tau * (lr2 * dt2)) * jnp.cos(tau * (li2 * dt2) - quarter)
        pr = powers[:, :STATE_DIM]
        pi = powers[:, STATE_DIM:]
        ar, ai = pr[1:2], pi[1:2]
        den = lr * lr + li * li
        xr = ar - 1.0
        wr = (xr * lr + ai * li) / den
        wi = (ai * lr - xr * li) / den
        btr, bti = btr_ref[g], bti_ref[g]
        br = btr * wr - bti * wi
        bi = btr * wi + bti * wr
        cr, ci = cr_ref[g], ci_ref[g]

        ca_r, ca_i = [], []
        for t in range(CHUNK + 1):
            ca_r.append(cr * pr[t:t + 1] - ci * pi[t:t + 1])
            ca_i.append(cr * pi[t:t + 1] + ci * pr[t:t + 1])

        rt = jnp.concatenate([jnp.concatenate(ca_r[:CHUNK], axis=0),
                              jnp.concatenate(ca_i[:CHUNK], axis=0)], axis=1)
        bcat = jnp.concatenate([br, -bi], axis=1)
        kp = lax.dot_general(bcat.astype(BF16), rt.astype(BF16), NT,
                             preferred_element_type=F32)
        for s in range(CHUNK):
            blk = kp if s == 0 else pltpu.roll(kp, GROUP_SIZE * s, 1)
            blk = jnp.where(lane >= GROUP_SIZE * s, blk, 0.0)
            blk = jnp.concatenate([pltpu.roll(blk[:, :LANES], GROUP_SIZE * g, 1),
                                   pltpu.roll(blk[:, LANES:], GROUP_SIZE * g, 1)], axis=1)
            dst = pl.ds(pl.multiple_of(_step_slot(s, g) * GROUP_SIZE, GROUP_SIZE), GROUP_SIZE)
            m_ref[g, dst, :] = blk.astype(m_ref.dtype)
            e = CHUNK - 1 - s
            be_re = br * pr[e:e + 1] - bi * pi[e:e + 1]
            be_im = br * pi[e:e + 1] + bi * pr[e:e + 1]
            be_ref[g, dst, :] = jnp.concatenate(
                [be_re, be_im], axis=1).astype(be_ref.dtype)
            cet_ref[g, dst, :] = jnp.concatenate(
                [ca_r[s + 1], -ca_i[s + 1]], axis=1).astype(cet_ref.dtype)

        sq_r, sq_i = [pr[CHUNK:CHUNK + 1]], [pi[CHUNK:CHUNK + 1]]
        for _ in range(N_SCAN_STEPS - 1):
            xr, xi = sq_r[-1], sq_i[-1]
            sq_r.append(xr * xr - xi * xi)
            sq_i.append(2.0 * (xr * xi))
        mul_r, mul_i = [jnp.ones_like(sq_r[0]), sq_r[0]], [jnp.zeros_like(sq_r[0]), sq_i[0]]
        for _ in range(SCAN_BLOCK - 2):
            xr, xi = mul_r[-1], mul_i[-1]
            mul_r.append(xr * sq_r[0] - xi * sq_i[0])
            mul_i.append(xr * sq_i[0] + xi * sq_r[0])
        sr, si = jnp.concatenate(sq_r, axis=0), jnp.concatenate(sq_i, axis=0)
        rr, ri = jnp.concatenate(mul_r, axis=0), jnp.concatenate(mul_i, axis=0)
        apr_ref[g] = jnp.concatenate([sr, sr], axis=1)
        api_ref[g] = jnp.concatenate([-si, si], axis=1)
        bpr_ref[g] = jnp.concatenate([rr, rr], axis=1)
        bpi_ref[g] = jnp.concatenate([-ri, ri], axis=1)
        return carry

    lax.fori_loop(0, GROUPS_PER_BLOCK, one_group, 0, unroll=4)


def _ssm_prep(lam_re, lam_im, log_dt, b_re, b_im, c_re, c_im):
    g = N_GROUPS
    gb = GROUPS_PER_BLOCK
    row = lambda a: a.reshape(g, 1, STATE_DIM)
    ldt = jnp.broadcast_to(log_dt.reshape(g, 1, 1), (g, 1, STATE_DIM))
    btr = jnp.swapaxes(b_re, 1, 2)
    bti = jnp.swapaxes(b_im, 1, 2)
    vec = pl.BlockSpec((gb, 1, STATE_DIM), lambda i: (i, 0, 0))
    mat = pl.BlockSpec((gb, GROUP_SIZE, STATE_DIM), lambda i: (i, 0, 0))
    return pl.pallas_call(
        _ssm_prep_kernel,
        out_shape=[jax.ShapeDtypeStruct((g, CHUNK_W, CHUNK_W), BF16),
                   jax.ShapeDtypeStruct((g, CHUNK_W, 2 * STATE_DIM), BF16),
                   jax.ShapeDtypeStruct((g, CHUNK_W, 2 * STATE_DIM), BF16),
                   jax.ShapeDtypeStruct((g, N_SCAN_STEPS, 2 * STATE_DIM), F32),
                   jax.ShapeDtypeStruct((g, N_SCAN_STEPS, 2 * STATE_DIM), F32),
                   jax.ShapeDtypeStruct((g, SCAN_BLOCK, 2 * STATE_DIM), F32),
                   jax.ShapeDtypeStruct((g, SCAN_BLOCK, 2 * STATE_DIM), F32)],
        grid=(g // gb,),
        in_specs=[vec, vec, vec, mat, mat, mat, mat],
        out_specs=[pl.BlockSpec((gb, CHUNK_W, CHUNK_W), lambda i: (i, 0, 0)),
                   pl.BlockSpec((gb, CHUNK_W, 2 * STATE_DIM), lambda i: (i, 0, 0)),
                   pl.BlockSpec((gb, CHUNK_W, 2 * STATE_DIM), lambda i: (i, 0, 0)),
                   pl.BlockSpec((gb, N_SCAN_STEPS, 2 * STATE_DIM), lambda i: (i, 0, 0)),
                   pl.BlockSpec((gb, N_SCAN_STEPS, 2 * STATE_DIM), lambda i: (i, 0, 0)),
                   pl.BlockSpec((gb, SCAN_BLOCK, 2 * STATE_DIM), lambda i: (i, 0, 0)),
                   pl.BlockSpec((gb, SCAN_BLOCK, 2 * STATE_DIM), lambda i: (i, 0, 0))],
        compiler_params=_params(1),
        name="ssm_prep",
    )(row(lam_re), row(lam_im), ldt, btr, bti, c_re, c_im)


XPOSE_ROWS = 256


def _step_slot(step, group):
    half, k = divmod(step, GROUPS_PER_BLOCK)
    return half * GROUPS_PER_BLOCK + (k + group) % GROUPS_PER_BLOCK


def _slot_masks(shape):
    slot = lax.broadcasted_iota(jnp.int32, shape, 1) // GROUP_SIZE
    return [slot == p for p in range(GROUPS_PER_BLOCK)]


def _merge_slots(pieces, masks):
    out = pieces[0]
    for p in range(1, GROUPS_PER_BLOCK):
        out = jnp.where(masks[p], pieces[p], out)
    return out


def _steps_to_groups(xs):
    masks = _slot_masks(xs[0].shape)
    n = GROUPS_PER_BLOCK
    rot = [xs[k] if k == 0 else _lane_roll(xs[k], GROUP_SIZE * k) for k in range(n)]
    return [_merge_slots([rot[(p - g) % n] for p in range(n)], masks) for g in range(n)]


def _groups_to_steps(ys):
    masks = _slot_masks(ys[0].shape)
    n = GROUPS_PER_BLOCK
    out = []
    for k in range(n):
        z = _merge_slots([ys[(p - k) % n] for p in range(n)], masks)
        out.append(z if k == 0 else _lane_roll(z, LANES - GROUP_SIZE * k))
    return out


def _cmul(x, mr, mi):
    return x * mr + pltpu.roll(x, STATE_DIM, 1) * mi


def _ssm_kernel(x_ref, m_ref, be_ref, cet_ref, apr_ref, api_ref, bpr_ref, bpi_ref, d_ref, wg_ref, wo_ref,
                o_ref, wgb_ref, wob_ref, uf_ref, yf_ref, xs_ref, cs_ref, *, batch, n_chunks):
    wgb_ref[...] = wg_ref[...].astype(BF16)
    wob_ref[...] = wo_ref[...].astype(BF16)
    rows = batch * n_chunks
    n_blocks = n_chunks // SCAN_BLOCK

    def gather(i, carry):
        r0 = pl.multiple_of(i * XPOSE_ROWS, XPOSE_ROWS)
        for half in range(2):
            vs = [x_ref[GROUPS_PER_BLOCK * half + k, pl.ds(r0, XPOSE_ROWS), :]
                  for k in range(GROUPS_PER_BLOCK)]
            ts = _steps_to_groups(vs)
            for g in range(GROUPS_PER_BLOCK):
                uf_ref[g, pl.ds(r0, XPOSE_ROWS), half * LANES:(half + 1) * LANES] = ts[g]
        return carry

    lax.fori_loop(0, rows // XPOSE_ROWS, gather, 0)

    n_all = rows // SCAN_BLOCK
    row = lax.broadcasted_iota(jnp.int32, (rows, 2 * STATE_DIM), 0)
    in_block = row & (SCAN_BLOCK - 1)
    brow = lax.broadcasted_iota(jnp.int32, (n_all, 2 * STATE_DIM), 0)
    in_seq = brow & (n_blocks - 1)

    def shift_rows(x, d, pos):
        return jnp.where(pos >= d, pltpu.roll(x, d, 0), 0.0)

    def one_group(g, carry):
        u = uf_ref[g]
        y = jnp.dot(u, m_ref[g], preferred_element_type=F32)
        x = jnp.dot(u, be_ref[g], preferred_element_type=F32)
        apr, api = apr_ref[g], api_ref[g]
        for k in range(N_LOCAL_STEPS):
            x = x + _cmul(shift_rows(x, 1 << k, in_block), apr[k:k + 1], api[k:k + 1])
        xs_ref[...] = x
        c = xs_ref[pl.ds(SCAN_BLOCK - 1, n_all, stride=SCAN_BLOCK), :]
        for k in range(N_LOCAL_STEPS, N_SCAN_STEPS):
            c = c + _cmul(shift_rows(c, 1 << (k - N_LOCAL_STEPS), in_seq), apr[k:k + 1], api[k:k + 1])
        cs_ref[...] = shift_rows(c, 1, in_seq)
        bpr, bpi = bpr_ref[g], bpi_ref[g]
        carried = [_cmul(jnp.broadcast_to(cs_ref[mblk:mblk + 1, :], (SCAN_BLOCK, 2 * STATE_DIM)), bpr, bpi)
                   for mblk in range(n_all)]
        e = (shift_rows(x, 1, in_block) + jnp.concatenate(carried, axis=0)).astype(BF16)
        y = y + lax.dot_general(e, cet_ref[g], NT, preferred_element_type=F32)
        y = y + d_ref[g] * u.astype(F32)
        yf_ref[g] = jax.nn.gelu(y).astype(yf_ref.dtype)
        return carry

    lax.fori_loop(0, GROUPS_PER_BLOCK, one_group, 0, unroll=4)

    def scatter(i, carry):
        r0 = pl.multiple_of(i * XPOSE_ROWS, XPOSE_ROWS)
        for half in range(2):
            vs = [yf_ref[g, pl.ds(r0, XPOSE_ROWS), half * LANES:(half + 1) * LANES]
                  for g in range(GROUPS_PER_BLOCK)]
            ts = _groups_to_steps(vs)
            for k in range(GROUPS_PER_BLOCK):
                o_ref[GROUPS_PER_BLOCK * half + k, pl.ds(r0, XPOSE_ROWS), :] = ts[k]
        return carry

    lax.fori_loop(0, rows // XPOSE_ROWS, scatter, 0)


def _ssm(ug, m, be, cet, apr, api, bpr, bpi, dvec, w_glu, w_out, batch):
    _, rows, _ = ug.shape
    n_chunks = rows // batch
    gb = GROUPS_PER_BLOCK
    steps = N_GROUPS // gb
    col_blk = pl.BlockSpec((CHUNK, rows, LANES), lambda i: (0, 0, i))
    grp = lambda last2: pl.BlockSpec((gb,) + last2, lambda i: (i, 0, 0))
    slab = lambda w: pl.BlockSpec((w.shape[0] // steps, w.shape[1]), lambda i: (i, 0))
    return pl.pallas_call(
        functools.partial(_ssm_kernel, batch=batch, n_chunks=n_chunks),
        out_shape=[jax.ShapeDtypeStruct((CHUNK, rows, SSM_WIDTH), BF16),
                   jax.ShapeDtypeStruct(w_glu.shape, BF16),
                   jax.ShapeDtypeStruct(w_out.shape, BF16)],
        grid=(steps,),
        in_specs=[col_blk, grp((CHUNK_W, CHUNK_W)), grp((CHUNK_W, 2 * STATE_DIM)),
                  grp((CHUNK_W, 2 * STATE_DIM)), grp((N_SCAN_STEPS, 2 * STATE_DIM)),
                  grp((N_SCAN_STEPS, 2 * STATE_DIM)), grp((SCAN_BLOCK, 2 * STATE_DIM)),
                  grp((SCAN_BLOCK, 2 * STATE_DIM)), grp((1, CHUNK_W)), slab(w_glu), slab(w_out)],
        out_specs=[col_blk, slab(w_glu), slab(w_out)],
        scratch_shapes=[pltpu.VMEM((gb, rows, CHUNK_W), BF16),
                        pltpu.VMEM((gb, rows, CHUNK_W), BF16),
                        pltpu.VMEM((rows, 2 * STATE_DIM), F32),
                        pltpu.VMEM((rows // SCAN_BLOCK, 2 * STATE_DIM), F32)],
        compiler_params=_params(1),
        name="ssm_chunked",
    )(ug, m, be, cet, apr, api, bpr, bpi, dvec, w_glu, w_out)


def kernel(x, norm_g, attn_w_in, attn_q_norm_g, attn_k_norm_g, attn_sinks, attn_w_out, ssm_w_in,
           ssm_log_dt, ssm_lam_re, ssm_lam_im, ssm_b_re, ssm_b_im, ssm_c_re, ssm_c_im, ssm_d,
           ssm_w_glu, ssm_w_out):
    batch, seq, d = x.shape
    m = batch * seq
    assert d == D_MODEL and seq % BLOCK == 0 and seq // CHUNK == 1 << N_SCAN_STEPS
    x2 = x.reshape(m, d)

    q_gain = jnp.tile(attn_q_norm_g[0].astype(F32) * (HEAD_DIM ** -0.5 * LOG2E), IN_TILE // HEAD_DIM)
    k_gain = jnp.tile(attn_k_norm_g[0].astype(F32), N_KV_HEADS)

    hn0 = _rmsnorm(x2, norm_g[0])
    proj = _attn_in_proj(hn0, attn_w_in[0], q_gain.reshape(1, IN_TILE), k_gain.reshape(1, KV_WIDTH))
    og = _attention(proj, attn_sinks[0].astype(F32), batch, seq)
    h1 = _attn_out_proj(og, attn_w_out[0].reshape(N_KV_HEADS, GROUP_W, D_MODEL), x2)

    nj = m // CHUNK
    hn1 = _rmsnorm_chunk_major(h1, norm_g[1])
    ug = _ssm_in_proj(hn1, ssm_w_in[0])
    mt, be, cet, apr, api, bpr, bpi = _ssm_prep(ssm_lam_re[0], ssm_lam_im[0], ssm_log_dt[0],
                                      ssm_b_re[0], ssm_b_im[0], ssm_c_re[0], ssm_c_im[0])
    dvec = jnp.tile(ssm_d[0].astype(F32).reshape(N_GROUPS, 1, GROUP_SIZE), (1, 1, CHUNK))
    y, w_glu, w_out = _ssm(ug.reshape(CHUNK, nj, 2 * SSM_WIDTH), mt, be, cet, apr, api, bpr, bpi, dvec,
                           ssm_w_glu[0], ssm_w_out[0], batch)
    y = y.reshape(m, SSM_WIDTH)
    p = _glu(y, w_glu, ug)
    out = _ssm_out_proj(p, w_out, h1)
    return out.reshape(batch, seq, d)
```

```python
import functools

import jax
import jax.numpy as jnp
from jax import lax
from jax.experimental import pallas as pl
from jax.experimental.pallas import tpu as pltpu

F32 = jnp.float32
BF16 = jnp.bfloat16

D_MODEL = 4096
HEAD_DIM = 64
N_Q_HEADS = 64
N_KV_HEADS = 8
Q_PER_KV = 8
N_PAIRS = Q_PER_KV // 2
KV_WIDTH = N_KV_HEADS * HEAD_DIM
GROUP_W = Q_PER_KV * HEAD_DIM
BLOCK = 128
STACK = Q_PER_KV * BLOCK
SSM_WIDTH = 8192
GROUP_SIZE = 16
N_GROUPS = 512
STATE_DIM = 64
CHUNK = 16
CHUNK_W = CHUNK * GROUP_SIZE
RMS_EPS = 1e-6
NEG_INF = -1e30

LANES = 128
VMEM_LIMIT = 60 * 1024 * 1024
MXU_W = 256
GROUPS_PER_BLOCK = LANES // GROUP_SIZE

NT = (((1,), (1,)), ((), ()))


def _lane_roll(x, shift):
    return jnp.concatenate([x[:, LANES - shift:], x[:, :LANES - shift]], axis=1)


def _sigmoid(x):
    return 0.5 * jnp.tanh(0.5 * x) + 0.5


def _params(n_grid_dims):
    return pltpu.CompilerParams(
        dimension_semantics=("arbitrary",) * n_grid_dims, vmem_limit_bytes=VMEM_LIMIT)


def _rms(x, g):
    var = jnp.mean(x * x, axis=-1, keepdims=True)
    return x * lax.rsqrt(var + RMS_EPS) * g


def _rmsnorm_kernel(x_ref, g_ref, o_ref):
    o_ref[...] = _rms(x_ref[...], g_ref[...]).astype(o_ref.dtype)


def _rmsnorm(x, g, tm=256):
    m, d = x.shape
    return pl.pallas_call(
        _rmsnorm_kernel,
        out_shape=jax.ShapeDtypeStruct((m, d), BF16),
        grid=(m // tm,),
        in_specs=[pl.BlockSpec((tm, d), lambda i: (i, 0)),
                  pl.BlockSpec((1, d), lambda i: (0, 0))],
        out_specs=pl.BlockSpec((tm, d), lambda i: (i, 0)),
        compiler_params=_params(1),
        name="rmsnorm",
    )(x, g.reshape(1, d))


def _rmsnorm_chunk_major_kernel(x_ref, g_ref, p_ref, o_ref):
    hn = _rms(x_ref[...], g_ref[...]).astype(BF16)
    moved = jnp.dot(p_ref[...], hn, preferred_element_type=F32).astype(o_ref.dtype)
    nj = hn.shape[0] // CHUNK
    for s in range(CHUNK):
        o_ref[s] = moved[s * nj:(s + 1) * nj]


def _rmsnorm_chunk_major(x, g, tm=256):
    m, d = x.shape
    nj = tm // CHUNK
    src = (jnp.arange(tm) % nj) * CHUNK + jnp.arange(tm) // nj
    perm = (src[:, None] == jnp.arange(tm)[None, :]).astype(BF16)
    out = pl.pallas_call(
        _rmsnorm_chunk_major_kernel,
        out_shape=jax.ShapeDtypeStruct((CHUNK, m // CHUNK, d), BF16),
        grid=(m // tm,),
        in_specs=[pl.BlockSpec((tm, d), lambda i: (i, 0)),
                  pl.BlockSpec((1, d), lambda i: (0, 0)),
                  pl.BlockSpec((tm, tm), lambda i: (0, 0))],
        out_specs=pl.BlockSpec((CHUNK, nj, d), lambda i: (0, i, 0)),
        compiler_params=_params(1),
        name="rmsnorm_chunk_major",
    )(x, g.reshape(1, d), perm)
    return out.reshape(m, d)


def _head_rmsnorm(acc, ones_bd, gain):
    outs = []
    for c in range(acc.shape[1] // MXU_W):
        a = acc[:, c * MXU_W:(c + 1) * MXU_W]
        ss = jnp.dot((a * a).astype(BF16), ones_bd, preferred_element_type=F32)
        inv = lax.rsqrt(ss * (1.0 / HEAD_DIM) + RMS_EPS)
        outs.append(a * inv * gain[:, c * MXU_W:(c + 1) * MXU_W])
    return outs[0] if len(outs) == 1 else jnp.concatenate(outs, axis=1)


def _ones_block_diag():
    r = jnp.arange(MXU_W) // HEAD_DIM
    return (r[:, None] == r[None, :]).astype(BF16)


IN_TILE = 1024
N_Q_TILES = D_MODEL // IN_TILE
KV_TILE = N_Q_TILES
GATE_TILE0 = KV_TILE + 1
N_IN_TILES = (2 * D_MODEL + 2 * KV_WIDTH) // IN_TILE


def _attn_in_kernel(x_ref, w_ref, e_ref, qg_ref, kg_ref, o_ref, wb_ref):
    jj, i = pl.program_id(0), pl.program_id(1)
    tile = jj - 1
    chunk = w_ref.shape[0]
    fill = jj % 2
    rows = pl.ds(pl.multiple_of(i * chunk, chunk), chunk)

    def stage_next_tile():
        wb_ref[fill, rows, :] = w_ref[...].astype(BF16)

    def product():
        return jnp.dot(x_ref[...], wb_ref[1 - fill], preferred_element_type=F32)

    @pl.when(tile < 0)
    def _():
        stage_next_tile()

    @pl.when(jnp.logical_and(tile >= 0, tile < KV_TILE))
    def _():
        stage_next_tile()
        o_ref[...] = _head_rmsnorm(product(), e_ref[...], qg_ref[...]).astype(o_ref.dtype)

    @pl.when(tile == KV_TILE)
    def _():
        stage_next_tile()
        acc = product()
        kn = _head_rmsnorm(acc[:, :KV_WIDTH], e_ref[...], kg_ref[...])
        for h in range(N_KV_HEADS):
            o_ref[:, h * LANES:h * LANES + HEAD_DIM] = kn[:, h * HEAD_DIM:(h + 1) * HEAD_DIM].astype(o_ref.dtype)
            o_ref[:, h * LANES + HEAD_DIM:(h + 1) * LANES] = acc[
                :, KV_WIDTH + h * HEAD_DIM:KV_WIDTH + (h + 1) * HEAD_DIM].astype(o_ref.dtype)

    @pl.when(tile > KV_TILE)
    def _():
        stage_next_tile()
        o_ref[...] = product().astype(o_ref.dtype)


def _attn_in_proj(hn, w_in, q_gain, k_gain, tm=1024):
    m, k = hn.shape
    ni = m // tm
    chunk = k // ni
    row_tile = lambda jj, i: jnp.where(jj == 0, 0, i)
    const = lambda jj, i: (0, 0)
    return pl.pallas_call(
        _attn_in_kernel,
        out_shape=jax.ShapeDtypeStruct((N_IN_TILES, m, IN_TILE), BF16),
        grid=(N_IN_TILES + 1, ni),
        in_specs=[pl.BlockSpec((tm, k), lambda jj, i: (row_tile(jj, i), 0)),
                  pl.BlockSpec((chunk, IN_TILE), lambda jj, i: (i, jnp.minimum(jj, N_IN_TILES - 1))),
                  pl.BlockSpec((MXU_W, MXU_W), const),
                  pl.BlockSpec((1, IN_TILE), const),
                  pl.BlockSpec((1, KV_WIDTH), const)],
        out_specs=pl.BlockSpec((None, tm, IN_TILE),
                               lambda jj, i: (jnp.maximum(jj - 1, 0), row_tile(jj, i), 0)),
        scratch_shapes=[pltpu.VMEM((2, k, IN_TILE), BF16)],
        compiler_params=_params(2),
        name="attn_in_proj",
    )(hn, w_in, _ones_block_diag(), q_gain, k_gain)


def _attn_kernel(sink_ref, alibi_ref, q_ref, kvc_ref, kvp_ref, g0_ref, g1_ref, g2_ref, g3_ref, o_ref):
    gate_refs = (g0_ref, g1_ref, g2_ref, g3_ref)
    row = lax.broadcasted_iota(jnp.int32, (BLOCK, BLOCK), 0)
    col = lax.broadcasted_iota(jnp.int32, (BLOCK, BLOCK), 1)
    lower = col <= row
    left = col < HEAD_DIM
    ones = jnp.ones((2 * BLOCK, LANES), BF16)

    def split_kv(kv):
        swapped = _lane_roll(kv, HEAD_DIM)
        zero = jnp.zeros_like(kv)
        return jnp.where(left, kv, zero), jnp.where(left, zero, swapped), jnp.where(left, swapped, kv)

    for kvh in range(N_KV_HEADS):
        tile, col0 = kvh // 2, (kvh % 2) * GROUP_W
        kc_even, kc_odd, vvc = split_kv(kvc_ref[:, kvh * LANES:(kvh + 1) * LANES])
        kp_even, kp_odd, vvp = split_kv(kvp_ref[:, kvh * LANES:(kvh + 1) * LANES])
        qp = jnp.concatenate(
            [q_ref[tile, :, col0 + a * LANES:col0 + (a + 1) * LANES] for a in range(N_PAIRS)], axis=0)
        keys = jnp.concatenate([kc_even, kc_odd, kp_even, kp_odd], axis=0)
        s_all = lax.dot_general(qp, keys, NT, preferred_element_type=F32)
        v_ones = jnp.concatenate([jnp.concatenate([vvc, vvp], axis=0), ones], axis=1)
        ps, sink_terms = [], []
        for par in range(2):
            for a in range(N_PAIRS):
                rows = slice(a * BLOCK, (a + 1) * BLOCK)
                sc = s_all[rows, par * LANES:(par + 1) * LANES]
                sp = s_all[rows, (2 + par) * LANES:(3 + par) * LANES]
                blk = par * N_PAIRS + a
                s = jnp.where(lower, sc, sp) + alibi_ref[kvh, blk * BLOCK:(blk + 1) * BLOCK, :]
                sink = sink_ref[kvh * Q_PER_KV + 2 * a + par]
                m = jnp.maximum(jnp.max(s, axis=-1, keepdims=True), sink)
                p = jnp.exp2(s - m)
                ps.append(jnp.concatenate([jnp.where(lower, p, 0.0), jnp.where(lower, 0.0, p)],
                                          axis=1).astype(BF16))
                sink_terms.append(jnp.exp2(sink - m))
        nd = jnp.dot(jnp.concatenate(ps, axis=0), v_ones, preferred_element_type=F32)
        for a in range(N_PAIRS):
            o = []
            for par in range(2):
                blk = par * N_PAIRS + a
                r = nd[blk * BLOCK:(blk + 1) * BLOCK]
                o.append(r[:, :LANES] / (r[:, LANES:] + sink_terms[blk]))
            gt = gate_refs[tile][:, col0 + a * LANES:col0 + (a + 1) * LANES].astype(F32)
            o_ref[kvh, :, a * LANES:(a + 1) * LANES] = (
                jnp.where(left, o[0], o[1]) * (gt * _sigmoid(gt))).astype(o_ref.dtype)


LOG2E = 1.4426950408889634


def _alibi_table():
    qi = jnp.arange(BLOCK)[:, None]
    kj = jnp.arange(BLOCK)[None, :]
    dist = jnp.where(kj <= qi, qi - kj, BLOCK + qi - kj).astype(F32)
    slopes = jnp.exp2(-8.0 * jnp.arange(1, N_Q_HEADS + 1, dtype=F32) / N_Q_HEADS)
    slopes = slopes.reshape(N_KV_HEADS, N_PAIRS, 2).transpose(0, 2, 1)
    bias = (-LOG2E * slopes[:, :, :, None, None] * dist)
    first = jnp.where(kj <= qi, bias, NEG_INF)
    return jnp.stack([first, bias]).reshape(2, N_KV_HEADS, STACK, BLOCK)


def _attention(proj, sinks, batch, seq):
    nb = seq // BLOCK
    m = batch * seq
    cur = lambda b, n: b * nb + n
    prev = lambda b, n: b * nb + jnp.maximum(n - 1, 0)
    tile = lambda t, rows: pl.BlockSpec((None, BLOCK, IN_TILE), lambda b, n: (t, rows(b, n), 0))
    return pl.pallas_call(
        _attn_kernel,
        out_shape=jax.ShapeDtypeStruct((N_KV_HEADS, m, GROUP_W), BF16),
        grid=(batch, nb),
        in_specs=[pl.BlockSpec(memory_space=pltpu.SMEM),
                  pl.BlockSpec((None, N_KV_HEADS, STACK, BLOCK), lambda b, n: (jnp.minimum(n, 1), 0, 0, 0)),
                  pl.BlockSpec((N_Q_TILES, BLOCK, IN_TILE), lambda b, n: (0, cur(b, n), 0)),
                  tile(KV_TILE, cur), tile(KV_TILE, prev)]
                 + [tile(GATE_TILE0 + t, cur) for t in range(N_Q_TILES)],
        out_specs=pl.BlockSpec((N_KV_HEADS, BLOCK, GROUP_W), lambda b, n: (0, cur(b, n), 0)),
        compiler_params=_params(2),
        name="band_attention",
    )(sinks * LOG2E, _alibi_table(), *([proj] * (3 + N_Q_TILES)))


def _attn_out_kernel(x_ref, w_ref, r_ref, o_ref, wb_ref):
    jj, i = pl.program_id(0), pl.program_id(1)
    fill = jj % 2

    @pl.when(jj == 0)
    def _():
        wb_ref[fill, i] = w_ref[...].astype(BF16)

    @pl.when(jj > 0)
    def _():
        wb_ref[fill, i] = w_ref[...].astype(BF16)
        acc = r_ref[...]
        for h in range(N_KV_HEADS):
            acc = acc + jnp.dot(x_ref[h], wb_ref[1 - fill, h], preferred_element_type=F32)
        o_ref[...] = acc


def _attn_out_proj(og, w, resid, tm=1024, tn=1024):
    _, m, _ = og.shape
    n = w.shape[2]
    ni, nj = m // tm, n // tn
    assert ni == N_KV_HEADS
    row_tile = lambda jj, i: jnp.where(jj == 0, 0, i)
    out_blk = pl.BlockSpec((tm, tn), lambda jj, i: (row_tile(jj, i), jnp.maximum(jj - 1, 0)))
    return pl.pallas_call(
        _attn_out_kernel,
        out_shape=jax.ShapeDtypeStruct((m, n), F32),
        grid=(nj + 1, ni),
        in_specs=[pl.BlockSpec((N_KV_HEADS, tm, GROUP_W), lambda jj, i: (0, row_tile(jj, i), 0)),
                  pl.BlockSpec((None, GROUP_W, tn), lambda jj, i: (i, 0, jnp.minimum(jj, nj - 1))),
                  out_blk],
        out_specs=out_blk,
        scratch_shapes=[pltpu.VMEM((2, N_KV_HEADS, GROUP_W, tn), BF16)],
        compiler_params=_params(2),
        name="attn_out_proj",
    )(og, w, resid)


def _ssm_in_kernel(x_ref, w_ref, o_ref, wb_ref):
    jj, i = pl.program_id(0), pl.program_id(1)
    chunk = w_ref.shape[0]
    fill = jj % 2
    rows = pl.ds(pl.multiple_of(i * chunk, chunk), chunk)

    @pl.when(jj == 0)
    def _():
        wb_ref[fill, rows, :] = w_ref[...].astype(BF16)

    @pl.when(jj > 0)
    def _():
        wb_ref[fill, rows, :] = w_ref[...].astype(BF16)
        o_ref[...] = jnp.dot(x_ref[...], wb_ref[1 - fill], preferred_element_type=F32).astype(o_ref.dtype)


def _ssm_in_proj(x, w, tm=1024, tn=1024):
    m, k = x.shape
    n = w.shape[1]
    ni, nj = m // tm, n // tn
    chunk = k // ni
    row_tile = lambda jj, i: jnp.where(jj == 0, 0, i)
    return pl.pallas_call(
        _ssm_in_kernel,
        out_shape=jax.ShapeDtypeStruct((m, n), BF16),
        grid=(nj + 1, ni),
        in_specs=[pl.BlockSpec((tm, k), lambda jj, i: (row_tile(jj, i), 0)),
                  pl.BlockSpec((chunk, tn), lambda jj, i: (i, jnp.minimum(jj, nj - 1)))],
        out_specs=pl.BlockSpec((tm, tn), lambda jj, i: (row_tile(jj, i), jnp.maximum(jj - 1, 0))),
        scratch_shapes=[pltpu.VMEM((2, k, tn), BF16)],
        compiler_params=_params(2),
        name="ssm_in_proj",
    )(x, w)


def _glu_kernel(x_ref, w_ref, y_ref, gate_ref, o_ref):
    gt = gate_ref[...]
    gated = y_ref[...] * (gt * _sigmoid(gt))
    z = jnp.dot(x_ref[...], w_ref[...], preferred_element_type=F32)
    o_ref[...] = gated * _sigmoid(z.astype(BF16))


def _glu(y, w, ug, tm=1024, tn=512):
    m, k = y.shape
    n = w.shape[1]
    gate_off = n // tn
    return pl.pallas_call(
        _glu_kernel,
        out_shape=jax.ShapeDtypeStruct((m, n), BF16),
        grid=(m // tm, n // tn),
        in_specs=[pl.BlockSpec((tm, k), lambda i, j: (i, 0)),
                  pl.BlockSpec((k, tn), lambda i, j: (0, j)),
                  pl.BlockSpec((tm, tn), lambda i, j: (i, j)),
                  pl.BlockSpec((tm, tn), lambda i, j: (i, j + gate_off))],
        out_specs=pl.BlockSpec((tm, tn), lambda i, j: (i, j)),
        compiler_params=_params(2),
        name="ssm_glu",
    )(y, w, y, ug)


def _ssm_out_kernel(x_ref, w_ref, r_ref, o_ref, cols_ref):
    nj = x_ref.shape[1]
    x = x_ref[...].reshape(CHUNK * nj, x_ref.shape[2])
    acc = jnp.dot(x, w_ref[...], preferred_element_type=F32)
    for c in range(acc.shape[1] // LANES):
        for s in range(CHUNK):
            cols_ref[c, pl.ds(s, nj, stride=CHUNK), :] = acc[
                s * nj:(s + 1) * nj, c * LANES:(c + 1) * LANES]
        o_ref[:, c * LANES:(c + 1) * LANES] = cols_ref[c] + r_ref[:, c * LANES:(c + 1) * LANES]


def _ssm_out_proj(x, w, resid, tm=1024, tn=512):
    m, k = x.shape
    n = w.shape[1]
    nj = tm // CHUNK
    return pl.pallas_call(
        _ssm_out_kernel,
        out_shape=jax.ShapeDtypeStruct((m, n), F32),
        grid=(m // tm, n // tn),
        in_specs=[pl.BlockSpec((CHUNK, nj, k), lambda i, j: (0, i, 0)),
                  pl.BlockSpec((k, tn), lambda i, j: (0, j)),
                  pl.BlockSpec((tm, tn), lambda i, j: (i, j))],
        out_specs=pl.BlockSpec((tm, tn), lambda i, j: (i, j)),
        scratch_shapes=[pltpu.VMEM((tn // LANES, tm, LANES), F32)],
        compiler_params=_params(2),
        name="ssm_out_proj",
    )(x.reshape(CHUNK, m // CHUNK, k), w, resid)


N_SCAN_STEPS = 8
SCAN_BLOCK = 8
N_LOCAL_STEPS = 3


def _ssm_prep_kernel(lr_ref, li_ref, ldt_ref, btr_ref, bti_ref, cr_ref, ci_ref,
                     m_ref, be_ref, cet_ref, apr_ref, api_ref, bpr_ref, bpi_ref):
    tau = lax.broadcasted_iota(jnp.int32, (24, 2 * STATE_DIM), 0).astype(F32)
    quarter = jnp.where(lax.broadcasted_iota(jnp.int32, (24, 2 * STATE_DIM), 1) < STATE_DIM, 0.0, 0.5 * jnp.pi)
    lane = lax.broadcasted_iota(jnp.int32, (GROUP_SIZE, CHUNK_W), 1)

    def one_group(g, carry):
        lr = lr_ref[g]
        li = li_ref[g]
        dt = jnp.exp(ldt_ref[g])
        lr2 = jnp.concatenate([lr, lr], axis=1)
        li2 = jnp.concatenate([li, li], axis=1)
        dt2 = jnp.concatenate([dt, dt], axis=1)
        powers = jnp.exp(tau * (lr2 * dt2)) * jnp.cos(tau * (li2 * dt2) - quarter)
        pr = powers[:, :STATE_DIM]
        pi = powers[:, STATE_DIM:]
        ar, ai = pr[1:2], pi[1:2]
        den = lr * lr + li * li
        xr = ar - 1.0
        wr = (xr * lr + ai * li) / den
        wi = (ai * lr - xr * li) / den
        btr, bti = btr_ref[g], bti_ref[g]
        br = btr * wr - bti * wi
        bi = btr * wi + bti * wr
        cr, ci = cr_ref[g], ci_ref[g]

        ca_r, ca_i = [], []
        for t in range(CHUNK + 1):
            ca_r.append(cr * pr[t:t + 1] - ci * pi[t:t + 1])
            ca_i.append(cr * pi[t:t + 1] + ci * pr[t:t + 1])

        rt = jnp.concatenate([jnp.concatenate(ca_r[:CHUNK], axis=0),
                              jnp.concatenate(ca_i[:CHUNK], axis=0)], axis=1)
        bcat = jnp.concatenate([br, -bi], axis=1)
        kp = lax.dot_general(bcat.astype(BF16), rt.astype(BF16), NT,
                             preferred_element_type=F32)
        for s in range(CHUNK):
            blk = kp if s == 0 else pltpu.roll(kp, GROUP_SIZE * s, 1)
            blk = jnp.where(lane >= GROUP_SIZE * s, blk, 0.0)
            blk = jnp.concatenate([pltpu.roll(blk[:, :LANES], GROUP_SIZE * g, 1),
                                   pltpu.roll(blk[:, LANES:], GROUP_SIZE * g, 1)], axis=1)
            dst = pl.ds(pl.multiple_of(_step_slot(s, g) * GROUP_SIZE, GROUP_SIZE), GROUP_SIZE)
            m_ref[g, dst, :] = blk.astype(m_ref.dtype)
            e = CHUNK - 1 - s
            be_re = br * pr[e:e + 1] - bi * pi[e:e + 1]
            be_im = br * pi[e:e + 1] + bi * pr[e:e + 1]
            be_ref[g, dst, :] = jnp.concatenate(
                [be_re, be_im], axis=1).astype(be_ref.dtype)
            cet_ref[g, dst, :] = jnp.concatenate(
                [ca_r[s + 1], -ca_i[s + 1]], axis=1).astype(cet_ref.dtype)

        sq_r, sq_i = [pr[CHUNK:CHUNK + 1]], [pi[CHUNK:CHUNK + 1]]
        for _ in range(N_SCAN_STEPS - 1):
            xr, xi = sq_r[-1], sq_i[-1]
            sq_r.append(xr * xr - xi * xi)
            sq_i.append(2.0 * (xr * xi))
        mul_r, mul_i = [jnp.ones_like(sq_r[0]), sq_r[0]], [jnp.zeros_like(sq_r[0]), sq_i[0]]
        for _ in range(SCAN_BLOCK - 2):
            xr, xi = mul_r[-1], mul_i[-1]
            mul_r.append(xr * sq_r[0] - xi * sq_i[0])
            mul_i.append(xr * sq_i[0] + xi * sq_r[0])
        sr, si = jnp.concatenate(sq_r, axis=0), jnp.concatenate(sq_i, axis=0)
        rr, ri = jnp.concatenate(mul_r, axis=0), jnp.concatenate(mul_i, axis=0)
        apr_ref[g] = jnp.concatenate([sr, sr], axis=1)
        api_ref[g] = jnp.concatenate([-si, si], axis=1)
        bpr_ref[g] = jnp.concatenate([rr, rr], axis=1)
        bpi_ref[g] = jnp.concatenate([-ri, ri], axis=1)
        return carry

    lax.fori_loop(0, GROUPS_PER_BLOCK, one_group, 0, unroll=4)


def _ssm_prep(lam_re, lam_im, log_dt, b_re, b_im, c_re, c_im):
    g = N_GROUPS
    gb = GROUPS_PER_BLOCK
    row = lambda a: a.reshape(g, 1, STATE_DIM)
    ldt = jnp.broadcast_to(log_dt.reshape(g, 1, 1), (g, 1, STATE_DIM))
    btr = jnp.swapaxes(b_re, 1, 2)
    bti = jnp.swapaxes(b_im, 1, 2)
    vec = pl.BlockSpec((gb, 1, STATE_DIM), lambda i: (i, 0, 0))
    mat = pl.BlockSpec((gb, GROUP_SIZE, STATE_DIM), lambda i: (i, 0, 0))
    return pl.pallas_call(
        _ssm_prep_kernel,
        out_shape=[jax.ShapeDtypeStruct((g, CHUNK_W, CHUNK_W), BF16),
                   jax.ShapeDtypeStruct((g, CHUNK_W, 2 * STATE_DIM), BF16),
                   jax.ShapeDtypeStruct((g, CHUNK_W, 2 * STATE_DIM), BF16),
                   jax.ShapeDtypeStruct((g, N_SCAN_STEPS, 2 * STATE_DIM), F32),
                   jax.ShapeDtypeStruct((g, N_SCAN_STEPS, 2 * STATE_DIM), F32),
                   jax.ShapeDtypeStruct((g, SCAN_BLOCK, 2 * STATE_DIM), F32),
                   jax.ShapeDtypeStruct((g, SCAN_BLOCK, 2 * STATE_DIM), F32)],
        grid=(g // gb,),
        in_specs=[vec, vec, vec, mat, mat, mat, mat],
        out_specs=[pl.BlockSpec((gb, CHUNK_W, CHUNK_W), lambda i: (i, 0, 0)),
                   pl.BlockSpec((gb, CHUNK_W, 2 * STATE_DIM), lambda i: (i, 0, 0)),
                   pl.BlockSpec((gb, CHUNK_W, 2 * STATE_DIM), lambda i: (i, 0, 0)),
                   pl.BlockSpec((gb, N_SCAN_STEPS, 2 * STATE_DIM), lambda i: (i, 0, 0)),
                   pl.BlockSpec((gb, N_SCAN_STEPS, 2 * STATE_DIM), lambda i: (i, 0, 0)),
                   pl.BlockSpec((gb, SCAN_BLOCK, 2 * STATE_DIM), lambda i: (i, 0, 0)),
                   pl.BlockSpec((gb, SCAN_BLOCK, 2 * STATE_DIM), lambda i: (i, 0, 0))],
        compiler_params=_params(1),
        name="ssm_prep",
    )(row(lam_re), row(lam_im), ldt, btr, bti, c_re, c_im)


XPOSE_ROWS = 256


def _step_slot(step, group):
    half, k = divmod(step, GROUPS_PER_BLOCK)
    return half * GROUPS_PER_BLOCK + (k + group) % GROUPS_PER_BLOCK


def _slot_masks(shape):
    slot = lax.broadcasted_iota(jnp.int32, shape, 1) // GROUP_SIZE
    return [slot == p for p in range(GROUPS_PER_BLOCK)]


def _merge_slots(pieces, masks):
    out = pieces[0]
    for p in range(1, GROUPS_PER_BLOCK):
        out = jnp.where(masks[p], pieces[p], out)
    return out


def _steps_to_groups(xs):
    masks = _slot_masks(xs[0].shape)
    n = GROUPS_PER_BLOCK
    rot = [xs[k] if k == 0 else _lane_roll(xs[k], GROUP_SIZE * k) for k in range(n)]
    return [_merge_slots([rot[(p - g) % n] for p in range(n)], masks) for g in range(n)]


def _groups_to_steps(ys):
    masks = _slot_masks(ys[0].shape)
    n = GROUPS_PER_BLOCK
    out = []
    for k in range(n):
        z = _merge_slots([ys[(p - k) % n] for p in range(n)], masks)
        out.append(z if k == 0 else _lane_roll(z, LANES - GROUP_SIZE * k))
    return out


def _cmul(x, mr, mi):
    return x * mr + pltpu.roll(x, STATE_DIM, 1) * mi


def _ssm_kernel(x_ref, m_ref, be_ref, cet_ref, apr_ref, api_ref, bpr_ref, bpi_ref, d_ref, wg_ref, wo_ref,
                o_ref, wgb_ref, wob_ref, uf_ref, yf_ref, xs_ref, cs_ref, *, batch, n_chunks):
    wgb_ref[...] = wg_ref[...].astype(BF16)
    wob_ref[...] = wo_ref[...].astype(BF16)
    rows = batch * n_chunks
    n_blocks = n_chunks // SCAN_BLOCK

    def gather(i, carry):
        r0 = pl.multiple_of(i * XPOSE_ROWS, XPOSE_ROWS)
        for half in range(2):
            vs = [x_ref[GROUPS_PER_BLOCK * half + k, pl.ds(r0, XPOSE_ROWS), :]
                  for k in range(GROUPS_PER_BLOCK)]
            ts = _steps_to_groups(vs)
            for g in range(GROUPS_PER_BLOCK):
                uf_ref[g, pl.ds(r0, XPOSE_ROWS), half * LANES:(half + 1) * LANES] = ts[g]
        return carry

    lax.fori_loop(0, rows // XPOSE_ROWS, gather, 0)

    n_all = rows // SCAN_BLOCK
    row = lax.broadcasted_iota(jnp.int32, (rows, 2 * STATE_DIM), 0)
    in_block = row & (SCAN_BLOCK - 1)
    brow = lax.broadcasted_iota(jnp.int32, (n_all, 2 * STATE_DIM), 0)
    in_seq = brow & (n_blocks - 1)

    def shift_rows(x, d, pos):
        return jnp.where(pos >= d, pltpu.roll(x, d, 0), 0.0)

    def one_group(g, carry):
        u = uf_ref[g]
        y = jnp.dot(u, m_ref[g], preferred_element_type=F32)
        x = jnp.dot(u, be_ref[g], preferred_element_type=F32)
        apr, api = apr_ref[g], api_ref[g]
        for k in range(N_LOCAL_STEPS):
            x = x + _cmul(shift_rows(x, 1 << k, in_block), apr[k:k + 1], api[k:k + 1])
        xs_ref[...] = x
        c = xs_ref[pl.ds(SCAN_BLOCK - 1, n_all, stride=SCAN_BLOCK), :]
        for k in range(N_LOCAL_STEPS, N_SCAN_STEPS):
            c = c + _cmul(shift_rows(c, 1 << (k - N_LOCAL_STEPS), in_seq), apr[k:k + 1], api[k:k + 1])
        cs_ref[...] = shift_rows(c, 1, in_seq)
        bpr, bpi = bpr_ref[g], bpi_ref[g]
        carried = [_cmul(jnp.broadcast_to(cs_ref[mblk:mblk + 1, :], (SCAN_BLOCK, 2 * STATE_DIM)), bpr, bpi)
                   for mblk in range(n_all)]
        e = (shift_rows(x, 1, in_block) + jnp.concatenate(carried, axis=0)).astype(BF16)
        y = y + lax.dot_general(e, cet_ref[g], NT, preferred_element_type=F32)
        y = y + d_ref[g] * u.astype(F32)
        yf_ref[g] = jax.nn.gelu(y).astype(yf_ref.dtype)
        return carry

    lax.fori_loop(0, GROUPS_PER_BLOCK, one_group, 0, unroll=4)

    def scatter(i, carry):
        r0 = pl.multiple_of(i * XPOSE_ROWS, XPOSE_ROWS)
        for half in range(2):
            vs = [yf_ref[g, pl.ds(r0, XPOSE_ROWS), half * LANES:(half + 1) * LANES]
                  for g in range(GROUPS_PER_BLOCK)]
            ts = _groups_to_steps(vs)
            for k in range(GROUPS_PER_BLOCK):
                o_ref[GROUPS_PER_BLOCK * half + k, pl.ds(r0, XPOSE_ROWS), :] = ts[k]
        return carry

    lax.fori_loop(0, rows // XPOSE_ROWS, scatter, 0)


def _ssm(ug, m, be, cet, apr, api, bpr, bpi, dvec, w_glu, w_out, batch):
    _, rows, _ = ug.shape
    n_chunks = rows // batch
    gb = GROUPS_PER_BLOCK
    steps = N_GROUPS // gb
    col_blk = pl.BlockSpec((CHUNK, rows, LANES), lambda i: (0, 0, i))
    grp = lambda last2: pl.BlockSpec((gb,) + last2, lambda i: (i, 0, 0))
    slab = lambda w: pl.BlockSpec((w.shape[0] // steps, w.shape[1]), lambda i: (i, 0))
    return pl.pallas_call(
        functools.partial(_ssm_kernel, batch=batch, n_chunks=n_chunks),
        out_shape=[jax.ShapeDtypeStruct((CHUNK, rows, SSM_WIDTH), BF16),
                   jax.ShapeDtypeStruct(w_glu.shape, BF16),
                   jax.ShapeDtypeStruct(w_out.shape, BF16)],
        grid=(steps,),
        in_specs=[col_blk, grp((CHUNK_W, CHUNK_W)), grp((CHUNK_W, 2 * STATE_DIM)),
                  grp((CHUNK_W, 2 * STATE_DIM)), grp((N_SCAN_STEPS, 2 * STATE_DIM)),
                  grp((N_SCAN_STEPS, 2 * STATE_DIM)), grp((SCAN_BLOCK, 2 * STATE_DIM)),
                  grp((SCAN_BLOCK, 2 * STATE_DIM)), grp((1, CHUNK_W)), slab(w_glu), slab(w_out)],
        out_specs=[col_blk, slab(w_glu), slab(w_out)],
        scratch_shapes=[pltpu.VMEM((gb, rows, CHUNK_W), BF16),
                        pltpu.VMEM((gb, rows, CHUNK_W), BF16),
                        pltpu.VMEM((rows, 2 * STATE_DIM), F32),
                        pltpu.VMEM((rows // SCAN_BLOCK, 2 * STATE_DIM), F32)],
        compiler_params=_params(1),
        name="ssm_chunked",
    )(ug, m, be, cet, apr, api, bpr, bpi, dvec, w_glu, w_out)


def kernel(x, norm_g, attn_w_in, attn_q_norm_g, attn_k_norm_g, attn_sinks, attn_w_out, ssm_w_in,
           ssm_log_dt, ssm_lam_re, ssm_lam_im, ssm_b_re, ssm_b_im, ssm_c_re, ssm_c_im, ssm_d,
           ssm_w_glu, ssm_w_out):
    batch, seq, d = x.shape
    m = batch * seq
    assert d == D_MODEL and seq % BLOCK == 0 and seq // CHUNK == 1 << N_SCAN_STEPS
    x2 = x.reshape(m, d)

    q_gain = jnp.tile(attn_q_norm_g[0].astype(F32) * (HEAD_DIM ** -0.5 * LOG2E), IN_TILE // HEAD_DIM)
    k_gain = jnp.tile(attn_k_norm_g[0].astype(F32), N_KV_HEADS)

    hn0 = _rmsnorm(x2, norm_g[0])
    proj = _attn_in_proj(hn0, attn_w_in[0], q_gain.reshape(1, IN_TILE), k_gain.reshape(1, KV_WIDTH))
    og = _attention(proj, attn_sinks[0].astype(F32), batch, seq)
    h1 = _attn_out_proj(og, attn_w_out[0].reshape(N_KV_HEADS, GROUP_W, D_MODEL), x2)

    nj = m // CHUNK
    hn1 = _rmsnorm_chunk_major(h1, norm_g[1])
    ug = _ssm_in_proj(hn1, ssm_w_in[0])
    mt, be, cet, apr, api, bpr, bpi = _ssm_prep(ssm_lam_re[0], ssm_lam_im[0], ssm_log_dt[0],
                                      ssm_b_re[0], ssm_b_im[0], ssm_c_re[0], ssm_c_im[0])
    dvec = jnp.tile(ssm_d[0].astype(F32).reshape(N_GROUPS, 1, GROUP_SIZE), (1, 1, CHUNK))
    y, w_glu, w_out = _ssm(ug.reshape(CHUNK, nj, 2 * SSM_WIDTH), mt, be, cet, apr, api, bpr, bpi, dvec,
                           ssm_w_glu[0], ssm_w_out[0], batch)
    y = y.reshape(m, SSM_WIDTH)
    p = _glu(y, w_glu, ug)
    out = _ssm_out_proj(p, w_out, h1)
    return out.reshape(batch, seq, d)
```

```python
import functools

import jax
import jax.numpy as jnp
from jax import lax
from jax.experimental import pallas as pl
from jax.experimental.pallas import tpu as pltpu

F32 = jnp.float32
BF16 = jnp.bfloat16

D_MODEL = 4096
HEAD_DIM = 64
N_Q_HEADS = 64
N_KV_HEADS = 8
Q_PER_KV = 8
N_PAIRS = Q_PER_KV // 2
KV_WIDTH = N_KV_HEADS * HEAD_DIM
GROUP_W = Q_PER_KV * HEAD_DIM
BLOCK = 128
STACK = Q_PER_KV * BLOCK
SSM_WIDTH = 8192
GROUP_SIZE = 16
N_GROUPS = 512
STATE_DIM = 64
CHUNK = 16
CHUNK_W = CHUNK * GROUP_SIZE
RMS_EPS = 1e-6
NEG_INF = -1e30

LANES = 128
VMEM_LIMIT = 60 * 1024 * 1024
MXU_W = 256
GROUPS_PER_BLOCK = LANES // GROUP_SIZE

NT = (((1,), (1,)), ((), ()))


def _lane_roll(x, shift):
    return jnp.concatenate([x[:, LANES - shift:], x[:, :LANES - shift]], axis=1)


def _sigmoid(x):
    return 0.5 * jnp.tanh(0.5 * x) + 0.5


def _params(n_grid_dims):
    return pltpu.CompilerParams(
        dimension_semantics=("arbitrary",) * n_grid_dims, vmem_limit_bytes=VMEM_LIMIT)


def _rms(x, g):
    var = jnp.mean(x * x, axis=-1, keepdims=True)
    return x * lax.rsqrt(var + RMS_EPS) * g


def _rmsnorm_kernel(x_ref, g_ref, o_ref):
    o_ref[...] = _rms(x_ref[...], g_ref[...]).astype(o_ref.dtype)


def _rmsnorm(x, g, tm=256):
    m, d = x.shape
    return pl.pallas_call(
        _rmsnorm_kernel,
        out_shape=jax.ShapeDtypeStruct((m, d), BF16),
        grid=(m // tm,),
        in_specs=[pl.BlockSpec((tm, d), lambda i: (i, 0)),
                  pl.BlockSpec((1, d), lambda i: (0, 0))],
        out_specs=pl.BlockSpec((tm, d), lambda i: (i, 0)),
        compiler_params=_params(1),
        name="rmsnorm",
    )(x, g.reshape(1, d))


def _rmsnorm_chunk_major_kernel(x_ref, g_ref, p_ref, o_ref):
    hn = _rms(x_ref[...], g_ref[...]).astype(BF16)
    moved = jnp.dot(p_ref[...], hn, preferred_element_type=F32).astype(o_ref.dtype)
    nj = hn.shape[0] // CHUNK
    for s in range(CHUNK):
        o_ref[s] = moved[s * nj:(s + 1) * nj]


def _rmsnorm_chunk_major(x, g, tm=256):
    m, d = x.shape
    nj = tm // CHUNK
    src = (jnp.arange(tm) % nj) * CHUNK + jnp.arange(tm) // nj
    perm = (src[:, None] == jnp.arange(tm)[None, :]).astype(BF16)
    out = pl.pallas_call(
        _rmsnorm_chunk_major_kernel,
        out_shape=jax.ShapeDtypeStruct((CHUNK, m // CHUNK, d), BF16),
        grid=(m // tm,),
        in_specs=[pl.BlockSpec((tm, d), lambda i: (i, 0)),
                  pl.BlockSpec((1, d), lambda i: (0, 0)),
                  pl.BlockSpec((tm, tm), lambda i: (0, 0))],
        out_specs=pl.BlockSpec((CHUNK, nj, d), lambda i: (0, i, 0)),
        compiler_params=_params(1),
        name="rmsnorm_chunk_major",
    )(x, g.reshape(1, d), perm)
    return out.reshape(m, d)


def _head_rmsnorm(acc, ones_bd, gain):
    outs = []
    for c in range(acc.shape[1] // MXU_W):
        a = acc[:, c * MXU_W:(c + 1) * MXU_W]
        ss = jnp.dot((a * a).astype(BF16), ones_bd, preferred_element_type=F32)
        inv = lax.rsqrt(ss * (1.0 / HEAD_DIM) + RMS_EPS)
        outs.append(a * inv * gain[:, c * MXU_W:(c + 1) * MXU_W])
    return outs[0] if len(outs) == 1 else jnp.concatenate(outs, axis=1)


def _ones_block_diag():
    r = jnp.arange(MXU_W) // HEAD_DIM
    return (r[:, None] == r[None, :]).astype(BF16)


IN_TILE = 1024
N_Q_TILES = D_MODEL // IN_TILE
KV_TILE = N_Q_TILES
GATE_TILE0 = KV_TILE + 1
N_IN_TILES = (2 * D_MODEL + 2 * KV_WIDTH) // IN_TILE


def _attn_in_kernel(x_ref, w_ref, e_ref, qg_ref, kg_ref, o_ref, wb_ref):
    jj, i = pl.program_id(0), pl.program_id(1)
    tile = jj - 1
    chunk = w_ref.shape[0]
    fill = jj % 2
    rows = pl.ds(pl.multiple_of(i * chunk, chunk), chunk)

    def stage_next_tile():
        wb_ref[fill, rows, :] = w_ref[...].astype(BF16)

    def product():
        return jnp.dot(x_ref[...], wb_ref[1 - fill], preferred_element_type=F32)

    @pl.when(tile < 0)
    def _():
        stage_next_tile()

    @pl.when(jnp.logical_and(tile >= 0, tile < KV_TILE))
    def _():
        stage_next_tile()
        o_ref[...] = _head_rmsnorm(product(), e_ref[...], qg_ref[...]).astype(o_ref.dtype)

    @pl.when(tile == KV_TILE)
    def _():
        stage_next_tile()
        acc = product()
        kn = _head_rmsnorm(acc[:, :KV_WIDTH], e_ref[...], kg_ref[...])
        for h in range(N_KV_HEADS):
            o_ref[:, h * LANES:h * LANES + HEAD_DIM] = kn[:, h * HEAD_DIM:(h + 1) * HEAD_DIM].astype(o_ref.dtype)
            o_ref[:, h * LANES + HEAD_DIM:(h + 1) * LANES] = acc[
                :, KV_WIDTH + h * HEAD_DIM:KV_WIDTH + (h + 1) * HEAD_DIM].astype(o_ref.dtype)

    @pl.when(tile > KV_TILE)
    def _():
        stage_next_tile()
        o_ref[...] = product().astype(o_ref.dtype)


def _attn_in_proj(hn, w_in, q_gain, k_gain, tm=1024):
    m, k = hn.shape
    ni = m // tm
    chunk = k // ni
    row_tile = lambda jj, i: jnp.where(jj == 0, 0, i)
    const = lambda jj, i: (0, 0)
    return pl.pallas_call(
        _attn_in_kernel,
        out_shape=jax.ShapeDtypeStruct((N_IN_TILES, m, IN_TILE), BF16),
        grid=(N_IN_TILES + 1, ni),
        in_specs=[pl.BlockSpec((tm, k), lambda jj, i: (row_tile(jj, i), 0)),
                  pl.BlockSpec((chunk, IN_TILE), lambda jj, i: (i, jnp.minimum(jj, N_IN_TILES - 1))),
                  pl.BlockSpec((MXU_W, MXU_W), const),
                  pl.BlockSpec((1, IN_TILE), const),
                  pl.BlockSpec((1, KV_WIDTH), const)],
        out_specs=pl.BlockSpec((None, tm, IN_TILE),
                               lambda jj, i: (jnp.maximum(jj - 1, 0), row_tile(jj, i), 0)),
        scratch_shapes=[pltpu.VMEM((2, k, IN_TILE), BF16)],
        compiler_params=_params(2),
        name="attn_in_proj",
    )(hn, w_in, _ones_block_diag(), q_gain, k_gain)


def _attn_kernel(sink_ref, alibi_ref, q_ref, kvc_ref, kvp_ref, g0_ref, g1_ref, g2_ref, g3_ref, o_ref):
    gate_refs = (g0_ref, g1_ref, g2_ref, g3_ref)
    row = lax.broadcasted_iota(jnp.int32, (BLOCK, BLOCK), 0)
    col = lax.broadcasted_iota(jnp.int32, (BLOCK, BLOCK), 1)
    lower = col <= row
    left = col < HEAD_DIM
    ones = jnp.ones((2 * BLOCK, LANES), BF16)

    def split_kv(kv):
        swapped = _lane_roll(kv, HEAD_DIM)
        zero = jnp.zeros_like(kv)
        return jnp.where(left, kv, zero), jnp.where(left, zero, swapped), jnp.where(left, swapped, kv)

    for kvh in range(N_KV_HEADS):
        tile, col0 = kvh // 2, (kvh % 2) * GROUP_W
        kc_even, kc_odd, vvc = split_kv(kvc_ref[:, kvh * LANES:(kvh + 1) * LANES])
        kp_even, kp_odd, vvp = split_kv(kvp_ref[:, kvh * LANES:(kvh + 1) * LANES])
        qp = jnp.concatenate(
            [q_ref[tile, :, col0 + a * LANES:col0 + (a + 1) * LANES] for a in range(N_PAIRS)], axis=0)
        keys = jnp.concatenate([kc_even, kc_odd, kp_even, kp_odd], axis=0)
        s_all = lax.dot_general(qp, keys, NT, preferred_element_type=F32)
        v_ones = jnp.concatenate([jnp.concatenate([vvc, vvp], axis=0), ones], axis=1)
        ps, sink_terms = [], []
        for par in range(2):
            for a in range(N_PAIRS):
                rows = slice(a * BLOCK, (a + 1) * BLOCK)
                sc = s_all[rows, par * LANES:(par + 1) * LANES]
                sp = s_all[rows, (2 + par) * LANES:(3 + par) * LANES]
                blk = par * N_PAIRS + a
                s = jnp.where(lower, sc, sp) + alibi_ref[kvh, blk * BLOCK:(blk + 1) * BLOCK, :]
                sink = sink_ref[kvh * Q_PER_KV + 2 * a + par]
                m = jnp.maximum(jnp.max(s, axis=-1, keepdims=True), sink)
                p = jnp.exp2(s - m)
                ps.append(jnp.concatenate([jnp.where(lower, p, 0.0), jnp.where(lower, 0.0, p)],
                                          axis=1).astype(BF16))
                sink_terms.append(jnp.exp2(sink - m))
        nd = jnp.dot(jnp.concatenate(ps, axis=0), v_ones, preferred_element_type=F32)
        for a in range(N_PAIRS):
            o = []
            for par in range(2):
                blk = par * N_PAIRS + a
                r = nd[blk * BLOCK:(blk + 1) * BLOCK]
                o.append(r[:, :LANES] / (r[:, LANES:] + sink_terms[blk]))
            gt = gate_refs[tile][:, col0 + a * LANES:col0 + (a + 1) * LANES].astype(F32)
            o_ref[kvh, :, a * LANES:(a + 1) * LANES] = (
                jnp.where(left, o[0], o[1]) * (gt * _sigmoid(gt))).astype(o_ref.dtype)


LOG2E = 1.4426950408889634


def _alibi_table():
    qi = jnp.arange(BLOCK)[:, None]
    kj = jnp.arange(BLOCK)[None, :]
    dist = jnp.where(kj <= qi, qi - kj, BLOCK + qi - kj).astype(F32)
    slopes = jnp.exp2(-8.0 * jnp.arange(1, N_Q_HEADS + 1, dtype=F32) / N_Q_HEADS)
    slopes = slopes.reshape(N_KV_HEADS, N_PAIRS, 2).transpose(0, 2, 1)
    bias = (-LOG2E * slopes[:, :, :, None, None] * dist)
    first = jnp.where(kj <= qi, bias, NEG_INF)
    return jnp.stack([first, bias]).reshape(2, N_KV_HEADS, STACK, BLOCK)


def _attention(proj, sinks, batch, seq):
    nb = seq // BLOCK
    m = batch * seq
    cur = lambda b, n: b * nb + n
    prev = lambda b, n: b * nb + jnp.maximum(n - 1, 0)
    tile = lambda t, rows: pl.BlockSpec((None, BLOCK, IN_TILE), lambda b, n: (t, rows(b, n), 0))
    return pl.pallas_call(
        _attn_kernel,
        out_shape=jax.ShapeDtypeStruct((N_KV_HEADS, m, GROUP_W), BF16),
        grid=(batch, nb),
        in_specs=[pl.BlockSpec(memory_space=pltpu.SMEM),
                  pl.BlockSpec((None, N_KV_HEADS, STACK, BLOCK), lambda b, n: (jnp.minimum(n, 1), 0, 0, 0)),
                  pl.BlockSpec((N_Q_TILES, BLOCK, IN_TILE), lambda b, n: (0, cur(b, n), 0)),
                  tile(KV_TILE, cur), tile(KV_TILE, prev)]
                 + [tile(GATE_TILE0 + t, cur) for t in range(N_Q_TILES)],
        out_specs=pl.BlockSpec((N_KV_HEADS, BLOCK, GROUP_W), lambda b, n: (0, cur(b, n), 0)),
        compiler_params=_params(2),
        name="band_attention",
    )(sinks * LOG2E, _alibi_table(), *([proj] * (3 + N_Q_TILES)))


def _attn_out_kernel(x_ref, w_ref, r_ref, o_ref, wb_ref):
    jj, i = pl.program_id(0), pl.program_id(1)
    fill = jj % 2

    @pl.when(jj == 0)
    def _():
        wb_ref[fill, i] = w_ref[...].astype(BF16)

    @pl.when(jj > 0)
    def _():
        wb_ref[fill, i] = w_ref[...].astype(BF16)
        acc = r_ref[...]
        for h in range(N_KV_HEADS):
            acc = acc + jnp.dot(x_ref[h], wb_ref[1 - fill, h], preferred_element_type=F32)
        o_ref[...] = acc


def _attn_out_proj(og, w, resid, tm=1024, tn=1024):
    _, m, _ = og.shape
    n = w.shape[2]
    ni, nj = m // tm, n // tn
    assert ni == N_KV_HEADS
    row_tile = lambda jj, i: jnp.where(jj == 0, 0, i)
    out_blk = pl.BlockSpec((tm, tn), lambda jj, i: (row_tile(jj, i), jnp.maximum(jj - 1, 0)))
    return pl.pallas_call(
        _attn_out_kernel,
        out_shape=jax.ShapeDtypeStruct((m, n), F32),
        grid=(nj + 1, ni),
        in_specs=[pl.BlockSpec((N_KV_HEADS, tm, GROUP_W), lambda jj, i: (0, row_tile(jj, i), 0)),
                  pl.BlockSpec((None, GROUP_W, tn), lambda jj, i: (i, 0, jnp.minimum(jj, nj - 1))),
                  out_blk],
        out_specs=out_blk,
        scratch_shapes=[pltpu.VMEM((2, N_KV_HEADS, GROUP_W, tn), BF16)],
        compiler_params=_params(2),
        name="attn_out_proj",
    )(og, w, resid)


def _ssm_in_kernel(x_ref, w_ref, o_ref, wb_ref):
    jj, i = pl.program_id(0), pl.program_id(1)
    chunk = w_ref.shape[0]
    fill = jj % 2
    rows = pl.ds(pl.multiple_of(i * chunk, chunk), chunk)

    @pl.when(jj == 0)
    def _():
        wb_ref[fill, rows, :] = w_ref[...].astype(BF16)

    @pl.when(jj > 0)
    def _():
        wb_ref[fill, rows, :] = w_ref[...].astype(BF16)
        o_ref[...] = jnp.dot(x_ref[...], wb_ref[1 - fill], preferred_element_type=F32).astype(o_ref.dtype)


def _ssm_in_proj(x, w, tm=1024, tn=1024):
    m, k = x.shape
    n = w.shape[1]
    ni, nj = m // tm, n // tn
    chunk = k // ni
    row_tile = lambda jj, i: jnp.where(jj == 0, 0, i)
    return pl.pallas_call(
        _ssm_in_kernel,
        out_shape=jax.ShapeDtypeStruct((m, n), BF16),
        grid=(nj + 1, ni),
        in_specs=[pl.BlockSpec((tm, k), lambda jj, i: (row_tile(jj, i), 0)),
                  pl.BlockSpec((chunk, tn), lambda jj, i: (i, jnp.minimum(jj, nj - 1)))],
        out_specs=pl.BlockSpec((tm, tn), lambda jj, i: (row_tile(jj, i), jnp.maximum(jj - 1, 0))),
        scratch_shapes=[pltpu.VMEM((2, k, tn), BF16)],
        compiler_params=_params(2),
        name="ssm_in_proj",
    )(x, w)


def _glu_kernel(x_ref, w_ref, y_ref, gate_ref, o_ref):
    gt = gate_ref[...]
    gated = y_ref[...] * (gt * _sigmoid(gt))
    z = jnp.dot(x_ref[...], w_ref[...], preferred_element_type=F32)
    o_ref[...] = gated * _sigmoid(z.astype(BF16))


def _glu(y, w, ug, tm=1024, tn=512):
    m, k = y.shape
    n = w.shape[1]
    gate_off = n // tn
    return pl.pallas_call(
        _glu_kernel,
        out_shape=jax.ShapeDtypeStruct((m, n), BF16),
        grid=(m // tm, n // tn),
        in_specs=[pl.BlockSpec((tm, k), lambda i, j: (i, 0)),
                  pl.BlockSpec((k, tn), lambda i, j: (0, j)),
                  pl.BlockSpec((tm, tn), lambda i, j: (i, j)),
                  pl.BlockSpec((tm, tn), lambda i, j: (i, j + gate_off))],
        out_specs=pl.BlockSpec((tm, tn), lambda i, j: (i, j)),
        compiler_params=_params(2),
        name="ssm_glu",
    )(y, w, y, ug)


def _ssm_out_kernel(x_ref, w_ref, r_ref, o_ref, cols_ref):
    nj = x_ref.shape[1]
    x = x_ref[...].reshape(CHUNK * nj, x_ref.shape[2])
    acc = jnp.dot(x, w_ref[...], preferred_element_type=F32)
    for c in range(acc.shape[1] // LANES):
        for s in range(CHUNK):
            cols_ref[c, pl.ds(s, nj, stride=CHUNK), :] = acc[
                s * nj:(s + 1) * nj, c * LANES:(c + 1) * LANES]
        o_ref[:, c * LANES:(c + 1) * LANES] = cols_ref[c] + r_ref[:, c * LANES:(c + 1) * LANES]


def _ssm_out_proj(x, w, resid, tm=1024, tn=512):
    m, k = x.shape
    n = w.shape[1]
    nj = tm // CHUNK
    return pl.pallas_call(
        _ssm_out_kernel,
        out_shape=jax.ShapeDtypeStruct((m, n), F32),
        grid=(m // tm, n // tn),
        in_specs=[pl.BlockSpec((CHUNK, nj, k), lambda i, j: (0, i, 0)),
                  pl.BlockSpec((k, tn), lambda i, j: (0, j)),
                  pl.BlockSpec((tm, tn), lambda i, j: (i, j))],
        out_specs=pl.BlockSpec((tm, tn), lambda i, j: (i, j)),
        scratch_shapes=[pltpu.VMEM((tn // LANES, tm, LANES), F32)],
        compiler_params=_params(2),
        name="ssm_out_proj",
    )(x.reshape(CHUNK, m // CHUNK, k), w, resid)


N_SCAN_STEPS = 8
SCAN_BLOCK = 8
N_LOCAL_STEPS = 3


def _ssm_prep_kernel(lr_ref, li_ref, ldt_ref, btr_ref, bti_ref, cr_ref, ci_ref,
                     m_ref, be_ref, cet_ref, apr_ref, api_ref, bpr_ref, bpi_ref):
    tau = lax.broadcasted_iota(jnp.int32, (24, 2 * STATE_DIM), 0).astype(F32)
    quarter = jnp.where(lax.broadcasted_iota(jnp.int32, (24, 2 * STATE_DIM), 1) < STATE_DIM, 0.0, 0.5 * jnp.pi)
    lane = lax.broadcasted_iota(jnp.int32, (GROUP_SIZE, CHUNK_W), 1)

    def one_group(g, carry):
        lr = lr_ref[g]
        li = li_ref[g]
        dt = jnp.exp(ldt_ref[g])
        lr2 = jnp.concatenate([lr, lr], axis=1)
        li2 = jnp.concatenate([li, li], axis=1)
        dt2 = jnp.concatenate([dt, dt], axis=1)
        powers = jnp.exp(tau * (lr2 * dt2)) * jnp.cos(tau * (li2 * dt2) - quarter)
        pr = powers[:, :STATE_DIM]
        pi = powers[:, STATE_DIM:]
        ar, ai = pr[1:2], pi[1:2]
        den = lr * lr + li * li
        xr = ar - 1.0
        wr = (xr * lr + ai * li) / den
        wi = (ai * lr - xr * li) / den
        btr, bti = btr_ref[g], bti_ref[g]
        br = btr * wr - bti * wi
        bi = btr * wi + bti * wr
        cr, ci = cr_ref[g], ci_ref[g]

        ca_r, ca_i = [], []
        for t in range(CHUNK + 1):
            ca_r.append(cr * pr[t:t + 1] - ci * pi[t:t + 1])
            ca_i.append(cr * pi[t:t + 1] + ci * pr[t:t + 1])

        rt = jnp.concatenate([jnp.concatenate(ca_r[:CHUNK], axis=0),
                              jnp.concatenate(ca_i[:CHUNK], axis=0)], axis=1)
        bcat = jnp.concatenate([br, -bi], axis=1)
        kp = lax.dot_general(bcat.astype(BF16), rt.astype(BF16), NT,
                             preferred_element_type=F32)
        for s in range(CHUNK):
            blk = kp if s == 0 else pltpu.roll(kp, GROUP_SIZE * s, 1)
            blk = jnp.where(lane >= GROUP_SIZE * s, blk, 0.0)
            blk = jnp.concatenate([pltpu.roll(blk[:, :LANES], GROUP_SIZE * g, 1),
                                   pltpu.roll(blk[:, LANES:], GROUP_SIZE * g, 1)], axis=1)
            dst = pl.ds(pl.multiple_of(_step_slot(s, g) * GROUP_SIZE, GROUP_SIZE), GROUP_SIZE)
            m_ref[g, dst, :] = blk.astype(m_ref.dtype)
            e = CHUNK - 1 - s
            be_re = br * pr[e:e + 1] - bi * pi[e:e + 1]
            be_im = br * pi[e:e + 1] + bi * pr[e:e + 1]
            be_ref[g, dst, :] = jnp.concatenate(
                [be_re, be_im], axis=1).astype(be_ref.dtype)
            cet_ref[g, dst, :] = jnp.concatenate(
                [ca_r[s + 1], -ca_i[s + 1]], axis=1).astype(cet_ref.dtype)

        sq_r, sq_i = [pr[CHUNK:CHUNK + 1]], [pi[CHUNK:CHUNK + 1]]
        for _ in range(N_SCAN_STEPS - 1):
            xr, xi = sq_r[-1], sq_i[-1]
            sq_r.append(xr * xr - xi * xi)
            sq_i.append(2.0 * (xr * xi))
        mul_r, mul_i = [jnp.ones_like(sq_r[0]), sq_r[0]], [jnp.zeros_like(sq_r[0]), sq_i[0]]
        for _ in range(SCAN_BLOCK - 2):
            xr, xi = mul_r[-1], mul_i[-1]
            mul_r.append(xr * sq_r[0] - xi * sq_i[0])
            mul_i.append(xr * sq_i[0] + xi * sq_r[0])
        sr, si = jnp.concatenate(sq_r, axis=0), jnp.concatenate(sq_i, axis=0)
        rr, ri = jnp.concatenate(mul_r, axis=0), jnp.concatenate(mul_i, axis=0)
        apr_ref[g] = jnp.concatenate([sr, sr], axis=1)
        api_ref[g] = jnp.concatenate([-si, si], axis=1)
        bpr_ref[g] = jnp.concatenate([rr, rr], axis=1)
        bpi_ref[g] = jnp.concatenate([-ri, ri], axis=1)
        return carry

    lax.fori_loop(0, GROUPS_PER_BLOCK, one_group, 0, unroll=4)


def _ssm_prep(lam_re, lam_im, log_dt, b_re, b_im, c_re, c_im):
    g = N_GROUPS
    gb = GROUPS_PER_BLOCK
    row = lambda a: a.reshape(g, 1, STATE_DIM)
    ldt = jnp.broadcast_to(log_dt.reshape(g, 1, 1), (g, 1, STATE_DIM))
    btr = jnp.swapaxes(b_re, 1, 2)
    bti = jnp.swapaxes(b_im, 1, 2)
    vec = pl.BlockSpec((gb, 1, STATE_DIM), lambda i: (i, 0, 0))
    mat = pl.BlockSpec((gb, GROUP_SIZE, STATE_DIM), lambda i: (i, 0, 0))
    return pl.pallas_call(
        _ssm_prep_kernel,
        out_shape=[jax.ShapeDtypeStruct((g, CHUNK_W, CHUNK_W), BF16),
                   jax.ShapeDtypeStruct((g, CHUNK_W, 2 * STATE_DIM), BF16),
                   jax.ShapeDtypeStruct((g, CHUNK_W, 2 * STATE_DIM), BF16),
                   jax.ShapeDtypeStruct((g, N_SCAN_STEPS, 2 * STATE_DIM), F32),
                   jax.ShapeDtypeStruct((g, N_SCAN_STEPS, 2 * STATE_DIM), F32),
                   jax.ShapeDtypeStruct((g, SCAN_BLOCK, 2 * STATE_DIM), F32),
                   jax.ShapeDtypeStruct((g, SCAN_BLOCK, 2 * STATE_DIM), F32)],
        grid=(g // gb,),
        in_specs=[vec, vec, vec, mat, mat, mat, mat],
        out_specs=[pl.BlockSpec((gb, CHUNK_W, CHUNK_W), lambda i: (i, 0, 0)),
                   pl.BlockSpec((gb, CHUNK_W, 2 * STATE_DIM), lambda i: (i, 0, 0)),
                   pl.BlockSpec((gb, CHUNK_W, 2 * STATE_DIM), lambda i: (i, 0, 0)),
                   pl.BlockSpec((gb, N_SCAN_STEPS, 2 * STATE_DIM), lambda i: (i, 0, 0)),
                   pl.BlockSpec((gb, N_SCAN_STEPS, 2 * STATE_DIM), lambda i: (i, 0, 0)),
                   pl.BlockSpec((gb, SCAN_BLOCK, 2 * STATE_DIM), lambda i: (i, 0, 0)),
                   pl.BlockSpec((gb, SCAN_BLOCK, 2 * STATE_DIM), lambda i: (i, 0, 0))],
        compiler_params=_params(1),
        name="ssm_prep",
    )(row(lam_re), row(lam_im), ldt, btr, bti, c_re, c_im)


XPOSE_ROWS = 256


def _step_slot(step, group):
    half, k = divmod(step, GROUPS_PER_BLOCK)
    return half * GROUPS_PER_BLOCK + (k + group) % GROUPS_PER_BLOCK


def _slot_masks(shape):
    slot = lax.broadcasted_iota(jnp.int32, shape, 1) // GROUP_SIZE
    return [slot == p for p in range(GROUPS_PER_BLOCK)]


def _merge_slots(pieces, masks):
    out = pieces[0]
    for p in range(1, GROUPS_PER_BLOCK):
        out = jnp.where(masks[p], pieces[p], out)
    return out


def _steps_to_groups(xs):
    masks = _slot_masks(xs[0].shape)
    n = GROUPS_PER_BLOCK
    rot = [xs[k] if k == 0 else _lane_roll(xs[k], GROUP_SIZE * k) for k in range(n)]
    return [_merge_slots([rot[(p - g) % n] for p in range(n)], masks) for g in range(n)]


def _groups_to_steps(ys):
    masks = _slot_masks(ys[0].shape)
    n = GROUPS_PER_BLOCK
    out = []
    for k in range(n):
        z = _merge_slots([ys[(p - k) % n] for p in range(n)], masks)
        out.append(z if k == 0 else _lane_roll(z, LANES - GROUP_SIZE * k))
    return out


def _cmul(x, mr, mi):
    return x * mr + pltpu.roll(x, STATE_DIM, 1) * mi


def _ssm_kernel(x_ref, m_ref, be_ref, cet_ref, apr_ref, api_ref, bpr_ref, bpi_ref, d_ref, wg_ref, wo_ref,
                o_ref, wgb_ref, wob_ref, uf_ref, yf_ref, xs_ref, cs_ref, *, batch, n_chunks):
    rows = batch * n_chunks
    slab_rows = wg_ref.shape[0] // GROUPS_PER_BLOCK
    n_blocks = n_chunks // SCAN_BLOCK

    def gather(i, carry):
        r0 = pl.multiple_of(i * XPOSE_ROWS, XPOSE_ROWS)
        for half in range(2):
            vs = [x_ref[GROUPS_PER_BLOCK * half + k, pl.ds(r0, XPOSE_ROWS), :]
                  for k in range(GROUPS_PER_BLOCK)]
            ts = _steps_to_groups(vs)
            for g in range(GROUPS_PER_BLOCK):
                uf_ref[g, pl.ds(r0, XPOSE_ROWS), half * LANES:(half + 1) * LANES] = ts[g]
        return carry

    lax.fori_loop(0, rows // XPOSE_ROWS, gather, 0)

    n_all = rows // SCAN_BLOCK
    row = lax.broadcasted_iota(jnp.int32, (rows, 2 * STATE_DIM), 0)
    in_block = row & (SCAN_BLOCK - 1)
    brow = lax.broadcasted_iota(jnp.int32, (n_all, 2 * STATE_DIM), 0)
    in_seq = brow & (n_blocks - 1)

    def shift_rows(x, d, pos):
        return jnp.where(pos >= d, pltpu.roll(x, d, 0), 0.0)

    def one_group(g, carry):
        piece = pl.ds(pl.multiple_of(g * slab_rows, slab_rows), slab_rows)
        wgb_ref[piece, :] = wg_ref[piece, :].astype(BF16)
        wob_ref[piece, :] = wo_ref[piece, :].astype(BF16)
        u = uf_ref[g]
        y = jnp.dot(u, m_ref[g], preferred_element_type=F32)
        x = jnp.dot(u, be_ref[g], preferred_element_type=F32)
        apr, api = apr_ref[g], api_ref[g]
        for k in range(N_LOCAL_STEPS):
            x = x + _cmul(shift_rows(x, 1 << k, in_block), apr[k:k + 1], api[k:k + 1])
        xs_ref[...] = x
        c = xs_ref[pl.ds(SCAN_BLOCK - 1, n_all, stride=SCAN_BLOCK), :]
        for k in range(N_LOCAL_STEPS, N_SCAN_STEPS):
            c = c + _cmul(shift_rows(c, 1 << (k - N_LOCAL_STEPS), in_seq), apr[k:k + 1], api[k:k + 1])
        cs_ref[...] = shift_rows(c, 1, in_seq)
        bpr, bpi = bpr_ref[g], bpi_ref[g]
        carried = [_cmul(jnp.broadcast_to(cs_ref[mblk:mblk + 1, :], (SCAN_BLOCK, 2 * STATE_DIM)), bpr, bpi)
                   for mblk in range(n_all)]
        e = (shift_rows(x, 1, in_block) + jnp.concatenate(carried, axis=0)).astype(BF16)
        y = y + lax.dot_general(e, cet_ref[g], NT, preferred_element_type=F32)
        y = y + d_ref[g] * u.astype(F32)
        yf_ref[g] = jax.nn.gelu(y).astype(yf_ref.dtype)
        return carry

    lax.fori_loop(0, GROUPS_PER_BLOCK, one_group, 0, unroll=4)

    def scatter(i, carry):
        r0 = pl.multiple_of(i * XPOSE_ROWS, XPOSE_ROWS)
        for half in range(2):
            vs = [yf_ref[g, pl.ds(r0, XPOSE_ROWS), half * LANES:(half + 1) * LANES]
                  for g in range(GROUPS_PER_BLOCK)]
            ts = _groups_to_steps(vs)
            for k in range(GROUPS_PER_BLOCK):
                o_ref[GROUPS_PER_BLOCK * half + k, pl.ds(r0, XPOSE_ROWS), :] = ts[k]
        return carry

    lax.fori_loop(0, rows // XPOSE_ROWS, scatter, 0)


def _ssm(ug, m, be, cet, apr, api, bpr, bpi, dvec, w_glu, w_out, batch):
    _, rows, _ = ug.shape
    n_chunks = rows // batch
    gb = GROUPS_PER_BLOCK
    steps = N_GROUPS // gb
    col_blk = pl.BlockSpec((CHUNK, rows, LANES), lambda i: (0, 0, i))
    grp = lambda last2: pl.BlockSpec((gb,) + last2, lambda i: (i, 0, 0))
    slab = lambda w: pl.BlockSpec((w.shape[0] // steps, w.shape[1]), lambda i: (i, 0))
    return pl.pallas_call(
        functools.partial(_ssm_kernel, batch=batch, n_chunks=n_chunks),
        out_shape=[jax.ShapeDtypeStruct((CHUNK, rows, SSM_WIDTH), BF16),
                   jax.ShapeDtypeStruct(w_glu.shape, BF16),
                   jax.ShapeDtypeStruct(w_out.shape, BF16)],
        grid=(steps,),
        in_specs=[col_blk, grp((CHUNK_W, CHUNK_W)), grp((CHUNK_W, 2 * STATE_DIM)),
                  grp((CHUNK_W, 2 * STATE_DIM)), grp((N_SCAN_STEPS, 2 * STATE_DIM)),
                  grp((N_SCAN_STEPS, 2 * STATE_DIM)), grp((SCAN_BLOCK, 2 * STATE_DIM)),
                  grp((SCAN_BLOCK, 2 * STATE_DIM)), grp((1, CHUNK_W)), slab(w_glu), slab(w_out)],
        out_specs=[col_blk, slab(w_glu), slab(w_out)],
        scratch_shapes=[pltpu.VMEM((gb, rows, CHUNK_W), BF16),
                        pltpu.VMEM((gb, rows, CHUNK_W), BF16),
                        pltpu.VMEM((rows, 2 * STATE_DIM), F32),
                        pltpu.VMEM((rows // SCAN_BLOCK, 2 * STATE_DIM), F32)],
        compiler_params=_params(1),
        name="ssm_chunked",
    )(ug, m, be, cet, apr, api, bpr, bpi, dvec, w_glu, w_out)


def kernel(x, norm_g, attn_w_in, attn_q_norm_g, attn_k_norm_g, attn_sinks, attn_w_out, ssm_w_in,
           ssm_log_dt, ssm_lam_re, ssm_lam_im, ssm_b_re, ssm_b_im, ssm_c_re, ssm_c_im, ssm_d,
           ssm_w_glu, ssm_w_out):
    batch, seq, d = x.shape
    m = batch * seq
    assert d == D_MODEL and seq % BLOCK == 0 and seq // CHUNK == 1 << N_SCAN_STEPS
    x2 = x.reshape(m, d)

    q_gain = jnp.tile(attn_q_norm_g[0].astype(F32) * (HEAD_DIM ** -0.5 * LOG2E), IN_TILE // HEAD_DIM)
    k_gain = jnp.tile(attn_k_norm_g[0].astype(F32), N_KV_HEADS)

    hn0 = _rmsnorm(x2, norm_g[0])
    proj = _attn_in_proj(hn0, attn_w_in[0], q_gain.reshape(1, IN_TILE), k_gain.reshape(1, KV_WIDTH))
    og = _attention(proj, attn_sinks[0].astype(F32), batch, seq)
    h1 = _attn_out_proj(og, attn_w_out[0].reshape(N_KV_HEADS, GROUP_W, D_MODEL), x2)

    nj = m // CHUNK
    hn1 = _rmsnorm_chunk_major(h1, norm_g[1])
    ug = _ssm_in_proj(hn1, ssm_w_in[0])
    mt, be, cet, apr, api, bpr, bpi = _ssm_prep(ssm_lam_re[0], ssm_lam_im[0], ssm_log_dt[0],
                                      ssm_b_re[0], ssm_b_im[0], ssm_c_re[0], ssm_c_im[0])
    dvec = jnp.tile(ssm_d[0].astype(F32).reshape(N_GROUPS, 1, GROUP_SIZE), (1, 1, CHUNK))
    y, w_glu, w_out = _ssm(ug.reshape(CHUNK, nj, 2 * SSM_WIDTH), mt, be, cet, apr, api, bpr, bpi, dvec,
                           ssm_w_glu[0], ssm_w_out[0], batch)
    y = y.reshape(m, SSM_WIDTH)
    p = _glu(y, w_glu, ug)
    out = _ssm_out_proj(p, w_out, h1)
    return out.reshape(batch, seq, d)
```
